```python
import math
import jax, jax.numpy as jnp
from jax import lax
import numpy as np

D_MODEL = 2048
BATCH = 4
SEQ = 2048
DEPTH = 1
DEC_BATCH = 32
DEC_SEQ = 1
PAST_LEN = 8192
PAGE_SIZE = 128

N_META = 16
META_BLOCK = 128
Q_BLOCK = 128
A_HEADS = 8
A_DH = 64
A_DV = 2 * A_DH
A_QK = A_HEADS * 2 * A_DH
A_WIDTH = A_HEADS * A_DV
ROPE_DIM = A_DH // 4
ROPE_THETA = 500000.0
B_HEADS = 8
B_DK = 128
B_DV = 128
B_WIDTH = B_HEADS * B_DV
B_QKV = 2 * B_HEADS * B_DK + B_WIDTH
CONV_W = 4
CHUNK = 64
SPLITS = (A_QK, A_QK, A_WIDTH, A_WIDTH, B_QKV, B_WIDTH, B_HEADS, B_HEADS, D_MODEL, D_MODEL)
D_IN = 2 * A_QK + 2 * A_WIDTH + B_QKV + B_WIDTH + 2 * B_HEADS + 2 * D_MODEL
EPS = 1e-6
NEG = -1e30

kernel_name = 'hybrid_diffattn_gated_deltanet_step'


def rmsnorm(x, w):
    xf = x.astype(jnp.float32)
    y = xf * lax.rsqrt(jnp.mean(xf * xf, axis=-1, keepdims=True) + EPS)
    return (y * w.astype(jnp.float32)).astype(x.dtype)


def l2norm(x):
    xf = x.astype(jnp.float32)
    return xf * lax.rsqrt(jnp.sum(xf * xf, axis=-1, keepdims=True) + EPS)


def rope(x, pos):
    half = ROPE_DIM // 2
    inv_freq = ROPE_THETA ** (-jnp.arange(0, ROPE_DIM, 2, dtype=jnp.float32) / ROPE_DIM)
    ang = pos.astype(jnp.float32)[:, None] * inv_freq[None, :]
    cos = jnp.cos(ang)[:, None, None, :]
    sin = jnp.sin(ang)[:, None, None, :]
    xf = x.astype(jnp.float32)
    x1, x2 = xf[..., :half], xf[..., half:ROPE_DIM]
    out = jnp.concatenate([x1 * cos - x2 * sin, x2 * cos + x1 * sin, xf[..., ROPE_DIM:]], axis=-1)
    return out.astype(x.dtype)


def split_proj(h):
    offsets = []
    acc = 0
    for width in SPLITS[:-1]:
        acc += width
        offsets.append(acc)
    return jnp.split(h, offsets, axis=-1)


def diff_lambda(lq1, lk1, lq2, lk2, lambda_init):
    f32 = jnp.float32
    return (jnp.exp(jnp.sum(lq1.astype(f32) * lk1.astype(f32)))
            - jnp.exp(jnp.sum(lq2.astype(f32) * lk2.astype(f32))) + lambda_init)


def diff_attn(q, k, v, mask, lam):
    s = jnp.einsum('bqhmd,bkhmd->bhmqk', q, k).astype(jnp.float32) * (A_DH ** -0.5)
    p = jax.nn.softmax(jnp.where(mask, s, NEG), axis=-1)
    pd = p[:, :, 0] - lam * p[:, :, 1]
    return jnp.einsum('bhqk,bkhd->bqhd', pd.astype(v.dtype), v)


def causal_conv_silu(ext, w):
    t = ext.shape[1] - (CONV_W - 1)
    y = ext[:, 0:t] * w[0]
    for j in range(1, CONV_W):
        y = y + ext[:, j:j + t] * w[j]
    return jax.nn.silu(y)


def gdn_inputs(qkv, a, b, a_log, dt_bias):
    bsz, t = qkv.shape[:2]
    nk = B_HEADS * B_DK
    q = l2norm(qkv[..., :nk].reshape(bsz, t, B_HEADS, B_DK)) * (B_DK ** -0.5)
    k = l2norm(qkv[..., nk:2 * nk].reshape(bsz, t, B_HEADS, B_DK))
    v = qkv[..., 2 * nk:].reshape(bsz, t, B_HEADS, B_DV).astype(jnp.float32)
    g = -jnp.exp(a_log.astype(jnp.float32)) * jax.nn.softplus(a.astype(jnp.float32) + dt_bias.astype(jnp.float32))
    beta = jax.nn.sigmoid(b.astype(jnp.float32))
    return q, k, v, g, beta


def gated_delta_chunked(q, k, v, g, beta, s0):
    bsz, l, h, _ = k.shape
    dv = v.shape[-1]
    n = l // CHUNK

    def chunks(t):
        t = t.astype(jnp.float32).reshape((bsz, n, CHUNK, h) + t.shape[3:])
        return jnp.moveaxis(t, (1, 3), (0, 2))

    qc, kc, vc, gc, bc = chunks(q), chunks(k), chunks(v), chunks(g), chunks(beta)
    gc = jnp.cumsum(gc, axis=-1)
    tril = jnp.tril(jnp.ones((CHUNK, CHUNK), bool))
    strict = jnp.tril(jnp.ones((CHUNK, CHUNK), bool), -1)
    diff = gc[..., :, None] - gc[..., None, :]
    decay = jnp.where(tril, jnp.exp(jnp.where(tril, diff, 0.0)), 0.0)
    kb = kc * bc[..., None]
    lmat = jnp.einsum('nbhid,nbhjd->nbhij', kb, kc) * decay
    a = jnp.where(strict, lmat, 0.0) + jnp.eye(CHUNK, dtype=jnp.float32)
    rhs = jnp.concatenate([vc * bc[..., None], kb * jnp.exp(gc)[..., None]], axis=-1)
    sol = lax.linalg.triangular_solve(a, rhs, left_side=True, lower=True, unit_diagonal=True)
    u, w = sol[..., :dv], sol[..., dv:]

    def step(s, xs):
        q_i, k_i, u_i, w_i, g_i, d_i = xs
        attn = jnp.einsum('bhid,bhjd->bhij', q_i, k_i) * d_i
        v_new = u_i - jnp.einsum('bhcd,bhdv->bhcv', w_i, s)
        o = (jnp.einsum('bhcd,bhdv->bhcv', q_i * jnp.exp(g_i)[..., None], s)
             + jnp.einsum('bhij,bhjv->bhiv', attn, v_new))
        g_last = g_i[..., -1]
        s = (s * jnp.exp(g_last)[..., None, None]
             + jnp.einsum('bhcd,bhcv->bhdv', k_i * jnp.exp(g_last[..., None] - g_i)[..., None], v_new))
        return s, o

    s_fin, o = lax.scan(step, s0.astype(jnp.float32), (qc, kc, u, w, gc, decay))
    o = jnp.moveaxis(o, (0, 2), (1, 3)).reshape(bsz, l, h, dv)
    return o, s_fin


def gated_delta_recurrent(q, k, v, g, beta, s0):
    def step(s, xs):
        q_t, k_t, v_t, g_t, b_t = xs
        s = s * jnp.exp(g_t)[..., None, None]
        kv = jnp.einsum('bhkv,bhk->bhv', s, k_t)
        s = s + jnp.einsum('bhk,bhv->bhkv', k_t, (v_t - kv) * b_t[..., None])
        return s, jnp.einsum('bhkv,bhk->bhv', s, q_t)

    xs = tuple(jnp.moveaxis(t.astype(jnp.float32), 1, 0) for t in (q, k, v, g, beta))
    s_fin, o = lax.scan(step, s0.astype(jnp.float32), xs)
    return jnp.moveaxis(o, 0, 1), s_fin


def merge_branches(oa, ob, z_a, z_b, gate_a, gate_b, subln_w, gdn_norm_w, w_pa, w_pb, w_o, lambda_init):
    bsz, t = oa.shape[:2]
    ya = (rmsnorm(oa, subln_w) * (1.0 - lambda_init)).reshape(bsz, t, A_WIDTH) * jax.nn.silu(z_a)
    yb = rmsnorm(ob, gdn_norm_w).reshape(bsz, t, B_WIDTH) * jax.nn.silu(z_b)
    mixed = jax.nn.sigmoid(gate_a) * (ya @ w_pa) + jax.nn.sigmoid(gate_b) * (yb @ w_pb)
    return mixed @ w_o


def setup_inputs(seed: int = 0) -> dict:
    key = jax.random.key(seed)
    ks = jax.random.split(key, 24)
    f32 = jnp.float32
    n_pages = PAST_LEN // PAGE_SIZE
    n_used = DEC_BATCH * n_pages
    n_pool = n_used + max(n_used // 4, 1)

    def nrm(k, shape, scale=1.0):
        return jax.random.normal(k, shape, f32) * scale

    page_table = jax.random.permutation(ks[6], n_pool)[:n_used].reshape(DEC_BATCH, n_pages).astype(jnp.int32)
    dt0 = jnp.exp(jax.random.uniform(ks[15], (DEPTH, B_HEADS), f32, math.log(1e-3), math.log(1e-1)))
    return {
        'x_prompt': nrm(ks[0], (BATCH, SEQ, D_MODEL)),
        'x_sample': nrm(ks[1], (DEC_BATCH, DEC_SEQ, D_MODEL)),
        'cache_k': nrm(ks[2], (DEPTH, n_pool, PAGE_SIZE, A_HEADS, 2 * A_DH)),
        'cache_v': nrm(ks[3], (DEPTH, n_pool, PAGE_SIZE, A_HEADS, A_DV)),
        'state_conv': nrm(ks[4], (DEPTH, DEC_BATCH, CONV_W - 1, B_QKV)),
        'state_ssm': nrm(ks[5], (DEPTH, DEC_BATCH, B_HEADS, B_DK, B_DV), 0.1),
        'page_table': page_table,
        'meta_tokens': nrm(ks[7], (N_META, D_MODEL)),
        'norm_w': 1.0 + nrm(ks[8], (DEPTH, D_MODEL), 0.02),
        'w_in': nrm(ks[9], (DEPTH, D_MODEL, D_IN), D_MODEL ** -0.5),
        'lambda_q1': nrm(ks[10], (DEPTH, A_DH), 0.1),
        'lambda_k1': nrm(ks[11], (DEPTH, A_DH), 0.1),
        'lambda_q2': nrm(ks[12], (DEPTH, A_DH), 0.1),
        'lambda_k2': nrm(ks[13], (DEPTH, A_DH), 0.1),
        'subln_w': 1.0 + nrm(ks[16], (DEPTH, A_DV), 0.02),
        'conv_w': nrm(ks[17], (DEPTH, CONV_W, B_QKV), CONV_W ** -0.5),
        'a_log': jnp.log(jax.random.uniform(ks[14], (DEPTH, B_HEADS), f32, 1.0, 16.0)),
        'dt_bias': dt0 + jnp.log(-jnp.expm1(-dt0)),
        'gdn_norm_w': 1.0 + nrm(ks[18], (DEPTH, B_DV), 0.02),
        'w_pa': nrm(ks[19], (DEPTH, A_WIDTH, D_MODEL), A_WIDTH ** -0.5),
        'w_pb': nrm(ks[20], (DEPTH, B_WIDTH, D_MODEL), B_WIDTH ** -0.5),
        'w_o': nrm(ks[21], (DEPTH, D_MODEL, D_MODEL), D_MODEL ** -0.5),
        'final_norm_w': 1.0 + nrm(ks[22], (D_MODEL,), 0.02),
    }


def reference(x_prompt, x_sample, cache_k, cache_v, state_conv, state_ssm, page_table,
              meta_tokens, norm_w, w_in, lambda_q1, lambda_k1, lambda_q2, lambda_k2,
              subln_w, conv_w, a_log, dt_bias, gdn_norm_w, w_pa, w_pb, w_o, final_norm_w):
    dt = x_prompt.dtype
    n_pad = META_BLOCK - N_META
    L = META_BLOCK + SEQ
    hp = jnp.concatenate([jnp.zeros((BATCH, n_pad, D_MODEL), dt),
                          jnp.broadcast_to(meta_tokens.astype(dt)[None], (BATCH, N_META, D_MODEL)),
                          x_prompt], axis=1)
    idx = jnp.arange(L)
    pos_p = idx - n_pad
    valid = pos_p >= 0
    vmask = valid[None, :, None].astype(jnp.float32)
    hs = x_sample
    pos_s = PAST_LEN + jnp.arange(DEC_SEQ)
    kidx_s = jnp.arange(PAST_LEN + DEC_SEQ)
    mask_s = kidx_s[None, :] <= PAST_LEN + jnp.arange(DEC_SEQ)[:, None]
    kp_l, vp_l, cp_l, sp_l, ks_l, vs_l, cs_l, ss_l = [], [], [], [], [], [], [], []
    for l in range(DEPTH):
        lambda_init = 0.8 - 0.6 * math.exp(-0.3 * l)
        lam = diff_lambda(lambda_q1[l], lambda_k1[l], lambda_q2[l], lambda_k2[l], lambda_init)

        xn = rmsnorm(hp, norm_w[l])
        qa, ka, va, za, qkv_b, zb, ab, bb, gate_a, gate_b = split_proj(xn @ w_in[l])
        qa = rope(qa.reshape(BATCH, L, A_HEADS, 2, A_DH), pos_p)
        ka = rope(ka.reshape(BATCH, L, A_HEADS, 2, A_DH), pos_p)
        va = va.reshape(BATCH, L, A_HEADS, A_DV)
        qblocks = jnp.moveaxis(qa.reshape(BATCH, L // Q_BLOCK, Q_BLOCK, A_HEADS, 2, A_DH), 1, 0)

        def attend_block(args):
            i, q_i = args
            qidx = i * Q_BLOCK + jnp.arange(Q_BLOCK)
            mask = (idx[None, :] <= qidx[:, None]) & valid[None, :]
            return diff_attn(q_i, ka, va, mask, lam)

        oa = lax.map(attend_block, (jnp.arange(L // Q_BLOCK), qblocks))
        oa = jnp.moveaxis(oa, 0, 1).reshape(BATCH, L, A_HEADS, A_DV)
        ext = jnp.concatenate([jnp.zeros((BATCH, CONV_W - 1, B_QKV), dt), qkv_b], axis=1)
        q_b, k_b, v_b, g_dec, beta = gdn_inputs(causal_conv_silu(ext, conv_w[l]), ab, bb, a_log[l], dt_bias[l])
        s0 = jnp.zeros((BATCH, B_HEADS, B_DK, B_DV), jnp.float32)
        ob, s_p = gated_delta_chunked(q_b, k_b, v_b, g_dec * vmask, beta * vmask, s0)
        upd = merge_branches(oa, ob.astype(dt), za, zb, gate_a, gate_b, subln_w[l], gdn_norm_w[l],
                             w_pa[l], w_pb[l], w_o[l], lambda_init)
        hp = hp + jnp.where(valid[None, :, None], upd, 0.0).astype(dt)
        kp_l.append(ka[:, n_pad:].reshape(BATCH, SEQ + N_META, A_HEADS, 2 * A_DH))
        vp_l.append(va[:, n_pad:])
        cp_l.append(ext[:, -(CONV_W - 1):])
        sp_l.append(s_p.astype(dt))

        xn = rmsnorm(hs, norm_w[l])
        qa, ka, va, za, qkv_b, zb, ab, bb, gate_a, gate_b = split_proj(xn @ w_in[l])
        qa = rope(qa.reshape(DEC_BATCH, DEC_SEQ, A_HEADS, 2, A_DH), pos_s)
        ka = rope(ka.reshape(DEC_BATCH, DEC_SEQ, A_HEADS, 2, A_DH), pos_s)
        va = va.reshape(DEC_BATCH, DEC_SEQ, A_HEADS, A_DV)
        k_past = jnp.take(cache_k[l], page_table, axis=0).reshape(DEC_BATCH, PAST_LEN, A_HEADS, 2, A_DH).astype(dt)
        v_past = jnp.take(cache_v[l], page_table, axis=0).reshape(DEC_BATCH, PAST_LEN, A_HEADS, A_DV).astype(dt)
        oa = diff_attn(qa, jnp.concatenate([k_past, ka], axis=1), jnp.concatenate([v_past, va], axis=1), mask_s, lam)
        ext = jnp.concatenate([state_conv[l].astype(dt), qkv_b], axis=1)
        q_b, k_b, v_b, g_dec, beta = gdn_inputs(causal_conv_silu(ext, conv_w[l]), ab, bb, a_log[l], dt_bias[l])
        ob, s_s = gated_delta_recurrent(q_b, k_b, v_b, g_dec, beta, state_ssm[l])
        hs = hs + merge_branches(oa, ob.astype(dt), za, zb, gate_a, gate_b, subln_w[l], gdn_norm_w[l],
                                 w_pa[l], w_pb[l], w_o[l], lambda_init)
        ks_l.append(ka.reshape(DEC_BATCH, DEC_SEQ, A_HEADS, 2 * A_DH))
        vs_l.append(va)
        cs_l.append(ext[:, -(CONV_W - 1):])
        ss_l.append(s_s.astype(dt))

    y_prompt = rmsnorm(hp[:, META_BLOCK:], final_norm_w)
    y_sample = rmsnorm(hs, final_norm_w)
    return (y_prompt, y_sample,
            jnp.stack(kp_l), jnp.stack(vp_l), jnp.stack(cp_l), jnp.stack(sp_l),
            jnp.stack(ks_l), jnp.stack(vs_l), jnp.stack(cs_l), jnp.stack(ss_l))
```

```python
import functools
import math

import jax
import jax.numpy as jnp
from jax import lax
from jax.experimental import pallas as pl
from jax.experimental.pallas import tpu as pltpu

F32 = jnp.float32
BF16 = jnp.bfloat16

D_MODEL = 2048
BATCH = 4
SEQ = 2048
DEC_BATCH = 32
PAST_LEN = 8192
PAGE_SIZE = 128
N_PAGES = PAST_LEN // PAGE_SIZE
N_META = 16
A_HEADS = 8
A_DH = 64
A_DV = 128
ROPE_DIM = 16
ROPE_THETA = 500000.0
B_HEADS = 8
B_DK = 128
B_DV = 128
B_QKV = 3072
CONV_W = 4
CHUNK = 64
EPS = 1e-6
NEG = -1e30
LAMBDA_INIT = 0.8 - 0.6 * math.exp(-0.3 * 0)

O_QA, O_KA, O_VA, O_ZA, O_QKVB, O_ZB, O_A, O_GA, O_GB, D_IN = (
    0, 1024, 2048, 3072, 4096, 7168, 8192, 8208, 10256, 12304)
M_QKVB, M_QA, M_KA, M_VA, M_ZA, M_ZB, M_GA, M_GB, N_MAIN = (
    0, 3072, 4096, 5120, 6144, 7168, 8192, 10240, 12288)

LANES = 128
VMEM_LIMIT = 56 * 1024 * 1024


def _dot(a, b):
    return jnp.dot(a, b, preferred_element_type=F32)


def _dot_nt(a, b):
    return lax.dot_general(a, b, (((1,), (1,)), ((), ())), preferred_element_type=F32)


def _dot_f32(a, b):
    return jnp.dot(a, b, preferred_element_type=F32, precision=lax.Precision.HIGHEST)


def _sigmoid(x):
    return 1.0 / (1.0 + jnp.exp(-x))


def _silu(x):
    return x * _sigmoid(x)


def _softplus(x):
    return jnp.maximum(x, 0.0) + jnp.log1p(jnp.exp(-jnp.abs(x)))


def _rmsnorm(x, w):
    return x * lax.rsqrt(jnp.mean(x * x, axis=-1, keepdims=True) + EPS) * w


def _rope_tables(pos):
    r = pos.shape[0]
    inv_freq = ROPE_THETA ** (-jnp.arange(0, ROPE_DIM, 2, dtype=F32) / ROPE_DIM)
    ang = pos.astype(F32)[:, None] * inv_freq[None, :]
    cos, sin = jnp.cos(ang), jnp.sin(ang)
    half = ROPE_DIM // 2
    rest = A_DH - ROPE_DIM
    c = jnp.concatenate([cos, cos, jnp.ones((r, rest), F32)], axis=1)
    sa = jnp.concatenate([jnp.zeros((r, half), F32), sin, jnp.zeros((r, rest), F32)], axis=1)
    sb = jnp.concatenate([-sin, jnp.zeros((r, half + rest), F32)], axis=1)
    return tuple(jnp.tile(t, (1, LANES // A_DH)) for t in (c, sa, sb))


def _rope_tile(t, c, sa, sb):
    out = []
    for i in range(t.shape[1] // LANES):
        x = t[:, i * LANES:(i + 1) * LANES]
        out.append(x * c + pltpu.roll(x, ROPE_DIM // 2, 1) * sa
                   + pltpu.roll(x, LANES - ROPE_DIM // 2, 1) * sb)
    return jnp.concatenate(out, axis=1) if len(out) > 1 else out[0]


IP_TM = 1024
IP_TN = 512
IP_ROWS = 256


def _inproj_kernel(x_ref, nw_ref, w_ref, wab_ref, c_ref, sa_ref, sb_ref, h_ref, hab_ref, xn_ref):
    j = pl.program_id(1)

    @pl.when(j == 0)
    def _():
        def body(r, carry):
            rows = pl.ds(pl.multiple_of(r * IP_ROWS, IP_ROWS), IP_ROWS)
            xn_ref[rows, :] = _rmsnorm(x_ref[rows, :], nw_ref[...]).astype(BF16)
            return carry
        lax.fori_loop(0, IP_TM // IP_ROWS, body, 0)
        hab_ref[...] = _dot(xn_ref[...], wab_ref[...])

    acc = _dot(xn_ref[...], w_ref[...])
    is_rope = (j >= M_QA // IP_TN) & (j < M_VA // IP_TN)

    @pl.when(is_rope)
    def _():
        h_ref[...] = _rope_tile(acc, c_ref[...], sa_ref[...], sb_ref[...])

    @pl.when(jnp.logical_not(is_rope))
    def _():
        h_ref[...] = acc


def _inproj(x, nw, w_main, w_ab, tabs):
    m = x.shape[0]
    per_seq = SEQ // IP_TM
    tab_spec = pl.BlockSpec((IP_TM, LANES), lambda i, j: (i % per_seq, 0))
    return pl.pallas_call(
        _inproj_kernel,
        grid=(m // IP_TM, N_MAIN // IP_TN),
        in_specs=[
            pl.BlockSpec((IP_TM, D_MODEL), lambda i, j: (i, 0)),
            pl.BlockSpec((1, D_MODEL), lambda i, j: (0, 0)),
            pl.BlockSpec((D_MODEL, IP_TN), lambda i, j: (0, j)),
            pl.BlockSpec((D_MODEL, LANES), lambda i, j: (0, 0)),
            tab_spec, tab_spec, tab_spec,
        ],
        out_specs=[
            pl.BlockSpec((IP_TM, IP_TN), lambda i, j: (i, j)),
            pl.BlockSpec((IP_TM, LANES), lambda i, j: (i, 0)),
        ],
        out_shape=[jax.ShapeDtypeStruct((m, N_MAIN), F32),
                   jax.ShapeDtypeStruct((m, LANES), F32)],
        scratch_shapes=[pltpu.VMEM((IP_TM, D_MODEL), BF16)],
        compiler_params=pltpu.CompilerParams(
            dimension_semantics=("arbitrary", "arbitrary"), vmem_limit_bytes=VMEM_LIMIT),
        name="inproj",
    )(x, nw, w_main, w_ab, *tabs)


AUX_ROWS = 64
AUX_TN = 512


def _aux_inproj_kernel(x_ref, nw_ref, w_ref, c_ref, sa_ref, sb_ref, o_ref, xs_ref):
    j = pl.program_id(0)

    @pl.when(j == 0)
    def _():
        xn = _rmsnorm(x_ref[...], nw_ref[...])
        hi = xn.astype(BF16)
        xs_ref[0:AUX_ROWS, :] = hi
        xs_ref[AUX_ROWS:2 * AUX_ROWS, :] = (xn - hi.astype(F32)).astype(BF16)

    w = w_ref[...]
    w_hi = w.astype(BF16)
    w_lo = (w - w_hi.astype(F32)).astype(BF16)
    r1 = _dot(xs_ref[...], w_hi)
    r2 = _dot(xs_ref[0:AUX_ROWS, :], w_lo)
    acc = r1[0:AUX_ROWS] + (r1[AUX_ROWS:] + r2)
    is_rope = j < O_VA // AUX_TN

    @pl.when(is_rope)
    def _():
        o_ref[...] = _rope_tile(acc, c_ref[...], sa_ref[...], sb_ref[...])

    @pl.when(jnp.logical_not(is_rope))
    def _():
        o_ref[...] = acc


def _aux_inproj(xa, nw, w, tabs):
    tab_spec = pl.BlockSpec((AUX_ROWS, LANES), lambda j: (0, 0))
    return pl.pallas_call(
        _aux_inproj_kernel,
        grid=(pl.cdiv(D_IN, AUX_TN),),
        in_specs=[
            pl.BlockSpec((AUX_ROWS, D_MODEL), lambda j: (0, 0)),
            pl.BlockSpec((1, D_MODEL), lambda j: (0, 0)),
            pl.BlockSpec((D_MODEL, AUX_TN), lambda j: (0, j)),
            tab_spec, tab_spec, tab_spec,
        ],
        out_specs=pl.BlockSpec((AUX_ROWS, AUX_TN), lambda j: (0, j)),
        out_shape=jax.ShapeDtypeStruct((AUX_ROWS, D_IN), F32),
        scratch_shapes=[pltpu.VMEM((2 * AUX_ROWS, D_MODEL), BF16)],
        compiler_params=pltpu.CompilerParams(
            dimension_semantics=("arbitrary",), vmem_limit_bytes=VMEM_LIMIT),
        name="aux_inproj",
    )(xa, nw, w, *tabs)


AT_TQ = 256
AT_TK = 256


def _diff_lambda(lq1, lk1, lq2, lk2):
    a = jnp.exp(jnp.sum(lq1 * lk1, axis=-1, keepdims=True))
    b = jnp.exp(jnp.sum(lq2 * lk2, axis=-1, keepdims=True))
    return a - b + LAMBDA_INIT


def _attn_kernel(lq1_ref, lk1_ref, lq2_ref, lk2_ref, q_ref, k_ref, v_ref, km_ref, vm_ref,
                 za_ref, sw_ref, o_ref, kb_ref, vb_ref, q2_ref, m_ref, l_ref, acc_ref):
    qi = pl.program_id(2)
    tq, tk = AT_TQ, AT_TK

    @pl.when(qi == 0)
    def _():
        kb_ref[...] = k_ref[...].astype(BF16)
        vb_ref[...] = v_ref[...].astype(BF16)

    q = q_ref[...] * (A_DH ** -0.5)
    lane = lax.broadcasted_iota(jnp.int32, (tq, LANES), 1)
    q2_ref[0:tq, :] = jnp.where(lane < A_DH, q, 0.0).astype(BF16)
    q2_ref[tq:2 * tq, :] = jnp.where(lane >= A_DH, q, 0.0).astype(BF16)

    s = _dot_nt(q2_ref[...], km_ref[...].astype(BF16))
    m0 = jnp.max(s, axis=-1, keepdims=True)
    p = jnp.exp(s - m0)
    m_ref[...] = m0
    l_ref[...] = jnp.sum(p, axis=-1, keepdims=True)
    acc_ref[...] = _dot(p.astype(BF16), vm_ref[...].astype(BF16))

    def step(j, masked):
        rows = pl.ds(pl.multiple_of(j * tk, tk), tk)
        s = _dot_nt(q2_ref[...], kb_ref[rows, :])
        if masked:
            r = lax.broadcasted_iota(jnp.int32, (2 * tq, tk), 0)
            r = jnp.where(r >= tq, r - tq, r)
            c = lax.broadcasted_iota(jnp.int32, (2 * tq, tk), 1)
            s = jnp.where(c <= r, s, NEG)
        m_prev = m_ref[...]
        m_new = jnp.maximum(m_prev, jnp.max(s, axis=-1, keepdims=True))
        alpha = jnp.exp(m_prev - m_new)
        p = jnp.exp(s - m_new)
        l_ref[...] = alpha * l_ref[...] + jnp.sum(p, axis=-1, keepdims=True)
        acc_ref[...] = alpha * acc_ref[...] + _dot(p.astype(BF16), vb_ref[rows, :])
        m_ref[...] = m_new

    def body(j, carry):
        step(j, False)
        return carry
    lax.fori_loop(0, qi, body, 0)
    step(qi, True)

    lam = _diff_lambda(lq1_ref[...], lk1_ref[...], lq2_ref[...], lk2_ref[...])
    o = acc_ref[0:tq, :] / l_ref[0:tq, :] - lam * (acc_ref[tq:2 * tq, :] / l_ref[tq:2 * tq, :])
    y = _rmsnorm(o, sw_ref[...]) * (1.0 - LAMBDA_INIT)
    o_ref[...] = (y * _silu(za_ref[...])).astype(BF16)


def _attention(lams, h_main, k_meta, v_meta, subln_w):
    nq = SEQ // AT_TQ
    lam_spec = pl.BlockSpec((1, A_DH), lambda b, h, i: (0, 0))
    return pl.pallas_call(
        _attn_kernel,
        grid=(BATCH, A_HEADS, nq),
        in_specs=[
            lam_spec, lam_spec, lam_spec, lam_spec,
            pl.BlockSpec((AT_TQ, LANES), lambda b, h, i: (b * nq + i, M_QA // LANES + h)),
            pl.BlockSpec((SEQ, LANES), lambda b, h, i: (b, M_KA // LANES + h)),
            pl.BlockSpec((SEQ, LANES), lambda b, h, i: (b, M_VA // LANES + h)),
            pl.BlockSpec((N_META, LANES), lambda b, h, i: (0, h)),
            pl.BlockSpec((N_META, LANES), lambda b, h, i: (0, h)),
            pl.BlockSpec((AT_TQ, LANES), lambda b, h, i: (b * nq + i, M_ZA // LANES + h)),
            pl.BlockSpec((1, A_DV), lambda b, h, i: (0, 0)),
        ],
        out_specs=pl.BlockSpec((AT_TQ, LANES), lambda b, h, i: (b * nq + i, h)),
        out_shape=jax.ShapeDtypeStruct((BATCH * SEQ, A_HEADS * A_DV), BF16),
        scratch_shapes=[
            pltpu.VMEM((SEQ, LANES), BF16), pltpu.VMEM((SEQ, LANES), BF16),
            pltpu.VMEM((2 * AT_TQ, LANES), BF16),
            pltpu.VMEM((2 * AT_TQ, 1), F32), pltpu.VMEM((2 * AT_TQ, 1), F32),
            pltpu.VMEM((2 * AT_TQ, A_DV), F32),
        ],
        compiler_params=pltpu.CompilerParams(
            dimension_semantics=("arbitrary", "arbitrary", "arbitrary"),
            vmem_limit_bytes=VMEM_LIMIT),
        name="diff_attn",
    )(*lams, h_main, h_main, h_main, k_meta, v_meta, h_main, subln_w)


def _unit_lower_inverse_minus_eye(a, i_idx, j_idx):
    base = 8
    a_d = jnp.where((i_idx // base) == (j_idx // base), a, 0.0)
    n = -a_d
    b = a_d
    for _ in range(2):
        bb = b.astype(BF16)
        b = _dot(bb, bb)
        n = n + b + _dot(n.astype(BF16), b.astype(BF16))
    s = base
    while s < CHUNK:
        join = ((i_idx // (2 * s)) == (j_idx // (2 * s))) & ((i_idx // s) % 2 == 1) & ((j_idx // s) % 2 == 0)
        a_s = jnp.where(join, a, 0.0)
        x = a_s + _dot(n.astype(BF16), a_s.astype(BF16))
        n = n - (x + _dot(x.astype(BF16), n.astype(BF16)))
        s *= 2
    return n


def _gdn_kernel(x_ref, ab_ref, abt_ref, zb_ref, cw_ref, alog_ref, dtb_ref, alogc_ref, dtbc_ref,
                gw_ref, halo_ref, s0_ref, yb_ref, sfin_ref, ext_ref, s_ref, *, masked_rows):
    c = pl.program_id(1)
    hist = 8

    @pl.when(c == 0)
    def _():
        ext_ref[0:hist, :] = halo_ref[...]
        s_ref[...] = s0_ref[...]

    x = x_ref[...]
    ext_ref[hist:hist + CHUNK, :] = x
    y = cw_ref[CONV_W - 1:CONV_W, :] * x
    for t in range(CONV_W - 1):
        lo = hist - (CONV_W - 1) + t
        y = y + cw_ref[t:t + 1, :] * ext_ref[lo:lo + CHUNK, :]
    y = _silu(y)
    ext_ref[0:hist, :] = x[CHUNK - hist:CHUNK, :]

    i_idx = lax.broadcasted_iota(jnp.int32, (CHUNK, CHUNK), 0)
    j_idx = lax.broadcasted_iota(jnp.int32, (CHUNK, CHUNK), 1)
    tril = i_idx >= j_idx
    strict = i_idx > j_idx
    tril_f = tril.astype(F32)
    triu_f = (i_idx <= j_idx).astype(F32)

    ab = ab_ref[...]
    g_c = -jnp.exp(alog_ref[...]) * _softplus(ab + dtb_ref[...])
    beta_c = _sigmoid(ab)
    abt = abt_ref[...]
    g_r = -jnp.exp(alogc_ref[...]) * _softplus(abt + dtbc_ref[...])
    if masked_rows:
        row_ok = lax.broadcasted_iota(jnp.int32, (CHUNK, LANES), 0) >= masked_rows
        col_ok = lax.broadcasted_iota(jnp.int32, (2 * B_HEADS, CHUNK), 1) >= masked_rows
        g_c = jnp.where(row_ok, g_c, 0.0)
        beta_c = jnp.where(row_ok, beta_c, 0.0)
        g_r = jnp.where(col_ok, g_r, 0.0)
    gc_c = _dot_f32(tril_f, g_c)
    gc_r = _dot_f32(g_r, triu_f)

    nk = B_HEADS * B_DK
    for h in range(B_HEADS):
        qh = y[:, h * B_DK:(h + 1) * B_DK]
        kh = y[:, nk + h * B_DK:nk + (h + 1) * B_DK]
        vh = y[:, 2 * nk + h * B_DV:2 * nk + (h + 1) * B_DV]
        qn = qh * lax.rsqrt(jnp.sum(qh * qh, axis=-1, keepdims=True) + EPS) * (B_DK ** -0.5)
        kn = kh * lax.rsqrt(jnp.sum(kh * kh, axis=-1, keepdims=True) + EPS)
        bcol = beta_c[:, B_HEADS + h:B_HEADS + h + 1]
        gcc = gc_c[:, h:h + 1]
        gcr = gc_r[h:h + 1, :]
        diff = gcc - gcr
        decay = jnp.where(tril, jnp.exp(jnp.where(tril, diff, 0.0)), 0.0)
        kbeta = kn * bcol
        kn_b = kn.astype(BF16)
        a = jnp.where(strict, _dot_nt(kbeta.astype(BF16), kn_b) * decay, 0.0)
        n = _unit_lower_inverse_minus_eye(a, i_idx, j_idx)
        egc = jnp.exp(gcc)
        rhs = jnp.concatenate([vh * bcol, kbeta * egc], axis=1)
        sol = rhs + _dot(n.astype(BF16), rhs.astype(BF16))
        u = sol[:, 0:B_DV]
        w = sol[:, B_DV:B_DV + B_DK]
        st = s_ref[h]
        st_b = st.astype(BF16)
        v_new = u - _dot(w.astype(BF16), st_b)
        v_new_b = v_new.astype(BF16)
        attn = _dot_nt(qn.astype(BF16), kn_b) * decay
        o = _dot((qn * egc).astype(BF16), st_b) + _dot(attn.astype(BF16), v_new_b)
        g_last = gcc[CHUNK - 1:CHUNK, :]
        ke = kn * jnp.exp(g_last - gcc)
        s_ref[h] = st * jnp.exp(g_last) + _dot(ke.T.astype(BF16), v_new_b)
        zb = zb_ref[:, h * B_DV:(h + 1) * B_DV]
        yb_ref[:, h * B_DV:(h + 1) * B_DV] = (_rmsnorm(o, gw_ref[...]) * _silu(zb)).astype(BF16)

    sfin_ref[...] = s_ref[...]


def _gdn_chunks(x, ab, abt, zsrc, zb_col, cw, gvecs, gw, halo, s0, n_seq, n_chunk, masked_rows):
    alog_l, dtb_l, alog_c, dtb_c = gvecs
    const2 = lambda b, c: (0, 0)
    kern = functools.partial(_gdn_kernel, masked_rows=masked_rows)
    return pl.pallas_call(
        kern,
        grid=(n_seq, n_chunk),
        in_specs=[
            pl.BlockSpec((CHUNK, B_QKV), lambda b, c: (b * n_chunk + c, 0)),
            pl.BlockSpec((CHUNK, LANES), lambda b, c: (b * n_chunk + c, 0)),
            pl.BlockSpec((None, None, 2 * B_HEADS, CHUNK), lambda b, c: (b, c, 0, 0)),
            pl.BlockSpec((CHUNK, B_HEADS * B_DV), lambda b, c: (b * n_chunk + c, zb_col)),
            pl.BlockSpec((CONV_W, B_QKV), const2),
            pl.BlockSpec((1, LANES), const2), pl.BlockSpec((1, LANES), const2),
            pl.BlockSpec((2 * B_HEADS, 1), const2), pl.BlockSpec((2 * B_HEADS, 1), const2),
            pl.BlockSpec((1, B_DV), const2),
            pl.BlockSpec((8, B_QKV), const2),
            pl.BlockSpec((B_HEADS, B_DK, B_DV), lambda b, c: (0, 0, 0)),
        ],
        out_specs=[
            pl.BlockSpec((CHUNK, B_HEADS * B_DV), lambda b, c: (b * n_chunk + c, 0)),
            pl.BlockSpec((None, B_HEADS, B_DK, B_DV), lambda b, c: (b, 0, 0, 0)),
        ],
        out_shape=[jax.ShapeDtypeStruct((n_seq * n_chunk * CHUNK, B_HEADS * B_DV), BF16),
                   jax.ShapeDtypeStruct((n_seq, B_HEADS, B_DK, B_DV), F32)],
        scratch_shapes=[pltpu.VMEM((8 + CHUNK, B_QKV), F32),
                        pltpu.VMEM((B_HEADS, B_DK, B_DV), F32)],
        compiler_params=pltpu.CompilerParams(
            dimension_semantics=("arbitrary", "arbitrary"), vmem_limit_bytes=VMEM_LIMIT),
        name="gdn_chunks",
    )(x, ab, abt, zsrc, cw, alog_l, dtb_l, alog_c, dtb_c, gw, halo, s0)


def _merge_kernel(x_ref, ya_ref, yb_ref, ga_ref, gb_ref, wpa_ref, wpb_ref, wo_ref, fw_ref, y_ref):
    pa = _dot(ya_ref[...], wpa_ref[...])
    pb = _dot(yb_ref[...], wpb_ref[...])
    mixed = _sigmoid(ga_ref[...]) * pa + _sigmoid(gb_ref[...]) * pb
    hp = x_ref[...] + _dot(mixed.astype(BF16), wo_ref[...])
    y_ref[...] = _rmsnorm(hp, fw_ref[...])


def _merge(x, ya, yb, gsrc_a, gsrc_b, ga_col, gb_col, wpa, wpb, wo, fw, tm):
    m = x.shape[0]
    once = pl.Buffered(1)
    return pl.pallas_call(
        _merge_kernel,
        grid=(m // tm,),
        in_specs=[
            pl.BlockSpec((tm, D_MODEL), lambda i: (i, 0)),
            pl.BlockSpec((tm, A_HEADS * A_DV), lambda i: (i, 0)),
            pl.BlockSpec((tm, B_HEADS * B_DV), lambda i: (i, 0)),
            pl.BlockSpec((tm, D_MODEL), lambda i: (i, ga_col)),
            pl.BlockSpec((tm, D_MODEL), lambda i: (i, gb_col)),
            pl.BlockSpec((A_HEADS * A_DV, D_MODEL), lambda i: (0, 0), pipeline_mode=once),
            pl.BlockSpec((B_HEADS * B_DV, D_MODEL), lambda i: (0, 0), pipeline_mode=once),
            pl.BlockSpec((D_MODEL, D_MODEL), lambda i: (0, 0), pipeline_mode=once),
            pl.BlockSpec((1, D_MODEL), lambda i: (0, 0)),
        ],
        out_specs=pl.BlockSpec((tm, D_MODEL), lambda i: (i, 0)),
        out_shape=jax.ShapeDtypeStruct((m, D_MODEL), F32),
        compiler_params=pltpu.CompilerParams(
            dimension_semantics=("arbitrary",), vmem_limit_bytes=VMEM_LIMIT),
        name="merge",
    )(x, ya, yb, gsrc_a, gsrc_b, wpa, wpb, wo, fw)


DA_P = 8
DA_G = N_PAGES // DA_P
TOK_TILES = PAGE_SIZE * A_HEADS // LANES


def _lane_group_reduce(x, op):
    s = A_HEADS
    while s < LANES:
        x = op(x, pltpu.roll(x, s, 1))
        s *= 2
    return x


def _decode_attn_kernel(pt_ref, lq1_ref, lk1_ref, lq2_ref, lk2_ref, q_ref, kn_ref, vn_ref,
                        za_ref, sw_ref, *rest):
    k_refs = rest[0:DA_P]
    v_refs = rest[DA_P:2 * DA_P]
    o_ref, sc_ref, acc_ref, wn_ref = rest[2 * DA_P:]
    j = pl.program_id(1)
    sub = lax.broadcasted_iota(jnp.int32, (A_HEADS, LANES), 0)
    lane = lax.broadcasted_iota(jnp.int32, (A_HEADS, LANES), 1)
    head_of_lane = lane % A_HEADS
    hmask = head_of_lane == sub
    half_sel = jnp.where((sub == 0) & (lane < A_DH), 1.0,
                         jnp.where((sub == 1) & (lane >= A_DH), 1.0, 0.0)).astype(BF16)

    @pl.when(j < DA_G)
    def _():
        q = q_ref[...]
        for p in range(DA_P):
            prod = (k_refs[p][...] * q[None]).reshape(PAGE_SIZE * A_HEADS, LANES)
            s = _dot_nt(half_sel, prod.astype(BF16))
            base = (j * DA_P + p) * TOK_TILES
            for c in range(TOK_TILES):
                sc_ref[base + c] = s[:, c * LANES:(c + 1) * LANES]

    @pl.when(j == DA_G)
    def _():
        lam = _diff_lambda(lq1_ref[...], lk1_ref[...], lq2_ref[...], lk2_ref[...])
        prod = q_ref[...] * kn_ref[...]
        hs1 = jnp.sum(jnp.where(lane < A_DH, prod, 0.0), axis=1, keepdims=True)
        hs2 = jnp.sum(jnp.where(lane >= A_DH, prod, 0.0), axis=1, keepdims=True)
        row1 = jnp.sum(jnp.where(hmask, hs1, 0.0), axis=0, keepdims=True)
        row2 = jnp.sum(jnp.where(hmask, hs2, 0.0), axis=0, keepdims=True)
        s_new = jnp.where(sub == 0, row1, jnp.where(sub == 1, row2, 0.0))
        sc = sc_ref[...]
        mx = _lane_group_reduce(jnp.max(sc, axis=0), jnp.maximum)
        mx = jnp.maximum(mx, s_new)
        p = jnp.exp(sc - mx[None])
        p_new = jnp.exp(s_new - mx)
        den = _lane_group_reduce(jnp.sum(p, axis=0), jnp.add) + p_new
        coef = jnp.where(sub == 0, 1.0 / den, jnp.where(sub == 1, -lam / den, 0.0))
        sc_ref[...] = p * coef[None]
        wn_ref[...] = jnp.sum(p_new * coef, axis=0, keepdims=True)
        acc_ref[...] = jnp.zeros_like(acc_ref)

    @pl.when(j >= DA_G)
    def _():
        acc = acc_ref[...]
        for p in range(DA_P):
            base = ((j - DA_G) * DA_P + p) * TOK_TILES
            tiles = []
            for c in range(TOK_TILES):
                r = jnp.sum(sc_ref[base + c], axis=0, keepdims=True)
                tiles.append(jnp.where(hmask, r, 0.0))
            w = jnp.concatenate(tiles, axis=1)
            w_hi = w.astype(BF16)
            w_lo = (w - w_hi.astype(F32)).astype(BF16)
            v = v_refs[p][...].reshape(PAGE_SIZE * A_HEADS, LANES).astype(BF16)
            r = _dot(jnp.concatenate([w_hi, w_lo], axis=0), v)
            acc = acc + (r[0:A_HEADS] + r[A_HEADS:])
        acc_ref[...] = acc

    @pl.when(j == 2 * DA_G - 1)
    def _():
        w_new = jnp.sum(jnp.where(hmask, wn_ref[...], 0.0), axis=1, keepdims=True) * (A_HEADS / LANES)
        o = acc_ref[...] + w_new * vn_ref[...]
        y = _rmsnorm(o, sw_ref[...]) * (1.0 - LAMBDA_INIT)
        o_ref[...] = y * _silu(za_ref[...])


def _decode_attention(page_table, lams, q_s, k_new, v_new, za_s, subln_w, cache_k, cache_v):
    lam_spec = pl.BlockSpec((1, A_DH), lambda b, j, pt: (0, 0))
    row_spec = pl.BlockSpec((None, A_HEADS, LANES), lambda b, j, pt: (b, 0, 0))
    page_block = (None, None, PAGE_SIZE, A_HEADS, LANES)

    def k_spec(p):
        return pl.BlockSpec(page_block, lambda b, j, pt: (0, pt[b, jnp.minimum(j, DA_G - 1) * DA_P + p], 0, 0, 0))

    def v_spec(p):
        return pl.BlockSpec(page_block, lambda b, j, pt: (0, pt[b, jnp.maximum(j - DA_G, 0) * DA_P + p], 0, 0, 0))

    grid_spec = pltpu.PrefetchScalarGridSpec(
        num_scalar_prefetch=1,
        grid=(DEC_BATCH, 2 * DA_G),
        in_specs=[lam_spec, lam_spec, lam_spec, lam_spec, row_spec, row_spec, row_spec, row_spec,
                  pl.BlockSpec((1, A_DV), lambda b, j, pt: (0, 0))]
                 + [k_spec(p) for p in range(DA_P)] + [v_spec(p) for p in range(DA_P)],
        out_specs=row_spec,
        scratch_shapes=[pltpu.VMEM((N_PAGES * TOK_TILES, A_HEADS, LANES), F32),
                        pltpu.VMEM((A_HEADS, A_DV), F32),
                        pltpu.VMEM((1, LANES), F32)],
    )
    return pl.pallas_call(
        _decode_attn_kernel,
        grid_spec=grid_spec,
        out_shape=jax.ShapeDtypeStruct((DEC_BATCH, A_HEADS, A_DV), F32),
        compiler_params=pltpu.CompilerParams(
            dimension_semantics=("arbitrary", "arbitrary"), vmem_limit_bytes=VMEM_LIMIT),
        name="decode_attn",
    )(page_table, *lams, q_s, k_new, v_new, za_s, subln_w,
      *([cache_k] * DA_P), *([cache_v] * DA_P))


def _gdn_step_kernel(sc_ref, x_ref, cw_ref, ab_ref, alog_ref, dtb_ref, zb_ref, gw_ref, s_ref,
                     yb_ref, sout_ref, cout_ref):
    x = x_ref[...]
    y = cw_ref[CONV_W - 1] * x
    for t in range(CONV_W - 1):
        y = y + cw_ref[t] * sc_ref[t]
    y = _silu(y)
    for t in range(CONV_W - 2):
        cout_ref[t] = sc_ref[t + 1]
    cout_ref[CONV_W - 2] = x

    q = y[0:B_HEADS]
    k = y[B_HEADS:2 * B_HEADS]
    v = y[2 * B_HEADS:3 * B_HEADS]
    qn = q * lax.rsqrt(jnp.sum(q * q, axis=-1, keepdims=True) + EPS) * (B_DK ** -0.5)
    kn = k * lax.rsqrt(jnp.sum(k * k, axis=-1, keepdims=True) + EPS)
    qt = qn.T
    kt = kn.T
    ab = ab_ref[...]
    g = -jnp.exp(alog_ref[...]) * _softplus(ab + dtb_ref[...])
    beta = _sigmoid(ab)
    for h in range(B_HEADS):
        kcol = kt[:, h:h + 1]
        qcol = qt[:, h:h + 1]
        st = s_ref[h] * jnp.exp(g[:, h:h + 1])
        kv = jnp.sum(st * kcol, axis=0, keepdims=True)
        d = (v[h:h + 1] - kv) * beta[:, B_HEADS + h:B_HEADS + h + 1]
        st = st + kcol * d
        sout_ref[h] = st
        o = jnp.sum(st * qcol, axis=0, keepdims=True)
        yb_ref[h:h + 1, :] = _rmsnorm(o, gw_ref[...]) * _silu(zb_ref[h:h + 1, :])


def _gdn_step(state_conv, x, cw, ab, alog_l, dtb_l, zb, gw, state_ssm):
    rows = B_QKV // LANES
    seq3 = lambda b: (b, 0, 0)
    seq4 = lambda b: (b, 0, 0, 0)
    const2 = lambda b: (0, 0)
    return pl.pallas_call(
        _gdn_step_kernel,
        grid=(DEC_BATCH,),
        in_specs=[
            pl.BlockSpec((None, CONV_W - 1, rows, LANES), seq4),
            pl.BlockSpec((None, rows, LANES), seq3),
            pl.BlockSpec((CONV_W, rows, LANES), lambda b: (0, 0, 0)),
            pl.BlockSpec((None, 1, LANES), seq3),
            pl.BlockSpec((1, LANES), const2), pl.BlockSpec((1, LANES), const2),
            pl.BlockSpec((None, B_HEADS, B_DV), seq3),
            pl.BlockSpec((1, B_DV), const2),
            pl.BlockSpec((None, B_HEADS, B_DK, B_DV), seq4),
        ],
        out_specs=[
            pl.BlockSpec((None, B_HEADS, B_DV), seq3),
            pl.BlockSpec((None, B_HEADS, B_DK, B_DV), seq4),
            pl.BlockSpec((None, CONV_W - 1, rows, LANES), seq4),
        ],
        out_shape=[jax.ShapeDtypeStruct((DEC_BATCH, B_HEADS, B_DV), F32),
                   jax.ShapeDtypeStruct((DEC_BATCH, B_HEADS, B_DK, B_DV), F32),
                   jax.ShapeDtypeStruct((DEC_BATCH, CONV_W - 1, rows, LANES), F32)],
        compiler_params=pltpu.CompilerParams(
            dimension_semantics=("arbitrary",), vmem_limit_bytes=VMEM_LIMIT),
        name="gdn_step",
    )(state_conv, x, cw, ab, alog_l, dtb_l, zb, gw, state_ssm)


def kernel(x_prompt, x_sample, cache_k, cache_v, state_conv, state_ssm, page_table, meta_tokens,
           norm_w, w_in, lambda_q1, lambda_k1, lambda_q2, lambda_k2, subln_w, conv_w, a_log,
           dt_bias, gdn_norm_w, w_pa, w_pb, w_o, final_norm_w):
    assert x_prompt.shape == (BATCH, SEQ, D_MODEL) and x_sample.shape == (DEC_BATCH, 1, D_MODEL)
    assert w_in.shape == (1, D_MODEL, D_IN) and page_table.shape == (DEC_BATCH, N_PAGES)
    w = w_in[0]
    nw = norm_w
    lams = (lambda_q1, lambda_k1, lambda_q2, lambda_k2)
    fw = final_norm_w.reshape(1, D_MODEL)
    wpa, wpb, wo = w_pa[0].astype(BF16), w_pb[0].astype(BF16), w_o[0].astype(BF16)
    cw = conv_w[0]

    def lanes8(v, off):
        return jnp.zeros((1, LANES), F32).at[0, off:off + B_HEADS].set(v)

    alog_l, dtb_l = lanes8(a_log[0], 0), lanes8(dt_bias[0], 0)
    alog_c, dtb_c = alog_l[0, 0:2 * B_HEADS].reshape(-1, 1), dtb_l[0, 0:2 * B_HEADS].reshape(-1, 1)
    gvecs = (alog_l, dtb_l, alog_c, dtb_c)

    w_main = jnp.concatenate([w[:, O_QKVB:O_ZB], w[:, O_QA:O_QKVB], w[:, O_ZB:O_A], w[:, O_GA:]],
                             axis=1).astype(BF16)
    w_ab = jnp.pad(w[:, O_A:O_GA], ((0, 0), (0, LANES - 2 * B_HEADS))).astype(BF16)
    tabs_p = _rope_tables(N_META + jnp.arange(SEQ))
    h_main, hab = _inproj(x_prompt.reshape(BATCH * SEQ, D_MODEL), nw, w_main, w_ab, tabs_p)

    xa = jnp.concatenate([x_sample[:, 0, :], meta_tokens,
                          jnp.zeros((AUX_ROWS - DEC_BATCH - N_META, D_MODEL), F32)], axis=0)
    pos_a = jnp.concatenate([jnp.full((DEC_BATCH,), PAST_LEN), jnp.arange(N_META),
                             jnp.zeros((AUX_ROWS - DEC_BATCH - N_META,), jnp.int32)])
    h_aux = _aux_inproj(xa, nw, w, _rope_tables(pos_a))
    hs, hm = h_aux[0:DEC_BATCH], h_aux[DEC_BATCH:DEC_BATCH + N_META]

    pad_rows = CHUNK - N_META
    x_meta = jnp.pad(hm[:, O_QKVB:O_ZB], ((pad_rows, 0), (0, 0)))
    ab_meta = jnp.pad(hm[:, O_A:O_GA], ((pad_rows, 0), (0, LANES - 2 * B_HEADS)))
    abt_meta = ab_meta[:, 0:2 * B_HEADS].T.reshape(1, 1, 2 * B_HEADS, CHUNK)
    _, s_meta = _gdn_chunks(
        x_meta, ab_meta, abt_meta, jnp.zeros((CHUNK, B_HEADS * B_DV), F32), 0, cw, gvecs, gdn_norm_w,
        jnp.zeros((8, B_QKV), F32), jnp.zeros((B_HEADS, B_DK, B_DV), F32), 1, 1, pad_rows)
    k_meta, v_meta = hm[:, O_KA:O_VA], hm[:, O_VA:O_ZA]

    n_chunk = SEQ // CHUNK
    abt = hab[:, 0:2 * B_HEADS].reshape(BATCH, n_chunk, CHUNK, 2 * B_HEADS).transpose(0, 1, 3, 2)
    yb, ssm_p = _gdn_chunks(
        h_main, hab, abt, h_main, M_ZB // (B_HEADS * B_DV), cw, gvecs, gdn_norm_w,
        x_meta[CHUNK - 8:CHUNK], s_meta[0], BATCH, n_chunk, 0)
    ya = _attention(lams, h_main, k_meta, v_meta, subln_w)
    y_prompt = _merge(x_prompt.reshape(BATCH * SEQ, D_MODEL), ya, yb, h_main, h_main,
                      M_GA // D_MODEL, M_GB // D_MODEL, wpa, wpb, wo, fw, 256)

    def with_meta(rows_meta, rows_tok):
        meta = jnp.broadcast_to(rows_meta.reshape(1, N_META, A_HEADS, LANES),
                                (BATCH, N_META, A_HEADS, LANES))
        return jnp.concatenate([meta, rows_tok.reshape(BATCH, SEQ, A_HEADS, LANES)], axis=1)[None]

    k_rows_p = with_meta(k_meta, h_main[:, M_KA:M_VA])
    v_rows_p = with_meta(v_meta, h_main[:, M_VA:M_ZA])
    conv_p = h_main.reshape(BATCH, SEQ, N_MAIN)[:, SEQ - (CONV_W - 1):, M_QKVB:M_QKVB + B_QKV][None]

    heads = lambda t: t.reshape(DEC_BATCH, A_HEADS, LANES)
    q_s = heads(hs[:, O_QA:O_KA]) * (A_DH ** -0.5)
    k_s, v_s = heads(hs[:, O_KA:O_VA]), heads(hs[:, O_VA:O_ZA])
    ya_s = _decode_attention(page_table, lams, q_s, k_s, v_s, heads(hs[:, O_ZA:O_QKVB]), subln_w,
                             cache_k, cache_v)
    rows = B_QKV // LANES
    ab_s = jnp.pad(hs[:, O_A:O_GA], ((0, 0), (0, LANES - 2 * B_HEADS))).reshape(DEC_BATCH, 1, LANES)
    yb_s, ssm_s, conv_s = _gdn_step(
        state_conv[0].reshape(DEC_BATCH, CONV_W - 1, rows, LANES),
        hs[:, O_QKVB:O_ZB].reshape(DEC_BATCH, rows, LANES), cw.reshape(CONV_W, rows, LANES),
        ab_s, alog_l, dtb_l, heads(hs[:, O_ZB:O_A]), gdn_norm_w, state_ssm[0])
    y_sample = _merge(x_sample[:, 0, :], ya_s.reshape(DEC_BATCH, -1).astype(BF16),
                      yb_s.reshape(DEC_BATCH, -1).astype(BF16), hs[:, O_GA:O_GB], hs[:, O_GB:],
                      0, 0, wpa, wpb, wo, fw, DEC_BATCH)

    return (y_prompt.reshape(BATCH, SEQ, D_MODEL), y_sample.reshape(DEC_BATCH, 1, D_MODEL),
            k_rows_p, v_rows_p, conv_p, ssm_p[None],
            k_s.reshape(1, DEC_BATCH, 1, A_HEADS, LANES), v_s.reshape(1, DEC_BATCH, 1, A_HEADS, LANES),
            conv_s.reshape(1, DEC_BATCH, CONV_W - 1, B_QKV), ssm_s[None])
```

```python
import functools
import math

import jax
import jax.numpy as jnp
from jax import lax
from jax.experimental import pallas as pl
from jax.experimental.pallas import tpu as pltpu

F32 = jnp.float32
BF16 = jnp.bfloat16

D_MODEL = 2048
BATCH = 4
SEQ = 2048
DEC_BATCH = 32
PAST_LEN = 8192
PAGE_SIZE = 128
N_PAGES = PAST_LEN // PAGE_SIZE
N_META = 16
A_HEADS = 8
A_DH = 64
A_DV = 128
ROPE_DIM = 16
ROPE_THETA = 500000.0
B_HEADS = 8
B_DK = 128
B_DV = 128
B_QKV = 3072
CONV_W = 4
CHUNK = 64
EPS = 1e-6
NEG = -1e30
LAMBDA_INIT = 0.8 - 0.6 * math.exp(-0.3 * 0)

O_QA, O_KA, O_VA, O_ZA, O_QKVB, O_ZB, O_A, O_GA, O_GB, D_IN = (
    0, 1024, 2048, 3072, 4096, 7168, 8192, 8208, 10256, 12304)
M_QKVB, M_QA, M_KA, M_VA, M_ZA, M_ZB, M_GA, M_GB, N_MAIN = (
    0, 3072, 4096, 5120, 6144, 7168, 8192, 10240, 12288)

LANES = 128
VMEM_LIMIT = 56 * 1024 * 1024


def _dot(a, b):
    return jnp.dot(a, b, preferred_element_type=F32)


def _dot_nt(a, b):
    return lax.dot_general(a, b, (((1,), (1,)), ((), ())), preferred_element_type=F32)


def _dot_f32(a, b):
    return jnp.dot(a, b, preferred_element_type=F32, precision=lax.Precision.HIGHEST)


def _sigmoid(x):
    return 1.0 / (1.0 + jnp.exp(-x))


def _silu(x):
    return x * _sigmoid(x)


def _softplus(x):
    return jnp.maximum(x, 0.0) + jnp.log1p(jnp.exp(-jnp.abs(x)))


def _rmsnorm(x, w):
    return x * lax.rsqrt(jnp.mean(x * x, axis=-1, keepdims=True) + EPS) * w


def _rope_tables(pos):
    r = pos.shape[0]
    inv_freq = ROPE_THETA ** (-jnp.arange(0, ROPE_DIM, 2, dtype=F32) / ROPE_DIM)
    ang = pos.astype(F32)[:, None] * inv_freq[None, :]
    cos, sin = jnp.cos(ang), jnp.sin(ang)
    half = ROPE_DIM // 2
    rest = A_DH - ROPE_DIM
    c = jnp.concatenate([cos, cos, jnp.ones((r, rest), F32)], axis=1)
    sa = jnp.concatenate([jnp.zeros((r, half), F32), sin, jnp.zeros((r, rest), F32)], axis=1)
    sb = jnp.concatenate([-sin, jnp.zeros((r, half + rest), F32)], axis=1)
    return tuple(jnp.tile(t, (1, LANES // A_DH)) for t in (c, sa, sb))


def _rope_tile(t, c, sa, sb):
    out = []
    for i in range(t.shape[1] // LANES):
        x = t[:, i * LANES:(i + 1) * LANES]
        out.append(x * c + pltpu.roll(x, ROPE_DIM // 2, 1) * sa
                   + pltpu.roll(x, LANES - ROPE_DIM // 2, 1) * sb)
    return jnp.concatenate(out, axis=1) if len(out) > 1 else out[0]


IP_TM = 1024
IP_TN = 512
IP_ROWS = 256


def _inproj_kernel(x_ref, nw_ref, w_ref, wab_ref, c_ref, sa_ref, sb_ref, h_ref, hab_ref, xn_ref):
    j = pl.program_id(1)

    @pl.when(j == 0)
    def _():
        def body(r, carry):
            rows = pl.ds(pl.multiple_of(r * IP_ROWS, IP_ROWS), IP_ROWS)
            xn_ref[rows, :] = _rmsnorm(x_ref[rows, :], nw_ref[...]).astype(BF16)
            return carry
        lax.fori_loop(0, IP_TM // IP_ROWS, body, 0)
        hab_ref[...] = _dot(xn_ref[...], wab_ref[...])

    acc = _dot(xn_ref[...], w_ref[...])
    is_rope = (j >= M_QA // IP_TN) & (j < M_VA // IP_TN)

    @pl.when(is_rope)
    def _():
        h_ref[...] = _rope_tile(acc, c_ref[...], sa_ref[...], sb_ref[...])

    @pl.when(jnp.logical_not(is_rope))
    def _():
        h_ref[...] = acc


def _inproj(x, nw, w_main, w_ab, tabs):
    m = x.shape[0]
    per_seq = SEQ // IP_TM
    tab_spec = pl.BlockSpec((IP_TM, LANES), lambda i, j: (i % per_seq, 0))
    return pl.pallas_call(
        _inproj_kernel,
        grid=(m // IP_TM, N_MAIN // IP_TN),
        in_specs=[
            pl.BlockSpec((IP_TM, D_MODEL), lambda i, j: (i, 0)),
            pl.BlockSpec((1, D_MODEL), lambda i, j: (0, 0)),
            pl.BlockSpec((D_MODEL, IP_TN), lambda i, j: (0, j)),
            pl.BlockSpec((D_MODEL, LANES), lambda i, j: (0, 0)),
            tab_spec, tab_spec, tab_spec,
        ],
        out_specs=[
            pl.BlockSpec((IP_TM, IP_TN), lambda i, j: (i, j)),
            pl.BlockSpec((IP_TM, LANES), lambda i, j: (i, 0)),
        ],
        out_shape=[jax.ShapeDtypeStruct((m, N_MAIN), F32),
                   jax.ShapeDtypeStruct((m, LANES), F32)],
        scratch_shapes=[pltpu.VMEM((IP_TM, D_MODEL), BF16)],
        compiler_params=pltpu.CompilerParams(
            dimension_semantics=("arbitrary", "arbitrary"), vmem_limit_bytes=VMEM_LIMIT),
        name="inproj",
    )(x, nw, w_main, w_ab, *tabs)


AUX_ROWS = 64
AUX_TN = 512


def _aux_inproj_kernel(x_ref, nw_ref, w_ref, c_ref, sa_ref, sb_ref, o_ref, xs_ref):
    j = pl.program_id(0)

    @pl.when(j == 0)
    def _():
        xn = _rmsnorm(x_ref[...], nw_ref[...])
        hi = xn.astype(BF16)
        xs_ref[0:AUX_ROWS, :] = hi
        xs_ref[AUX_ROWS:2 * AUX_ROWS, :] = (xn - hi.astype(F32)).astype(BF16)

    w = w_ref[...]
    w_hi = w.astype(BF16)
    w_lo = (w - w_hi.astype(F32)).astype(BF16)
    r1 = _dot(xs_ref[...], w_hi)
    r2 = _dot(xs_ref[0:AUX_ROWS, :], w_lo)
    acc = r1[0:AUX_ROWS] + (r1[AUX_ROWS:] + r2)
    is_rope = j < O_VA // AUX_TN

    @pl.when(is_rope)
    def _():
        o_ref[...] = _rope_tile(acc, c_ref[...], sa_ref[...], sb_ref[...])

    @pl.when(jnp.logical_not(is_rope))
    def _():
        o_ref[...] = acc


def _aux_inproj(xa, nw, w, tabs):
    tab_spec = pl.BlockSpec((AUX_ROWS, LANES), lambda j: (0, 0))
    return pl.pallas_call(
        _aux_inproj_kernel,
        grid=(pl.cdiv(D_IN, AUX_TN),),
        in_specs=[
            pl.BlockSpec((AUX_ROWS, D_MODEL), lambda j: (0, 0)),
            pl.BlockSpec((1, D_MODEL), lambda j: (0, 0)),
            pl.BlockSpec((D_MODEL, AUX_TN), lambda j: (0, j)),
            tab_spec, tab_spec, tab_spec,
        ],
        out_specs=pl.BlockSpec((AUX_ROWS, AUX_TN), lambda j: (0, j)),
        out_shape=jax.ShapeDtypeStruct((AUX_ROWS, D_IN), F32),
        scratch_shapes=[pltpu.VMEM((2 * AUX_ROWS, D_MODEL), BF16)],
        compiler_params=pltpu.CompilerParams(
            dimension_semantics=("arbitrary",), vmem_limit_bytes=VMEM_LIMIT),
        name="aux_inproj",
    )(xa, nw, w, *tabs)


AT_TQ = 512
AT_TK = 512
LOG2E = math.log2(math.e)


def _diff_lambda(lq1, lk1, lq2, lk2):
    a = jnp.exp(jnp.sum(lq1 * lk1, axis=-1, keepdims=True))
    b = jnp.exp(jnp.sum(lq2 * lk2, axis=-1, keepdims=True))
    return a - b + LAMBDA_INIT


def _attn_kernel(lq1_ref, lk1_ref, lq2_ref, lk2_ref, q_ref, k_ref, v_ref, km_ref, vm_ref,
                 za_ref, sw_ref, o_ref, kb_ref, vb_ref, kmb_ref, vmb_ref, q2_ref, s_ref, sm_ref,
                 m_ref, l_ref, acc_ref):
    qi = pl.program_id(2)
    tq, tk = AT_TQ, AT_TK
    n_tiles = tk // LANES

    @pl.when(qi == 0)
    def _():
        kb_ref[...] = k_ref[...].astype(BF16)
        vb_ref[...] = v_ref[...].astype(BF16)
        pad = jnp.zeros((LANES - N_META, LANES), BF16)
        kmb_ref[...] = jnp.concatenate([km_ref[...].astype(BF16), pad], axis=0)
        vmb_ref[...] = jnp.concatenate([vm_ref[...].astype(BF16), pad], axis=0)

    q = q_ref[...] * (A_DH ** -0.5 * LOG2E)
    lane = lax.broadcasted_iota(jnp.int32, (tq, LANES), 1)
    q2_ref[0:tq, :] = jnp.where(lane < A_DH, q, 0.0).astype(BF16)
    q2_ref[tq:2 * tq, :] = jnp.where(lane >= A_DH, q, 0.0).astype(BF16)

    def tile_max(m, s):
        for c in range(s.shape[1] // LANES):
            m = jnp.maximum(m, s[:, c * LANES:(c + 1) * LANES])
        return m

    lane2 = lax.broadcasted_iota(jnp.int32, (2 * tq, LANES), 1)
    s = jnp.where(lane2 < N_META, _dot_nt(q2_ref[...], kmb_ref[...]), NEG)
    sm_ref[...] = s
    m_ref[...] = s

    def scores(j):
        rows = pl.ds(pl.multiple_of(j * tk, tk), tk)
        return _dot_nt(q2_ref[...], kb_ref[rows, :])

    def pass1(j, carry):
        s = scores(j)
        s_ref[j] = s
        m_ref[...] = tile_max(m_ref[...], s)
        return carry
    lax.fori_loop(0, qi, pass1, 0)

    r = lax.broadcasted_iota(jnp.int32, (2 * tq, tk), 0)
    r = jnp.where(r >= tq, r - tq, r)
    c = lax.broadcasted_iota(jnp.int32, (2 * tq, tk), 1)
    s = jnp.where(c <= r, scores(qi), NEG)
    s_ref[qi] = s
    m = jnp.max(tile_max(m_ref[...], s), axis=-1, keepdims=True)
    m_ref[...] = jnp.broadcast_to(m, (2 * tq, LANES))

    p = jnp.exp2(sm_ref[...] - m_ref[...])
    l_ref[...] = p
    acc_ref[...] = _dot(p.astype(BF16), vmb_ref[...])

    def pass2(j, carry):
        mb = m_ref[...]
        lsum = l_ref[...]
        ps = []
        for t in range(n_tiles):
            p = jnp.exp2(s_ref[j, :, t * LANES:(t + 1) * LANES] - mb)
            lsum = lsum + p
            ps.append(p.astype(BF16))
        l_ref[...] = lsum
        rows = pl.ds(pl.multiple_of(j * tk, tk), tk)
        acc_ref[...] += _dot(jnp.concatenate(ps, axis=1), vb_ref[rows, :])
        return carry
    lax.fori_loop(0, qi + 1, pass2, 0)

    lam = _diff_lambda(lq1_ref[...], lk1_ref[...], lq2_ref[...], lk2_ref[...])
    l = jnp.sum(l_ref[...], axis=-1, keepdims=True)
    o = acc_ref[0:tq, :] / l[0:tq] - lam * (acc_ref[tq:2 * tq, :] / l[tq:2 * tq])
    y = _rmsnorm(o, sw_ref[...]) * (1.0 - LAMBDA_INIT)
    o_ref[...] = (y * _silu(za_ref[...])).astype(BF16)


def _attention(lams, h_main, k_meta, v_meta, subln_w):
    nq = SEQ // AT_TQ
    lam_spec = pl.BlockSpec((1, A_DH), lambda b, h, i: (0, 0))
    return pl.pallas_call(
        _attn_kernel,
        grid=(BATCH, A_HEADS, nq),
        in_specs=[
            lam_spec, lam_spec, lam_spec, lam_spec,
            pl.BlockSpec((AT_TQ, LANES), lambda b, h, i: (b * nq + i, M_QA // LANES + h)),
            pl.BlockSpec((SEQ, LANES), lambda b, h, i: (b, M_KA // LANES + h)),
            pl.BlockSpec((SEQ, LANES), lambda b, h, i: (b, M_VA // LANES + h)),
            pl.BlockSpec((N_META, LANES), lambda b, h, i: (0, h)),
            pl.BlockSpec((N_META, LANES), lambda b, h, i: (0, h)),
            pl.BlockSpec((AT_TQ, LANES), lambda b, h, i: (b * nq + i, M_ZA // LANES + h)),
            pl.BlockSpec((1, A_DV), lambda b, h, i: (0, 0)),
        ],
        out_specs=pl.BlockSpec((AT_TQ, LANES), lambda b, h, i: (b * nq + i, h)),
        out_shape=jax.ShapeDtypeStruct((BATCH * SEQ, A_HEADS * A_DV), BF16),
        scratch_shapes=[
            pltpu.VMEM((SEQ, LANES), BF16), pltpu.VMEM((SEQ, LANES), BF16),
            pltpu.VMEM((LANES, LANES), BF16), pltpu.VMEM((LANES, LANES), BF16),
            pltpu.VMEM((2 * AT_TQ, LANES), BF16),
            pltpu.VMEM((SEQ // AT_TK, 2 * AT_TQ, AT_TK), F32),
            pltpu.VMEM((2 * AT_TQ, LANES), F32),
            pltpu.VMEM((2 * AT_TQ, LANES), F32), pltpu.VMEM((2 * AT_TQ, LANES), F32),
            pltpu.VMEM((2 * AT_TQ, A_DV), F32),
        ],
        compiler_params=pltpu.CompilerParams(
            dimension_semantics=("arbitrary", "arbitrary", "arbitrary"),
            vmem_limit_bytes=VMEM_LIMIT),
        name="diff_attn",
    )(*lams, h_main, h_main, h_main, k_meta, v_meta, h_main, subln_w)


def _unit_lower_inverse_minus_eye(a_list, i_idx, j_idx):
    base = 8
    diag = (i_idx // base) == (j_idx // base)
    b = [jnp.where(diag, a, 0.0) for a in a_list]
    n = [-x for x in b]
    for _ in range(2):
        bb = [x.astype(BF16) for x in b]
        b = [_dot(x, x) for x in bb]
        nb = [_dot(x.astype(BF16), y.astype(BF16)) for x, y in zip(n, b)]
        n = [x + y + z for x, y, z in zip(n, b, nb)]
    s = base
    while s < CHUNK:
        join = ((i_idx // (2 * s)) == (j_idx // (2 * s))) & ((i_idx // s) % 2 == 1) & ((j_idx // s) % 2 == 0)
        a_s = [jnp.where(join, a, 0.0) for a in a_list]
        x = [p + _dot(q.astype(BF16), p.astype(BF16)) for p, q in zip(a_s, n)]
        xn = [_dot(p.astype(BF16), q.astype(BF16)) for p, q in zip(x, n)]
        n = [q - (p + r) for q, p, r in zip(n, x, xn)]
        s *= 2
    return n


def _gdn_kernel(x_ref, ab_ref, abt_ref, zb_ref, cw_ref, alog_ref, dtb_ref, alogc_ref, dtbc_ref,
                gw_ref, halo_ref, s0_ref, yb_ref, sfin_ref, ext_ref, s_ref, *, masked_rows):
    c = pl.program_id(1)
    hist = 8

    @pl.when(c == 0)
    def _():
        ext_ref[0:hist, :] = halo_ref[...]
        s_ref[...] = s0_ref[...]

    x = x_ref[...]
    ext_ref[hist:hist + CHUNK, :] = x
    y = cw_ref[CONV_W - 1:CONV_W, :] * x
    for t in range(CONV_W - 1):
        lo = hist - (CONV_W - 1) + t
        y = y + cw_ref[t:t + 1, :] * ext_ref[lo:lo + CHUNK, :]
    y = _silu(y)
    ext_ref[0:hist, :] = x[CHUNK - hist:CHUNK, :]

    i_idx = lax.broadcasted_iota(jnp.int32, (CHUNK, CHUNK), 0)
    j_idx = lax.broadcasted_iota(jnp.int32, (CHUNK, CHUNK), 1)
    tril = i_idx >= j_idx
    strict = i_idx > j_idx
    tril_f = tril.astype(F32)
    triu_f = (i_idx <= j_idx).astype(F32)

    ab = ab_ref[...]
    g_c = -jnp.exp(alog_ref[...]) * _softplus(ab + dtb_ref[...])
    beta_c = _sigmoid(ab)
    abt = abt_ref[...]
    g_r = -jnp.exp(alogc_ref[...]) * _softplus(abt + dtbc_ref[...])
    if masked_rows:
        row_ok = lax.broadcasted_iota(jnp.int32, (CHUNK, LANES), 0) >= masked_rows
        col_ok = lax.broadcasted_iota(jnp.int32, (2 * B_HEADS, CHUNK), 1) >= masked_rows
        g_c = jnp.where(row_ok, g_c, 0.0)
        beta_c = jnp.where(row_ok, beta_c, 0.0)
        g_r = jnp.where(col_ok, g_r, 0.0)
    gc_c = _dot_f32(tril_f, g_c)
    gc_r = _dot_f32(g_r, triu_f)

    nk = B_HEADS * B_DK
    heads = range(B_HEADS)

    def l2n(t):
        return t * lax.rsqrt(jnp.sum(t * t, axis=-1, keepdims=True) + EPS)

    qn = [l2n(y[:, h * B_DK:(h + 1) * B_DK]) * (B_DK ** -0.5) for h in heads]
    kn = [l2n(y[:, nk + h * B_DK:nk + (h + 1) * B_DK]) for h in heads]
    vh = [y[:, 2 * nk + h * B_DV:2 * nk + (h + 1) * B_DV] for h in heads]
    bcol = [beta_c[:, B_HEADS + h:B_HEADS + h + 1] for h in heads]
    gcc = [gc_c[:, h:h + 1] for h in heads]
    decay = [jnp.where(tril, jnp.exp(jnp.where(tril, gcc[h] - gc_r[h:h + 1, :], 0.0)), 0.0)
             for h in heads]
    kbeta = [kn[h] * bcol[h] for h in heads]
    kn_b = [t.astype(BF16) for t in kn]
    kk = [_dot_nt(kbeta[h].astype(BF16), kn_b[h]) for h in heads]
    qk = [_dot_nt(qn[h].astype(BF16), kn_b[h]) for h in heads]
    a = [jnp.where(strict, kk[h] * decay[h], 0.0) for h in heads]
    n = _unit_lower_inverse_minus_eye(a, i_idx, j_idx)
    egc = [jnp.exp(t) for t in gcc]
    rhs = [jnp.concatenate([vh[h] * bcol[h], kbeta[h] * egc[h]], axis=1) for h in heads]
    nr = [_dot(n[h].astype(BF16), rhs[h].astype(BF16)) for h in heads]
    sol = [rhs[h] + nr[h] for h in heads]
    st = [s_ref[h] for h in heads]
    st_b = [t.astype(BF16) for t in st]
    ws = [_dot(sol[h][:, B_DV:B_DV + B_DK].astype(BF16), st_b[h]) for h in heads]
    qs = [_dot((qn[h] * egc[h]).astype(BF16), st_b[h]) for h in heads]
    v_new_b = [(sol[h][:, 0:B_DV] - ws[h]).astype(BF16) for h in heads]
    av = [_dot((qk[h] * decay[h]).astype(BF16), v_new_b[h]) for h in heads]
    g_last = [t[CHUNK - 1:CHUNK, :] for t in gcc]
    ke_t = [(kn[h] * jnp.exp(g_last[h] - gcc[h])).T.astype(BF16) for h in heads]
    kv = [_dot(ke_t[h], v_new_b[h]) for h in heads]
    for h in heads:
        s_ref[h] = st[h] * jnp.exp(g_last[h]) + kv[h]
    for h in heads:
        zb = zb_ref[:, h * B_DV:(h + 1) * B_DV]
        yb_ref[:, h * B_DV:(h + 1) * B_DV] = (_rmsnorm(qs[h] + av[h], gw_ref[...]) * _silu(zb)).astype(BF16)

    sfin_ref[...] = s_ref[...]


def _gdn_chunks(x, ab, abt, zsrc, zb_col, cw, gvecs, gw, halo, s0, n_seq, n_chunk, masked_rows):
    alog_l, dtb_l, alog_c, dtb_c = gvecs
    const2 = lambda b, c: (0, 0)
    kern = functools.partial(_gdn_kernel, masked_rows=masked_rows)
    return pl.pallas_call(
        kern,
        grid=(n_seq, n_chunk),
        in_specs=[
            pl.BlockSpec((CHUNK, B_QKV), lambda b, c: (b * n_chunk + c, 0)),
            pl.BlockSpec((CHUNK, LANES), lambda b, c: (b * n_chunk + c, 0)),
            pl.BlockSpec((None, None, 2 * B_HEADS, CHUNK), lambda b, c: (b, c, 0, 0)),
            pl.BlockSpec((CHUNK, B_HEADS * B_DV), lambda b, c: (b * n_chunk + c, zb_col)),
            pl.BlockSpec((CONV_W, B_QKV), const2),
            pl.BlockSpec((1, LANES), const2), pl.BlockSpec((1, LANES), const2),
            pl.BlockSpec((2 * B_HEADS, 1), const2), pl.BlockSpec((2 * B_HEADS, 1), const2),
            pl.BlockSpec((1, B_DV), const2),
            pl.BlockSpec((8, B_QKV), const2),
            pl.BlockSpec((B_HEADS, B_DK, B_DV), lambda b, c: (0, 0, 0)),
        ],
        out_specs=[
            pl.BlockSpec((CHUNK, B_HEADS * B_DV), lambda b, c: (b * n_chunk + c, 0)),
            pl.BlockSpec((None, B_HEADS, B_DK, B_DV), lambda b, c: (b, 0, 0, 0)),
        ],
        out_shape=[jax.ShapeDtypeStruct((n_seq * n_chunk * CHUNK, B_HEADS * B_DV), BF16),
                   jax.ShapeDtypeStruct((n_seq, B_HEADS, B_DK, B_DV), F32)],
        scratch_shapes=[pltpu.VMEM((8 + CHUNK, B_QKV), F32),
                        pltpu.VMEM((B_HEADS, B_DK, B_DV), F32)],
        compiler_params=pltpu.CompilerParams(
            dimension_semantics=("arbitrary", "arbitrary"), vmem_limit_bytes=VMEM_LIMIT),
        name="gdn_chunks",
    )(x, ab, abt, zsrc, cw, alog_l, dtb_l, alog_c, dtb_c, gw, halo, s0)


def _merge_kernel(x_ref, ya_ref, yb_ref, ga_ref, gb_ref, wpa_ref, wpb_ref, wo_ref, fw_ref, y_ref):
    pa = _dot(ya_ref[...], wpa_ref[...])
    pb = _dot(yb_ref[...], wpb_ref[...])
    mixed = _sigmoid(ga_ref[...]) * pa + _sigmoid(gb_ref[...]) * pb
    hp = x_ref[...] + _dot(mixed.astype(BF16), wo_ref[...])
    y_ref[...] = _rmsnorm(hp, fw_ref[...])


def _merge(x, ya, yb, gsrc_a, gsrc_b, ga_col, gb_col, wpa, wpb, wo, fw, tm):
    m = x.shape[0]
    once = pl.Buffered(1)
    return pl.pallas_call(
        _merge_kernel,
        grid=(m // tm,),
        in_specs=[
            pl.BlockSpec((tm, D_MODEL), lambda i: (i, 0)),
            pl.BlockSpec((tm, A_HEADS * A_DV), lambda i: (i, 0)),
            pl.BlockSpec((tm, B_HEADS * B_DV), lambda i: (i, 0)),
            pl.BlockSpec((tm, D_MODEL), lambda i: (i, ga_col)),
            pl.BlockSpec((tm, D_MODEL), lambda i: (i, gb_col)),
            pl.BlockSpec((A_HEADS * A_DV, D_MODEL), lambda i: (0, 0), pipeline_mode=once),
            pl.BlockSpec((B_HEADS * B_DV, D_MODEL), lambda i: (0, 0), pipeline_mode=once),
            pl.BlockSpec((D_MODEL, D_MODEL), lambda i: (0, 0), pipeline_mode=once),
            pl.BlockSpec((1, D_MODEL), lambda i: (0, 0)),
        ],
        out_specs=pl.BlockSpec((tm, D_MODEL), lambda i: (i, 0)),
        out_shape=jax.ShapeDtypeStruct((m, D_MODEL), F32),
        compiler_params=pltpu.CompilerParams(
            dimension_semantics=("arbitrary",), vmem_limit_bytes=VMEM_LIMIT),
        name="merge",
    )(x, ya, yb, gsrc_a, gsrc_b, wpa, wpb, wo, fw)


DA_P = 8
DA_G = N_PAGES // DA_P
TOK_TILES = PAGE_SIZE * A_HEADS // LANES


def _lane_group_reduce(x, op):
    s = A_HEADS
    while s < LANES:
        x = op(x, pltpu.roll(x, s, 1))
        s *= 2
    return x


def _decode_attn_kernel(pt_ref, lq1_ref, lk1_ref, lq2_ref, lk2_ref, q_ref, kn_ref, vn_ref,
                        za_ref, sw_ref, *rest):
    k_refs = rest[0:DA_P]
    v_refs = rest[DA_P:2 * DA_P]
    o_ref, sc_ref, acc_ref, wn_ref = rest[2 * DA_P:]
    j = pl.program_id(1)
    sub = lax.broadcasted_iota(jnp.int32, (A_HEADS, LANES), 0)
    lane = lax.broadcasted_iota(jnp.int32, (A_HEADS, LANES), 1)
    head_of_lane = lane % A_HEADS
    hmask = head_of_lane == sub
    half_sel = jnp.where((sub == 0) & (lane < A_DH), 1.0,
                         jnp.where((sub == 1) & (lane >= A_DH), 1.0, 0.0)).astype(BF16)

    @pl.when(j < DA_G)
    def _():
        q = q_ref[...]
        for p in range(DA_P):
            prod = (k_refs[p][...] * q[None]).reshape(PAGE_SIZE * A_HEADS, LANES)
            s = _dot_nt(half_sel, prod.astype(BF16))
            base = (j * DA_P + p) * TOK_TILES
            for c in range(TOK_TILES):
                sc_ref[base + c] = s[:, c * LANES:(c + 1) * LANES]

    @pl.when(j == DA_G)
    def _():
        lam = _diff_lambda(lq1_ref[...], lk1_ref[...], lq2_ref[...], lk2_ref[...])
        prod = q_ref[...] * kn_ref[...]
        hs1 = jnp.sum(jnp.where(lane < A_DH, prod, 0.0), axis=1, keepdims=True)
        hs2 = jnp.sum(jnp.where(lane >= A_DH, prod, 0.0), axis=1, keepdims=True)
        row1 = jnp.sum(jnp.where(hmask, hs1, 0.0), axis=0, keepdims=True)
        row2 = jnp.sum(jnp.where(hmask, hs2, 0.0), axis=0, keepdims=True)
        s_new = jnp.where(sub == 0, row1, jnp.where(sub == 1, row2, 0.0))
        sc = sc_ref[...]
        mx = _lane_group_reduce(jnp.max(sc, axis=0), jnp.maximum)
        mx = jnp.maximum(mx, s_new)
        p = jnp.exp(sc - mx[None])
        p_new = jnp.exp(s_new - mx)
        den = _lane_group_reduce(jnp.sum(p, axis=0), jnp.add) + p_new
        coef = jnp.where(sub == 0, 1.0 / den, jnp.where(sub == 1, -lam / den, 0.0))
        sc_ref[...] = p * coef[None]
        wn_ref[...] = jnp.sum(p_new * coef, axis=0, keepdims=True)
        acc_ref[...] = jnp.zeros_like(acc_ref)

    @pl.when(j >= DA_G)
    def _():
        acc = acc_ref[...]
        for p in range(DA_P):
            base = ((j - DA_G) * DA_P + p) * TOK_TILES
            tiles = []
            for c in range(TOK_TILES):
                r = jnp.sum(sc_ref[base + c], axis=0, keepdims=True)
                tiles.append(jnp.where(hmask, r, 0.0))
            w = jnp.concatenate(tiles, axis=1)
            w_hi = w.astype(BF16)
            w_lo = (w - w_hi.astype(F32)).astype(BF16)
            v = v_refs[p][...].reshape(PAGE_SIZE * A_HEADS, LANES).astype(BF16)
            r = _dot(jnp.concatenate([w_hi, w_lo], axis=0), v)
            acc = acc + (r[0:A_HEADS] + r[A_HEADS:])
        acc_ref[...] = acc

    @pl.when(j == 2 * DA_G - 1)
    def _():
        w_new = jnp.sum(jnp.where(hmask, wn_ref[...], 0.0), axis=1, keepdims=True) * (A_HEADS / LANES)
        o = acc_ref[...] + w_new * vn_ref[...]
        y = _rmsnorm(o, sw_ref[...]) * (1.0 - LAMBDA_INIT)
        o_ref[...] = y * _silu(za_ref[...])


def _decode_attention(page_table, lams, q_s, k_new, v_new, za_s, subln_w, cache_k, cache_v):
    lam_spec = pl.BlockSpec((1, A_DH), lambda b, j, pt: (0, 0))
    row_spec = pl.BlockSpec((None, A_HEADS, LANES), lambda b, j, pt: (b, 0, 0))
    page_block = (None, None, PAGE_SIZE, A_HEADS, LANES)

    def k_spec(p):
        return pl.BlockSpec(page_block, lambda b, j, pt: (0, pt[b, jnp.minimum(j, DA_G - 1) * DA_P + p], 0, 0, 0))

    def v_spec(p):
        return pl.BlockSpec(page_block, lambda b, j, pt: (0, pt[b, jnp.maximum(j - DA_G, 0) * DA_P + p], 0, 0, 0))

    grid_spec = pltpu.PrefetchScalarGridSpec(
        num_scalar_prefetch=1,
        grid=(DEC_BATCH, 2 * DA_G),
        in_specs=[lam_spec, lam_spec, lam_spec, lam_spec, row_spec, row_spec, row_spec, row_spec,
                  pl.BlockSpec((1, A_DV), lambda b, j, pt: (0, 0))]
                 + [k_spec(p) for p in range(DA_P)] + [v_spec(p) for p in range(DA_P)],
        out_specs=row_spec,
        scratch_shapes=[pltpu.VMEM((N_PAGES * TOK_TILES, A_HEADS, LANES), F32),
                        pltpu.VMEM((A_HEADS, A_DV), F32),
                        pltpu.VMEM((1, LANES), F32)],
    )
    return pl.pallas_call(
        _decode_attn_kernel,
        grid_spec=grid_spec,
        out_shape=jax.ShapeDtypeStruct((DEC_BATCH, A_HEADS, A_DV), F32),
        compiler_params=pltpu.CompilerParams(
            dimension_semantics=("arbitrary", "arbitrary"), vmem_limit_bytes=VMEM_LIMIT),
        name="decode_attn",
    )(page_table, *lams, q_s, k_new, v_new, za_s, subln_w,
      *([cache_k] * DA_P), *([cache_v] * DA_P))


def _gdn_step_kernel(sc_ref, x_ref, cw_ref, ab_ref, alog_ref, dtb_ref, zb_ref, gw_ref, s_ref,
                     yb_ref, sout_ref, cout_ref):
    x = x_ref[...]
    y = cw_ref[CONV_W - 1] * x
    for t in range(CONV_W - 1):
        y = y + cw_ref[t] * sc_ref[t]
    y = _silu(y)
    for t in range(CONV_W - 2):
        cout_ref[t] = sc_ref[t + 1]
    cout_ref[CONV_W - 2] = x

    q = y[0:B_HEADS]
    k = y[B_HEADS:2 * B_HEADS]
    v = y[2 * B_HEADS:3 * B_HEADS]
    qn = q * lax.rsqrt(jnp.sum(q * q, axis=-1, keepdims=True) + EPS) * (B_DK ** -0.5)
    kn = k * lax.rsqrt(jnp.sum(k * k, axis=-1, keepdims=True) + EPS)
    qt = qn.T
    kt = kn.T
    ab = ab_ref[...]
    g = -jnp.exp(alog_ref[...]) * _softplus(ab + dtb_ref[...])
    beta = _sigmoid(ab)
    for h in range(B_HEADS):
        kcol = kt[:, h:h + 1]
        qcol = qt[:, h:h + 1]
        st = s_ref[h] * jnp.exp(g[:, h:h + 1])
        kv = jnp.sum(st * kcol, axis=0, keepdims=True)
        d = (v[h:h + 1] - kv) * beta[:, B_HEADS + h:B_HEADS + h + 1]
        st = st + kcol * d
        sout_ref[h] = st
        o = jnp.sum(st * qcol, axis=0, keepdims=True)
        yb_ref[h:h + 1, :] = _rmsnorm(o, gw_ref[...]) * _silu(zb_ref[h:h + 1, :])


def _gdn_step(state_conv, x, cw, ab, alog_l, dtb_l, zb, gw, state_ssm):
    rows = B_QKV // LANES
    seq3 = lambda b: (b, 0, 0)
    seq4 = lambda b: (b, 0, 0, 0)
    const2 = lambda b: (0, 0)
    return pl.pallas_call(
        _gdn_step_kernel,
        grid=(DEC_BATCH,),
        in_specs=[
            pl.BlockSpec((None, CONV_W - 1, rows, LANES), seq4),
            pl.BlockSpec((None, rows, LANES), seq3),
            pl.BlockSpec((CONV_W, rows, LANES), lambda b: (0, 0, 0)),
            pl.BlockSpec((None, 1, LANES), seq3),
            pl.BlockSpec((1, LANES), const2), pl.BlockSpec((1, LANES), const2),
            pl.BlockSpec((None, B_HEADS, B_DV), seq3),
            pl.BlockSpec((1, B_DV), const2),
            pl.BlockSpec((None, B_HEADS, B_DK, B_DV), seq4),
        ],
        out_specs=[
            pl.BlockSpec((None, B_HEADS, B_DV), seq3),
            pl.BlockSpec((None, B_HEADS, B_DK, B_DV), seq4),
            pl.BlockSpec((None, CONV_W - 1, rows, LANES), seq4),
        ],
        out_shape=[jax.ShapeDtypeStruct((DEC_BATCH, B_HEADS, B_DV), F32),
                   jax.ShapeDtypeStruct((DEC_BATCH, B_HEADS, B_DK, B_DV), F32),
                   jax.ShapeDtypeStruct((DEC_BATCH, CONV_W - 1, rows, LANES), F32)],
        compiler_params=pltpu.CompilerParams(
            dimension_semantics=("arbitrary",), vmem_limit_bytes=VMEM_LIMIT),
        name="gdn_step",
    )(state_conv, x, cw, ab, alog_l, dtb_l, zb, gw, state_ssm)


def kernel(x_prompt, x_sample, cache_k, cache_v, state_conv, state_ssm, page_table, meta_tokens,
           norm_w, w_in, lambda_q1, lambda_k1, lambda_q2, lambda_k2, subln_w, conv_w, a_log,
           dt_bias, gdn_norm_w, w_pa, w_pb, w_o, final_norm_w):
    assert x_prompt.shape == (BATCH, SEQ, D_MODEL) and x_sample.shape == (DEC_BATCH, 1, D_MODEL)
    assert w_in.shape == (1, D_MODEL, D_IN) and page_table.shape == (DEC_BATCH, N_PAGES)
    w = w_in[0]
    nw = norm_w
    lams = (lambda_q1, lambda_k1, lambda_q2, lambda_k2)
    fw = final_norm_w.reshape(1, D_MODEL)
    wpa, wpb, wo = w_pa[0].astype(BF16), w_pb[0].astype(BF16), w_o[0].astype(BF16)
    cw = conv_w[0]

    def lanes8(v, off):
        return jnp.zeros((1, LANES), F32).at[0, off:off + B_HEADS].set(v)

    alog_l, dtb_l = lanes8(a_log[0], 0), lanes8(dt_bias[0], 0)
    alog_c, dtb_c = alog_l[0, 0:2 * B_HEADS].reshape(-1, 1), dtb_l[0, 0:2 * B_HEADS].reshape(-1, 1)
    gvecs = (alog_l, dtb_l, alog_c, dtb_c)

    w_main = jnp.concatenate([w[:, O_QKVB:O_ZB], w[:, O_QA:O_QKVB], w[:, O_ZB:O_A], w[:, O_GA:]],
                             axis=1).astype(BF16)
    w_ab = jnp.pad(w[:, O_A:O_GA], ((0, 0), (0, LANES - 2 * B_HEADS))).astype(BF16)
    tabs_p = _rope_tables(N_META + jnp.arange(SEQ))
    h_main, hab = _inproj(x_prompt.reshape(BATCH * SEQ, D_MODEL), nw, w_main, w_ab, tabs_p)

    xa = jnp.concatenate([x_sample[:, 0, :], meta_tokens,
                          jnp.zeros((AUX_ROWS - DEC_BATCH - N_META, D_MODEL), F32)], axis=0)
    pos_a = jnp.concatenate([jnp.full((DEC_BATCH,), PAST_LEN), jnp.arange(N_META),
                             jnp.zeros((AUX_ROWS - DEC_BATCH - N_META,), jnp.int32)])
    h_aux = _aux_inproj(xa, nw, w, _rope_tables(pos_a))
    hs, hm = h_aux[0:DEC_BATCH], h_aux[DEC_BATCH:DEC_BATCH + N_META]

    pad_rows = CHUNK - N_META
    x_meta = jnp.pad(hm[:, O_QKVB:O_ZB], ((pad_rows, 0), (0, 0)))
    ab_meta = jnp.pad(hm[:, O_A:O_GA], ((pad_rows, 0), (0, LANES - 2 * B_HEADS)))
    abt_meta = ab_meta[:, 0:2 * B_HEADS].T.reshape(1, 1, 2 * B_HEADS, CHUNK)
    _, s_meta = _gdn_chunks(
        x_meta, ab_meta, abt_meta, jnp.zeros((CHUNK, B_HEADS * B_DV), F32), 0, cw, gvecs, gdn_norm_w,
        jnp.zeros((8, B_QKV), F32), jnp.zeros((B_HEADS, B_DK, B_DV), F32), 1, 1, pad_rows)
    k_meta, v_meta = hm[:, O_KA:O_VA], hm[:, O_VA:O_ZA]

    n_chunk = SEQ // CHUNK
    abt = hab[:, 0:2 * B_HEADS].reshape(BATCH, n_chunk, CHUNK, 2 * B_HEADS).transpose(0, 1, 3, 2)
    yb, ssm_p = _gdn_chunks(
        h_main, hab, abt, h_main, M_ZB // (B_HEADS * B_DV), cw, gvecs, gdn_norm_w,
        x_meta[CHUNK - 8:CHUNK], s_meta[0], BATCH, n_chunk, 0)
    ya = _attention(lams, h_main, k_meta, v_meta, subln_w)
    y_prompt = _merge(x_prompt.reshape(BATCH * SEQ, D_MODEL), ya, yb, h_main, h_main,
                      M_GA // D_MODEL, M_GB // D_MODEL, wpa, wpb, wo, fw, 256)

    def with_meta(rows_meta, rows_tok):
        meta = jnp.broadcast_to(rows_meta.reshape(1, N_META, A_HEADS, LANES),
                                (BATCH, N_META, A_HEADS, LANES))
        return jnp.concatenate([meta, rows_tok.reshape(BATCH, SEQ, A_HEADS, LANES)], axis=1)[None]

    k_rows_p = with_meta(k_meta, h_main[:, M_KA:M_VA])
    v_rows_p = with_meta(v_meta, h_main[:, M_VA:M_ZA])
    conv_p = h_main.reshape(BATCH, SEQ, N_MAIN)[:, SEQ - (CONV_W - 1):, M_QKVB:M_QKVB + B_QKV][None]

    heads = lambda t: t.reshape(DEC_BATCH, A_HEADS, LANES)
    q_s = heads(hs[:, O_QA:O_KA]) * (A_DH ** -0.5)
    k_s, v_s = heads(hs[:, O_KA:O_VA]), heads(hs[:, O_VA:O_ZA])
    ya_s = _decode_attention(page_table, lams, q_s, k_s, v_s, heads(hs[:, O_ZA:O_QKVB]), subln_w,
                             cache_k, cache_v)
    rows = B_QKV // LANES
    ab_s = jnp.pad(hs[:, O_A:O_GA], ((0, 0), (0, LANES - 2 * B_HEADS))).reshape(DEC_BATCH, 1, LANES)
    yb_s, ssm_s, conv_s = _gdn_step(
        state_conv[0].reshape(DEC_BATCH, CONV_W - 1, rows, LANES),
        hs[:, O_QKVB:O_ZB].reshape(DEC_BATCH, rows, LANES), cw.reshape(CONV_W, rows, LANES),
        ab_s, alog_l, dtb_l, heads(hs[:, O_ZB:O_A]), gdn_norm_w, state_ssm[0])
    y_sample = _merge(x_sample[:, 0, :], ya_s.reshape(DEC_BATCH, -1).astype(BF16),
                      yb_s.reshape(DEC_BATCH, -1).astype(BF16), hs[:, O_GA:O_GB], hs[:, O_GB:],
                      0, 0, wpa, wpb, wo, fw, DEC_BATCH)

    return (y_prompt.reshape(BATCH, SEQ, D_MODEL), y_sample.reshape(DEC_BATCH, 1, D_MODEL),
            k_rows_p, v_rows_p, conv_p, ssm_p[None],
            k_s.reshape(1, DEC_BATCH, 1, A_HEADS, LANES), v_s.reshape(1, DEC_BATCH, 1, A_HEADS, LANES),
            conv_s.reshape(1, DEC_BATCH, CONV_W - 1, B_QKV), ssm_s[None])
```

```python
import functools
import math

import jax
import jax.numpy as jnp
from jax import lax
from jax.experimental import pallas as pl
from jax.experimental.pallas import tpu as pltpu

F32 = jnp.float32
BF16 = jnp.bfloat16

D_MODEL = 2048
BATCH = 4
SEQ = 2048
DEC_BATCH = 32
PAST_LEN = 8192
PAGE_SIZE = 128
N_PAGES = PAST_LEN // PAGE_SIZE
N_META = 16
A_HEADS = 8
A_DH = 64
A_DV = 128
ROPE_DIM = 16
ROPE_THETA = 500000.0
B_HEADS = 8
B_DK = 128
B_DV = 128
B_QKV = 3072
CONV_W = 4
CHUNK = 64
EPS = 1e-6
NEG = -1e30
LAMBDA_INIT = 0.8 - 0.6 * math.exp(-0.3 * 0)

O_QA, O_KA, O_VA, O_ZA, O_QKVB, O_ZB, O_A, O_GA, O_GB, D_IN = (
    0, 1024, 2048, 3072, 4096, 7168, 8192, 8208, 10256, 12304)
LANES = 128
N_MAIN = (D_IN // LANES) * LANES
GATE_SHIFT = O_GA % LANES
VMEM_LIMIT = 56 * 1024 * 1024


def _dot(a, b):
    return jnp.dot(a, b, preferred_element_type=F32)


def _dot_nt(a, b):
    return lax.dot_general(a, b, (((1,), (1,)), ((), ())), preferred_element_type=F32)


def _dot_f32(a, b):
    return jnp.dot(a, b, preferred_element_type=F32, precision=lax.Precision.HIGHEST)


def _sigmoid(x):
    return 1.0 / (1.0 + jnp.exp(-x))


def _silu(x):
    return x * _sigmoid(x)


def _softplus(x):
    return jnp.maximum(x, 0.0) + jnp.log1p(jnp.exp(-jnp.abs(x)))


def _rmsnorm(x, w):
    return x * lax.rsqrt(jnp.mean(x * x, axis=-1, keepdims=True) + EPS) * w


def _rope_tables(pos):
    r = pos.shape[0]
    inv_freq = ROPE_THETA ** (-jnp.arange(0, ROPE_DIM, 2, dtype=F32) / ROPE_DIM)
    ang = pos.astype(F32)[:, None] * inv_freq[None, :]
    cos, sin = jnp.cos(ang), jnp.sin(ang)
    half = ROPE_DIM // 2
    rest = A_DH - ROPE_DIM
    c = jnp.concatenate([cos, cos, jnp.ones((r, rest), F32)], axis=1)
    sa = jnp.concatenate([jnp.zeros((r, half), F32), sin, jnp.zeros((r, rest), F32)], axis=1)
    sb = jnp.concatenate([-sin, jnp.zeros((r, half + rest), F32)], axis=1)
    return tuple(jnp.tile(t, (1, LANES // A_DH)) for t in (c, sa, sb))


def _rope_tile(t, c, sa, sb):
    out = []
    for i in range(t.shape[1] // LANES):
        x = t[:, i * LANES:(i + 1) * LANES]
        out.append(x * c + pltpu.roll(x, ROPE_DIM // 2, 1) * sa
                   + pltpu.roll(x, LANES - ROPE_DIM // 2, 1) * sb)
    return jnp.concatenate(out, axis=1) if len(out) > 1 else out[0]


IP_TM = 1024
IP_TN = 1024
IP_ROWS = 256


def _inproj_kernel(x_ref, nw_ref, w_ref, wt_ref, c_ref, sa_ref, sb_ref, h_ref, ht_ref, xn_ref):
    j = pl.program_id(1)

    @pl.when(j == 0)
    def _():
        def body(r, carry):
            rows = pl.ds(pl.multiple_of(r * IP_ROWS, IP_ROWS), IP_ROWS)
            xn_ref[rows, :] = _rmsnorm(x_ref[rows, :], nw_ref[...]).astype(BF16)
            return carry
        lax.fori_loop(0, IP_TM // IP_ROWS, body, 0)
        ht_ref[...] = _dot_nt(xn_ref[...], wt_ref[...])

    h_ref[...] = _dot_nt(xn_ref[...], w_ref[...])

    @pl.when(j < O_VA // IP_TN)
    def _():
        def body(r, carry):
            rows = pl.ds(pl.multiple_of(r * IP_ROWS, IP_ROWS), IP_ROWS)
            h_ref[rows, :] = _rope_tile(h_ref[rows, :], c_ref[rows, :], sa_ref[rows, :], sb_ref[rows, :])
            return carry
        lax.fori_loop(0, IP_TM // IP_ROWS, body, 0)


def _inproj(x, nw, w_bf, w_tail, tabs):
    m = x.shape[0]
    per_seq = SEQ // IP_TM
    tab_spec = pl.BlockSpec((IP_TM, LANES), lambda i, j: (i % per_seq, 0))
    return pl.pallas_call(
        _inproj_kernel,
        grid=(m // IP_TM, N_MAIN // IP_TN),
        in_specs=[
            pl.BlockSpec((IP_TM, D_MODEL), lambda i, j: (i, 0)),
            pl.BlockSpec((1, D_MODEL), lambda i, j: (0, 0)),
            pl.BlockSpec((IP_TN, D_MODEL), lambda i, j: (j, 0)),
            pl.BlockSpec((LANES, D_MODEL), lambda i, j: (0, 0)),
            tab_spec, tab_spec, tab_spec,
        ],
        out_specs=[
            pl.BlockSpec((IP_TM, IP_TN), lambda i, j: (i, j)),
            pl.BlockSpec((IP_TM, LANES), lambda i, j: (i, 0)),
        ],
        out_shape=[jax.ShapeDtypeStruct((m, N_MAIN), F32),
                   jax.ShapeDtypeStruct((m, LANES), F32)],
        scratch_shapes=[pltpu.VMEM((IP_TM, D_MODEL), BF16)],
        compiler_params=pltpu.CompilerParams(
            dimension_semantics=("arbitrary", "arbitrary"), vmem_limit_bytes=VMEM_LIMIT),
        name="inproj",
    )(x, nw, w_bf, w_tail, *tabs)


AUX_ROWS = 64
AUX_TN = 512


def _aux_inproj_kernel(x_ref, nw_ref, w_ref, c_ref, sa_ref, sb_ref, o_ref, wbf_ref, wtail_ref, xs_ref):
    j = pl.program_id(0)

    @pl.when(j == 0)
    def _():
        xn = _rmsnorm(x_ref[...], nw_ref[...])
        hi = xn.astype(BF16)
        xs_ref[0:AUX_ROWS, :] = hi
        xs_ref[AUX_ROWS:2 * AUX_ROWS, :] = (xn - hi.astype(F32)).astype(BF16)

    w = w_ref[...]
    w_hi = w.astype(BF16)
    wbf_ref[...] = w_hi

    @pl.when(j == pl.num_programs(0) - 1)
    def _():
        row = lax.broadcasted_iota(jnp.int32, (LANES, D_MODEL), 0)
        wtail_ref[...] = jnp.where(row < D_IN - N_MAIN, w_hi[0:LANES, :], jnp.zeros((), BF16))

    w_lo = (w - w_hi.astype(F32)).astype(BF16)
    r1 = _dot_nt(xs_ref[...], w_hi)
    r2 = _dot_nt(xs_ref[0:AUX_ROWS, :], w_lo)
    acc = r1[0:AUX_ROWS] + (r1[AUX_ROWS:] + r2)
    is_rope = j < O_VA // AUX_TN

    @pl.when(is_rope)
    def _():
        o_ref[...] = _rope_tile(acc, c_ref[...], sa_ref[...], sb_ref[...])

    @pl.when(jnp.logical_not(is_rope))
    def _():
        o_ref[...] = acc


def _aux_inproj(xa, nw, w, tabs):
    tab_spec = pl.BlockSpec((AUX_ROWS, LANES), lambda j: (0, 0))
    return pl.pallas_call(
        _aux_inproj_kernel,
        grid=(pl.cdiv(D_IN, AUX_TN),),
        in_specs=[
            pl.BlockSpec((AUX_ROWS, D_MODEL), lambda j: (0, 0)),
            pl.BlockSpec((1, D_MODEL), lambda j: (0, 0)),
            pl.BlockSpec((AUX_TN, D_MODEL), lambda j: (j, 0)),
            tab_spec, tab_spec, tab_spec,
        ],
        out_specs=[pl.BlockSpec((AUX_ROWS, AUX_TN), lambda j: (0, j)),
                   pl.BlockSpec((AUX_TN, D_MODEL), lambda j: (j, 0)),
                   pl.BlockSpec((LANES, D_MODEL), lambda j: (0, 0))],
        out_shape=[jax.ShapeDtypeStruct((AUX_ROWS, D_IN), F32),
                   jax.ShapeDtypeStruct((D_IN, D_MODEL), BF16),
                   jax.ShapeDtypeStruct((LANES, D_MODEL), BF16)],
        scratch_shapes=[pltpu.VMEM((2 * AUX_ROWS, D_MODEL), BF16)],
        compiler_params=pltpu.CompilerParams(
            dimension_semantics=("arbitrary",), vmem_limit_bytes=VMEM_LIMIT),
        name="aux_inproj",
    )(xa, nw, w, *tabs)


AT_TQ = 512
AT_TK = 512
LOG2E = math.log2(math.e)


def _diff_lambda(lq1, lk1, lq2, lk2):
    a = jnp.exp(jnp.sum(lq1 * lk1, axis=-1, keepdims=True))
    b = jnp.exp(jnp.sum(lq2 * lk2, axis=-1, keepdims=True))
    return a - b + LAMBDA_INIT


def _attn_kernel(lq1_ref, lk1_ref, lq2_ref, lk2_ref, q_ref, k_ref, v_ref, km_ref, vm_ref,
                 za_ref, sw_ref, o_ref, kb_ref, vb_ref, kmb_ref, vmb_ref, q2_ref, s_ref, sm_ref,
                 m_ref, l_ref, acc_ref):
    qi = pl.program_id(2)
    tq, tk = AT_TQ, AT_TK
    n_tiles = tk // LANES

    @pl.when(qi == 0)
    def _():
        kb_ref[...] = k_ref[...].astype(BF16)
        vb_ref[...] = v_ref[...].astype(BF16)
        pad = jnp.zeros((LANES - N_META, LANES), BF16)
        kmb_ref[...] = jnp.concatenate([km_ref[...].astype(BF16), pad], axis=0)
        vmb_ref[...] = jnp.concatenate([vm_ref[...].astype(BF16), pad], axis=0)

    q = q_ref[...] * (A_DH ** -0.5 * LOG2E)
    lane = lax.broadcasted_iota(jnp.int32, (tq, LANES), 1)
    q2_ref[0:tq, :] = jnp.where(lane < A_DH, q, 0.0).astype(BF16)
    q2_ref[tq:2 * tq, :] = jnp.where(lane >= A_DH, q, 0.0).astype(BF16)

    def tile_max(m, s):
        for c in range(s.shape[1] // LANES):
            m = jnp.maximum(m, s[:, c * LANES:(c + 1) * LANES])
        return m

    lane2 = lax.broadcasted_iota(jnp.int32, (2 * tq, LANES), 1)
    s = jnp.where(lane2 < N_META, _dot_nt(q2_ref[...], kmb_ref[...]), NEG)
    sm_ref[...] = s
    m_ref[...] = s

    def scores(j):
        rows = pl.ds(pl.multiple_of(j * tk, tk), tk)
        return _dot_nt(q2_ref[...], kb_ref[rows, :])

    def pass1(j, carry):
        s = scores(j)
        s_ref[j] = s
        m_ref[...] = tile_max(m_ref[...], s)
        return carry
    lax.fori_loop(0, qi, pass1, 0)

    r = lax.broadcasted_iota(jnp.int32, (2 * tq, tk), 0)
    r = jnp.where(r >= tq, r - tq, r)
    c = lax.broadcasted_iota(jnp.int32, (2 * tq, tk), 1)
    s = jnp.where(c <= r, scores(qi), NEG)
    s_ref[qi] = s
    m = jnp.max(tile_max(m_ref[...], s), axis=-1, keepdims=True)
    m_ref[...] = jnp.broadcast_to(m, (2 * tq, LANES))

    p = jnp.exp2(sm_ref[...] - m_ref[...])
    l_ref[...] = p
    acc_ref[...] = _dot(p.astype(BF16), vmb_ref[...])

    def pass2(j, carry):
        mb = m_ref[...]
        lsum = l_ref[...]
        ps = []
        for t in range(n_tiles):
            p = jnp.exp2(s_ref[j, :, t * LANES:(t + 1) * LANES] - mb)
            lsum = lsum + p
            ps.append(p.astype(BF16))
        l_ref[...] = lsum
        rows = pl.ds(pl.multiple_of(j * tk, tk), tk)
        acc_ref[...] += _dot(jnp.concatenate(ps, axis=1), vb_ref[rows, :])
        return carry
    lax.fori_loop(0, qi + 1, pass2, 0)

    lam = _diff_lambda(lq1_ref[...], lk1_ref[...], lq2_ref[...], lk2_ref[...])
    l = jnp.sum(l_ref[...], axis=-1, keepdims=True)
    o = acc_ref[0:tq, :] / l[0:tq] - lam * (acc_ref[tq:2 * tq, :] / l[tq:2 * tq])
    y = _rmsnorm(o, sw_ref[...]) * (1.0 - LAMBDA_INIT)
    o_ref[...] = (y * _silu(za_ref[...])).astype(BF16)


def _attention(lams, h_main, k_meta, v_meta, subln_w):
    nq = SEQ // AT_TQ
    lam_spec = pl.BlockSpec((1, A_DH), lambda b, h, i: (0, 0))
    return pl.pallas_call(
        _attn_kernel,
        grid=(BATCH, A_HEADS, nq),
        in_specs=[
            lam_spec, lam_spec, lam_spec, lam_spec,
            pl.BlockSpec((AT_TQ, LANES), lambda b, h, i: (b * nq + i, O_QA // LANES + h)),
            pl.BlockSpec((SEQ, LANES), lambda b, h, i: (b, O_KA // LANES + h)),
            pl.BlockSpec((SEQ, LANES), lambda b, h, i: (b, O_VA // LANES + h)),
            pl.BlockSpec((N_META, LANES), lambda b, h, i: (0, h)),
            pl.BlockSpec((N_META, LANES), lambda b, h, i: (0, h)),
            pl.BlockSpec((AT_TQ, LANES), lambda b, h, i: (b * nq + i, O_ZA // LANES + h)),
            pl.BlockSpec((1, A_DV), lambda b, h, i: (0, 0)),
        ],
        out_specs=pl.BlockSpec((AT_TQ, LANES), lambda b, h, i: (b * nq + i, h)),
        out_shape=jax.ShapeDtypeStruct((BATCH * SEQ, A_HEADS * A_DV), BF16),
        scratch_shapes=[
            pltpu.VMEM((SEQ, LANES), BF16), pltpu.VMEM((SEQ, LANES), BF16),
            pltpu.VMEM((LANES, LANES), BF16), pltpu.VMEM((LANES, LANES), BF16),
            pltpu.VMEM((2 * AT_TQ, LANES), BF16),
            pltpu.VMEM((SEQ // AT_TK, 2 * AT_TQ, AT_TK), F32),
            pltpu.VMEM((2 * AT_TQ, LANES), F32),
            pltpu.VMEM((2 * AT_TQ, LANES), F32), pltpu.VMEM((2 * AT_TQ, LANES), F32),
            pltpu.VMEM((2 * AT_TQ, A_DV), F32),
        ],
        compiler_params=pltpu.CompilerParams(
            dimension_semantics=("arbitrary", "arbitrary", "arbitrary"),
            vmem_limit_bytes=VMEM_LIMIT),
        name="diff_attn",
    )(*lams, h_main, h_main, h_main, k_meta, v_meta, h_main, subln_w)


def _unit_lower_inverse_minus_eye(a_list, i_idx, j_idx):
    base = 8
    diag = (i_idx // base) == (j_idx // base)
    b = [jnp.where(diag, a, 0.0) for a in a_list]
    n = [-x for x in b]
    for _ in range(2):
        bb = [x.astype(BF16) for x in b]
        b = [_dot(x, x) for x in bb]
        nb = [_dot(x.astype(BF16), y.astype(BF16)) for x, y in zip(n, b)]
        n = [x + y + z for x, y, z in zip(n, b, nb)]
    s = base
    while s < CHUNK:
        join = ((i_idx // (2 * s)) == (j_idx // (2 * s))) & ((i_idx // s) % 2 == 1) & ((j_idx // s) % 2 == 0)
        a_s = [jnp.where(join, a, 0.0) for a in a_list]
        x = [p + _dot(q.astype(BF16), p.astype(BF16)) for p, q in zip(a_s, n)]
        xn = [_dot(p.astype(BF16), q.astype(BF16)) for p, q in zip(x, n)]
        n = [q - (p + r) for q, p, r in zip(n, x, xn)]
        s *= 2
    return n


def _gdn_kernel(xq_ref, xk_ref, xv_ref, ab_ref, abt_ref, zb_ref, cw_ref, alog_ref, dtb_ref,
                alogc_ref, dtbc_ref, gw_ref, halo_ref, s0_ref, yb_ref, sfin_ref, ext_ref, s_ref,
                *, masked_rows):
    c = pl.program_id(1)
    hist = 8

    @pl.when(c == 0)
    def _():
        ext_ref[0:hist, :] = halo_ref[...]
        s_ref[...] = s0_ref[...]

    x = jnp.concatenate([xq_ref[...], xk_ref[...], xv_ref[...]], axis=1)
    ext_ref[hist:hist + CHUNK, :] = x
    y = cw_ref[CONV_W - 1:CONV_W, :] * x
    for t in range(CONV_W - 1):
        lo = hist - (CONV_W - 1) + t
        y = y + cw_ref[t:t + 1, :] * ext_ref[lo:lo + CHUNK, :]
    y = _silu(y)
    ext_ref[0:hist, :] = x[CHUNK - hist:CHUNK, :]

    i_idx = lax.broadcasted_iota(jnp.int32, (CHUNK, CHUNK), 0)
    j_idx = lax.broadcasted_iota(jnp.int32, (CHUNK, CHUNK), 1)
    tril = i_idx >= j_idx
    strict = i_idx > j_idx
    tril_f = tril.astype(F32)
    triu_f = (i_idx <= j_idx).astype(F32)

    ab = ab_ref[...]
    g_c = -jnp.exp(alog_ref[...]) * _softplus(ab + dtb_ref[...])
    beta_c = _sigmoid(ab)
    abt = abt_ref[...]
    g_r = -jnp.exp(alogc_ref[...]) * _softplus(abt + dtbc_ref[...])
    if masked_rows:
        row_ok = lax.broadcasted_iota(jnp.int32, (CHUNK, LANES), 0) >= masked_rows
        col_ok = lax.broadcasted_iota(jnp.int32, (2 * B_HEADS, CHUNK), 1) >= masked_rows
        g_c = jnp.where(row_ok, g_c, 0.0)
        beta_c = jnp.where(row_ok, beta_c, 0.0)
        g_r = jnp.where(col_ok, g_r, 0.0)
    gc_c = _dot_f32(tril_f, g_c)
    gc_r = _dot_f32(g_r, triu_f)

    nk = B_HEADS * B_DK
    heads = range(B_HEADS)

    def l2n(t):
        return t * lax.rsqrt(jnp.sum(t * t, axis=-1, keepdims=True) + EPS)

    qn = [l2n(y[:, h * B_DK:(h + 1) * B_DK]) * (B_DK ** -0.5) for h in heads]
    kn = [l2n(y[:, nk + h * B_DK:nk + (h + 1) * B_DK]) for h in heads]
    vh = [y[:, 2 * nk + h * B_DV:2 * nk + (h + 1) * B_DV] for h in heads]
    bcol = [beta_c[:, B_HEADS + h:B_HEADS + h + 1] for h in heads]
    gcc = [gc_c[:, h:h + 1] for h in heads]
    decay = [jnp.where(tril, jnp.exp(jnp.where(tril, gcc[h] - gc_r[h:h + 1, :], 0.0)), 0.0)
             for h in heads]
    kbeta = [kn[h] * bcol[h] for h in heads]
    kn_b = [t.astype(BF16) for t in kn]
    kk = [_dot_nt(kbeta[h].astype(BF16), kn_b[h]) for h in heads]
    qk = [_dot_nt(qn[h].astype(BF16), kn_b[h]) for h in heads]
    a = [jnp.where(strict, kk[h] * decay[h], 0.0) for h in heads]
    n = _unit_lower_inverse_minus_eye(a, i_idx, j_idx)
    egc = [jnp.exp(t) for t in gcc]
    rhs = [jnp.concatenate([vh[h] * bcol[h], kbeta[h] * egc[h]], axis=1) for h in heads]
    nr = [_dot(n[h].astype(BF16), rhs[h].astype(BF16)) for h in heads]
    sol = [rhs[h] + nr[h] for h in heads]
    st = [s_ref[h] for h in heads]
    st_b = [t.astype(BF16) for t in st]
    ws = [_dot(sol[h][:, B_DV:B_DV + B_DK].astype(BF16), st_b[h]) for h in heads]
    qs = [_dot((qn[h] * egc[h]).astype(BF16), st_b[h]) for h in heads]
    v_new_b = [(sol[h][:, 0:B_DV] - ws[h]).astype(BF16) for h in heads]
    av = [_dot((qk[h] * decay[h]).astype(BF16), v_new_b[h]) for h in heads]
    g_last = [t[CHUNK - 1:CHUNK, :] for t in gcc]
    ke_t = [(kn[h] * jnp.exp(g_last[h] - gcc[h])).T.astype(BF16) for h in heads]
    kv = [_dot(ke_t[h], v_new_b[h]) for h in heads]
    for h in heads:
        s_ref[h] = st[h] * jnp.exp(g_last[h]) + kv[h]
    for h in heads:
        zb = zb_ref[:, h * B_DV:(h + 1) * B_DV]
        yb_ref[:, h * B_DV:(h + 1) * B_DV] = (_rmsnorm(qs[h] + av[h], gw_ref[...]) * _silu(zb)).astype(BF16)

    sfin_ref[...] = s_ref[...]


def _gdn_chunks(x, x_col, ab, ab_col, abt, zsrc, zb_col, cw, gvecs, gw, halo, s0, n_seq, n_chunk,
                masked_rows):
    alog_l, dtb_l, alog_c, dtb_c = gvecs
    const2 = lambda b, c: (0, 0)
    kern = functools.partial(_gdn_kernel, masked_rows=masked_rows)
    part = B_QKV // 3

    def x_spec(k):
        return pl.BlockSpec((CHUNK, part), lambda b, c: (b * n_chunk + c, x_col + k))

    return pl.pallas_call(
        kern,
        grid=(n_seq, n_chunk),
        in_specs=[
            x_spec(0), x_spec(1), x_spec(2),
            pl.BlockSpec((CHUNK, LANES), lambda b, c: (b * n_chunk + c, ab_col)),
            pl.BlockSpec((None, None, 2 * B_HEADS, CHUNK), lambda b, c: (b, c, 0, 0)),
            pl.BlockSpec((CHUNK, B_HEADS * B_DV), lambda b, c: (b * n_chunk + c, zb_col)),
            pl.BlockSpec((CONV_W, B_QKV), const2),
            pl.BlockSpec((1, LANES), const2), pl.BlockSpec((1, LANES), const2),
            pl.BlockSpec((2 * B_HEADS, 1), const2), pl.BlockSpec((2 * B_HEADS, 1), const2),
            pl.BlockSpec((1, B_DV), const2),
            pl.BlockSpec((8, B_QKV), const2),
            pl.BlockSpec((B_HEADS, B_DK, B_DV), lambda b, c: (0, 0, 0)),
        ],
        out_specs=[
            pl.BlockSpec((CHUNK, B_HEADS * B_DV), lambda b, c: (b * n_chunk + c, 0)),
            pl.BlockSpec((None, B_HEADS, B_DK, B_DV), lambda b, c: (b, 0, 0, 0)),
        ],
        out_shape=[jax.ShapeDtypeStruct((n_seq * n_chunk * CHUNK, B_HEADS * B_DV), BF16),
                   jax.ShapeDtypeStruct((n_seq, B_HEADS, B_DK, B_DV), F32)],
        scratch_shapes=[pltpu.VMEM((8 + CHUNK, B_QKV), F32),
                        pltpu.VMEM((B_HEADS, B_DK, B_DV), F32)],
        compiler_params=pltpu.CompilerParams(
            dimension_semantics=("arbitrary", "arbitrary"), vmem_limit_bytes=VMEM_LIMIT),
        name="gdn_chunks",
    )(x, x, x, ab, abt, zsrc, cw, alog_l, dtb_l, alog_c, dtb_c, gw, halo, s0)


def _shift_lanes(x, k):
    nblk = x.shape[1] // LANES
    r = [pltpu.roll(x[:, c * LANES:(c + 1) * LANES], LANES - k, 1) for c in range(nblk)]
    lane = lax.broadcasted_iota(jnp.int32, (x.shape[0], LANES), 1)
    return jnp.concatenate([jnp.where(lane < LANES - k, r[c], r[c + 1]) for c in range(nblk - 1)],
                           axis=1)


def _merge_kernel(x_ref, ya_ref, yb_ref, g4_ref, g5_ref, gt_ref, wpa_ref, wpb_ref, wo_ref, fw_ref,
                  y_ref):
    g5 = g5_ref[...]
    ga = _shift_lanes(jnp.concatenate([g4_ref[...], g5[:, 0:LANES]], axis=1), GATE_SHIFT)
    gb = _shift_lanes(jnp.concatenate([g5, gt_ref[...]], axis=1), GATE_SHIFT)
    pa = _dot(ya_ref[...], wpa_ref[...])
    pb = _dot(yb_ref[...], wpb_ref[...])
    mixed = _sigmoid(ga) * pa + _sigmoid(gb) * pb
    hp = x_ref[...] + _dot(mixed.astype(BF16), wo_ref[...])
    y_ref[...] = _rmsnorm(hp, fw_ref[...])


def _merge(x, ya, yb, hsrc, htail, wpa, wpb, wo, fw, tm):
    m = x.shape[0]
    once = pl.Buffered(1)
    return pl.pallas_call(
        _merge_kernel,
        grid=(m // tm,),
        in_specs=[
            pl.BlockSpec((tm, D_MODEL), lambda i: (i, 0)),
            pl.BlockSpec((tm, A_HEADS * A_DV), lambda i: (i, 0)),
            pl.BlockSpec((tm, B_HEADS * B_DV), lambda i: (i, 0)),
            pl.BlockSpec((tm, D_MODEL), lambda i: (i, O_A // D_MODEL)),
            pl.BlockSpec((tm, D_MODEL), lambda i: (i, O_A // D_MODEL + 1)),
            pl.BlockSpec((tm, LANES), lambda i: (i, 0)),
            pl.BlockSpec((A_HEADS * A_DV, D_MODEL), lambda i: (0, 0), pipeline_mode=once),
            pl.BlockSpec((B_HEADS * B_DV, D_MODEL), lambda i: (0, 0), pipeline_mode=once),
            pl.BlockSpec((D_MODEL, D_MODEL), lambda i: (0, 0), pipeline_mode=once),
            pl.BlockSpec((1, D_MODEL), lambda i: (0, 0)),
        ],
        out_specs=pl.BlockSpec((tm, D_MODEL), lambda i: (i, 0)),
        out_shape=jax.ShapeDtypeStruct((m, D_MODEL), F32),
        compiler_params=pltpu.CompilerParams(
            dimension_semantics=("arbitrary",), vmem_limit_bytes=VMEM_LIMIT),
        name="merge",
    )(x, ya, yb, hsrc, hsrc, htail, wpa, wpb, wo, fw)


DA_NBUF = 8
DA_GRP = 4
DA_G = N_PAGES // DA_NBUF
TOK_TILES = PAGE_SIZE * A_HEADS // LANES


def _lane_group_reduce(x, op):
    s = A_HEADS
    while s < LANES:
        x = op(x, pltpu.roll(x, s, 1))
        s *= 2
    return x


def _decode_attn_kernel(pt_ref, lq1_ref, lk1_ref, lq2_ref, lk2_ref, q_ref, kn_ref, vn_ref,
                        za_ref, sw_ref, ck_ref, cv_ref, o_ref, buf_ref, sem_ref, sc_ref):
    b = pl.program_id(0)
    n_seq = pl.num_programs(0)

    def page_copy(src_ref, seq, page_idx, slot):
        return pltpu.make_async_copy(src_ref.at[0, pt_ref[seq, page_idx]], buf_ref.at[slot],
                                     sem_ref.at[slot])

    def wait_slot(slot):
        page_copy(ck_ref, b, 0, slot).wait()

    @pl.when(b == 0)
    def _():
        for s in range(DA_NBUF):
            page_copy(ck_ref, 0, s, s).start(priority=s % 2)

    sub = lax.broadcasted_iota(jnp.int32, (A_HEADS, LANES), 0)
    lane = lax.broadcasted_iota(jnp.int32, (A_HEADS, LANES), 1)
    head_of_lane = lane % A_HEADS
    hmask = head_of_lane == sub
    half_sel = jnp.where((sub == 0) & (lane < A_DH), 1.0,
                         jnp.where((sub == 1) & (lane >= A_DH), 1.0, 0.0)).astype(BF16)

    q = q_ref[...]

    def k_turn(g, carry):
        for s0 in range(0, DA_NBUF, DA_GRP):
            slots = range(s0, s0 + DA_GRP)
            for s in slots:
                wait_slot(s)
            prods = [(buf_ref[s] * q[None]).reshape(PAGE_SIZE * A_HEADS, LANES).astype(BF16)
                     for s in slots]
            scs = [_dot_nt(half_sel, p) for p in prods]
            for s, sc in zip(slots, scs):
                base = (g * DA_NBUF + s) * TOK_TILES
                for c in range(TOK_TILES):
                    sc_ref[base + c] = sc[:, c * LANES:(c + 1) * LANES]

            @pl.when(g < DA_G - 1)
            def _():
                for s in slots:
                    page_copy(ck_ref, b, (g + 1) * DA_NBUF + s, s).start(priority=s % 2)

            @pl.when(g == DA_G - 1)
            def _():
                for s in slots:
                    page_copy(cv_ref, b, s, s).start(priority=s % 2)
        return carry
    lax.fori_loop(0, DA_G, k_turn, 0)

    lam = _diff_lambda(lq1_ref[...], lk1_ref[...], lq2_ref[...], lk2_ref[...])
    prod = q * kn_ref[...]
    hs1 = jnp.sum(jnp.where(lane < A_DH, prod, 0.0), axis=1, keepdims=True)
    hs2 = jnp.sum(jnp.where(lane >= A_DH, prod, 0.0), axis=1, keepdims=True)
    row1 = jnp.sum(jnp.where(hmask, hs1, 0.0), axis=0, keepdims=True)
    row2 = jnp.sum(jnp.where(hmask, hs2, 0.0), axis=0, keepdims=True)
    s_new = jnp.where(sub == 0, row1, jnp.where(sub == 1, row2, 0.0))
    sc = sc_ref[...]
    mx = _lane_group_reduce(jnp.max(sc, axis=0), jnp.maximum)
    mx = jnp.maximum(mx, s_new)
    p = jnp.exp(sc - mx[None])
    p_new = jnp.exp(s_new - mx)
    den = _lane_group_reduce(jnp.sum(p, axis=0), jnp.add) + p_new
    coef = jnp.where(sub == 0, 1.0 / den, jnp.where(sub == 1, -lam / den, 0.0))
    sc_ref[...] = p * coef[None]
    wn = jnp.sum(p_new * coef, axis=0, keepdims=True)

    def page_weights(page):
        tiles = []
        for c in range(TOK_TILES):
            r = jnp.sum(sc_ref[page * TOK_TILES + c], axis=0, keepdims=True)
            tiles.append(jnp.where(hmask, r, 0.0))
        w = jnp.concatenate(tiles, axis=1)
        w_hi = w.astype(BF16)
        w_lo = (w - w_hi.astype(F32)).astype(BF16)
        return jnp.concatenate([w_hi, w_lo], axis=0)

    def v_turn(g, acc):
        for s0 in range(0, DA_NBUF, DA_GRP):
            slots = range(s0, s0 + DA_GRP)
            ws = [page_weights(g * DA_NBUF + s) for s in slots]
            for s in slots:
                wait_slot(s)
            vs = [buf_ref[s].reshape(PAGE_SIZE * A_HEADS, LANES).astype(BF16) for s in slots]
            rs = [_dot(w, v) for w, v in zip(ws, vs)]
            for r in rs:
                acc = acc + (r[0:A_HEADS] + r[A_HEADS:])

            @pl.when(g < DA_G - 1)
            def _():
                for s in slots:
                    page_copy(cv_ref, b, (g + 1) * DA_NBUF + s, s).start(priority=s % 2)

            @pl.when((g == DA_G - 1) & (b < n_seq - 1))
            def _():
                for s in slots:
                    page_copy(ck_ref, b + 1, s, s).start(priority=s % 2)
        return acc
    acc = lax.fori_loop(0, DA_G, v_turn, jnp.zeros((A_HEADS, A_DV), F32))

    w_new = jnp.sum(jnp.where(hmask, wn, 0.0), axis=1, keepdims=True) * (A_HEADS / LANES)
    o = acc + w_new * vn_ref[...]
    y = _rmsnorm(o, sw_ref[...]) * (1.0 - LAMBDA_INIT)
    o_ref[...] = y * _silu(za_ref[...])


def _decode_attention(page_table, lams, q_s, k_new, v_new, za_s, subln_w, cache_k, cache_v):
    lam_spec = pl.BlockSpec((1, A_DH), lambda b, pt: (0, 0))
    row_spec = pl.BlockSpec((None, A_HEADS, LANES), lambda b, pt: (b, 0, 0))
    hbm_spec = pl.BlockSpec(memory_space=pl.ANY)
    grid_spec = pltpu.PrefetchScalarGridSpec(
        num_scalar_prefetch=1,
        grid=(DEC_BATCH,),
        in_specs=[lam_spec, lam_spec, lam_spec, lam_spec, row_spec, row_spec, row_spec, row_spec,
                  pl.BlockSpec((1, A_DV), lambda b, pt: (0, 0)), hbm_spec, hbm_spec],
        out_specs=row_spec,
        scratch_shapes=[pltpu.VMEM((DA_NBUF, PAGE_SIZE, A_HEADS, LANES), F32),
                        pltpu.SemaphoreType.DMA((DA_NBUF,)),
                        pltpu.VMEM((N_PAGES * TOK_TILES, A_HEADS, LANES), F32)],
    )
    return pl.pallas_call(
        _decode_attn_kernel,
        grid_spec=grid_spec,
        out_shape=jax.ShapeDtypeStruct((DEC_BATCH, A_HEADS, A_DV), F32),
        compiler_params=pltpu.CompilerParams(
            dimension_semantics=("arbitrary",), vmem_limit_bytes=VMEM_LIMIT),
        name="decode_attn",
    )(page_table, *lams, q_s, k_new, v_new, za_s, subln_w, cache_k, cache_v)


def _gdn_step_kernel(sc_ref, x_ref, cw_ref, ab_ref, alog_ref, dtb_ref, zb_ref, gw_ref, s_ref,
                     yb_ref, sout_ref, cout_ref):
    x = x_ref[...]
    y = cw_ref[CONV_W - 1] * x
    for t in range(CONV_W - 1):
        y = y + cw_ref[t] * sc_ref[t]
    y = _silu(y)
    for t in range(CONV_W - 2):
        cout_ref[t] = sc_ref[t + 1]
    cout_ref[CONV_W - 2] = x

    q = y[0:B_HEADS]
    k = y[B_HEADS:2 * B_HEADS]
    v = y[2 * B_HEADS:3 * B_HEADS]
    qn = q * lax.rsqrt(jnp.sum(q * q, axis=-1, keepdims=True) + EPS) * (B_DK ** -0.5)
    kn = k * lax.rsqrt(jnp.sum(k * k, axis=-1, keepdims=True) + EPS)
    qt = qn.T
    kt = kn.T
    ab = ab_ref[...]
    g = -jnp.exp(alog_ref[...]) * _softplus(ab + dtb_ref[...])
    beta = _sigmoid(ab)
    for h in range(B_HEADS):
        kcol = kt[:, h:h + 1]
        qcol = qt[:, h:h + 1]
        st = s_ref[h] * jnp.exp(g[:, h:h + 1])
        kv = jnp.sum(st * kcol, axis=0, keepdims=True)
        d = (v[h:h + 1] - kv) * beta[:, B_HEADS + h:B_HEADS + h + 1]
        st = st + kcol * d
        sout_ref[h] = st
        o = jnp.sum(st * qcol, axis=0, keepdims=True)
        yb_ref[h:h + 1, :] = _rmsnorm(o, gw_ref[...]) * _silu(zb_ref[h:h + 1, :])


def _gdn_step(state_conv, x, cw, ab, alog_l, dtb_l, zb, gw, state_ssm):
    rows = B_QKV // LANES
    seq3 = lambda b: (b, 0, 0)
    seq4 = lambda b: (b, 0, 0, 0)
    const2 = lambda b: (0, 0)
    return pl.pallas_call(
        _gdn_step_kernel,
        grid=(DEC_BATCH,),
        in_specs=[
            pl.BlockSpec((None, CONV_W - 1, rows, LANES), seq4),
            pl.BlockSpec((None, rows, LANES), seq3),
            pl.BlockSpec((CONV_W, rows, LANES), lambda b: (0, 0, 0)),
            pl.BlockSpec((None, 1, LANES), seq3),
            pl.BlockSpec((1, LANES), const2), pl.BlockSpec((1, LANES), const2),
            pl.BlockSpec((None, B_HEADS, B_DV), seq3),
            pl.BlockSpec((1, B_DV), const2),
            pl.BlockSpec((None, B_HEADS, B_DK, B_DV), seq4),
        ],
        out_specs=[
            pl.BlockSpec((None, B_HEADS, B_DV), seq3),
            pl.BlockSpec((None, B_HEADS, B_DK, B_DV), seq4),
            pl.BlockSpec((None, CONV_W - 1, rows, LANES), seq4),
        ],
        out_shape=[jax.ShapeDtypeStruct((DEC_BATCH, B_HEADS, B_DV), F32),
                   jax.ShapeDtypeStruct((DEC_BATCH, B_HEADS, B_DK, B_DV), F32),
                   jax.ShapeDtypeStruct((DEC_BATCH, CONV_W - 1, rows, LANES), F32)],
        compiler_params=pltpu.CompilerParams(
            dimension_semantics=("arbitrary",), vmem_limit_bytes=VMEM_LIMIT),
        name="gdn_step",
    )(state_conv, x, cw, ab, alog_l, dtb_l, zb, gw, state_ssm)


def kernel(x_prompt, x_sample, cache_k, cache_v, state_conv, state_ssm, page_table, meta_tokens,
           norm_w, w_in, lambda_q1, lambda_k1, lambda_q2, lambda_k2, subln_w, conv_w, a_log,
           dt_bias, gdn_norm_w, w_pa, w_pb, w_o, final_norm_w):
    assert x_prompt.shape == (BATCH, SEQ, D_MODEL) and x_sample.shape == (DEC_BATCH, 1, D_MODEL)
    assert w_in.shape == (1, D_MODEL, D_IN) and page_table.shape == (DEC_BATCH, N_PAGES)
    w = w_in[0].T
    nw = norm_w
    lams = (lambda_q1, lambda_k1, lambda_q2, lambda_k2)
    fw = final_norm_w.reshape(1, D_MODEL)
    wpa, wpb, wo = w_pa[0].astype(BF16), w_pb[0].astype(BF16), w_o[0].astype(BF16)
    cw = conv_w[0]

    def lanes8(v, off):
        return jnp.zeros((1, LANES), F32).at[0, off:off + B_HEADS].set(v)

    alog_l, dtb_l = lanes8(a_log[0], 0), lanes8(dt_bias[0], 0)
    alog_c, dtb_c = alog_l[0, 0:2 * B_HEADS].reshape(-1, 1), dtb_l[0, 0:2 * B_HEADS].reshape(-1, 1)
    gvecs = (alog_l, dtb_l, alog_c, dtb_c)

    xa = jnp.concatenate([x_sample[:, 0, :], meta_tokens,
                          jnp.zeros((AUX_ROWS - DEC_BATCH - N_META, D_MODEL), F32)], axis=0)
    pos_a = jnp.concatenate([jnp.full((DEC_BATCH,), PAST_LEN), jnp.arange(N_META),
                             jnp.zeros((AUX_ROWS - DEC_BATCH - N_META,), jnp.int32)])
    h_aux, w_bf, w_tail = _aux_inproj(xa, nw, w, _rope_tables(pos_a))
    hs, hm = h_aux[0:DEC_BATCH], h_aux[DEC_BATCH:DEC_BATCH + N_META]

    tabs_p = _rope_tables(N_META + jnp.arange(SEQ))
    h_main, h_tail = _inproj(x_prompt.reshape(BATCH * SEQ, D_MODEL), nw, w_bf, w_tail, tabs_p)

    pad_rows = CHUNK - N_META
    x_meta = jnp.pad(hm[:, O_QKVB:O_ZB], ((pad_rows, 0), (0, 0)))
    ab_meta = jnp.pad(hm[:, O_A:O_GA], ((pad_rows, 0), (0, LANES - 2 * B_HEADS)))
    abt_meta = ab_meta[:, 0:2 * B_HEADS].T.reshape(1, 1, 2 * B_HEADS, CHUNK)
    _, s_meta = _gdn_chunks(
        x_meta, 0, ab_meta, 0, abt_meta, jnp.zeros((CHUNK, B_HEADS * B_DV), F32), 0, cw, gvecs,
        gdn_norm_w, jnp.zeros((8, B_QKV), F32), jnp.zeros((B_HEADS, B_DK, B_DV), F32), 1, 1, pad_rows)
    k_meta, v_meta = hm[:, O_KA:O_VA], hm[:, O_VA:O_ZA]

    n_chunk = SEQ // CHUNK
    part = B_QKV // 3
    abt = h_main[:, O_A:O_GA].reshape(BATCH, n_chunk, CHUNK, 2 * B_HEADS).transpose(0, 1, 3, 2)
    yb, ssm_p = _gdn_chunks(
        h_main, O_QKVB // part, h_main, O_A // LANES, abt, h_main, O_ZB // part, cw, gvecs,
        gdn_norm_w, x_meta[CHUNK - 8:CHUNK], s_meta[0], BATCH, n_chunk, 0)
    ya = _attention(lams, h_main, k_meta, v_meta, subln_w)
    y_prompt = _merge(x_prompt.reshape(BATCH * SEQ, D_MODEL), ya, yb, h_main, h_tail,
                      wpa, wpb, wo, fw, 256)

    def with_meta(rows_meta, rows_tok):
        meta = jnp.broadcast_to(rows_meta.reshape(1, N_META, A_HEADS, LANES),
                                (BATCH, N_META, A_HEADS, LANES))
        return jnp.concatenate([meta, rows_tok.reshape(BATCH, SEQ, A_HEADS, LANES)], axis=1)[None]

    k_rows_p = with_meta(k_meta, h_main[:, O_KA:O_VA])
    v_rows_p = with_meta(v_meta, h_main[:, O_VA:O_ZA])
    conv_p = h_main.reshape(BATCH, SEQ, N_MAIN)[:, SEQ - (CONV_W - 1):, O_QKVB:O_ZB][None]

    heads = lambda t: t.reshape(DEC_BATCH, A_HEADS, LANES)
    q_s = heads(hs[:, O_QA:O_KA]) * (A_DH ** -0.5)
    k_s, v_s = heads(hs[:, O_KA:O_VA]), heads(hs[:, O_VA:O_ZA])
    ya_s = _decode_attention(page_table, lams, q_s, k_s, v_s, heads(hs[:, O_ZA:O_QKVB]), subln_w,
                             cache_k, cache_v)
    rows = B_QKV // LANES
    ab_s = jnp.pad(hs[:, O_A:O_GA], ((0, 0), (0, LANES - 2 * B_HEADS))).reshape(DEC_BATCH, 1, LANES)
    yb_s, ssm_s, conv_s = _gdn_step(
        state_conv[0].reshape(DEC_BATCH, CONV_W - 1, rows, LANES),
        hs[:, O_QKVB:O_ZB].reshape(DEC_BATCH, rows, LANES), cw.reshape(CONV_W, rows, LANES),
        ab_s, alog_l, dtb_l, heads(hs[:, O_ZB:O_A]), gdn_norm_w, state_ssm[0])
    hs_tail = jnp.pad(hs[:, N_MAIN:], ((0, 0), (0, LANES - (D_IN - N_MAIN))))
    y_sample = _merge(x_sample[:, 0, :], ya_s.reshape(DEC_BATCH, -1).astype(BF16),
                      yb_s.reshape(DEC_BATCH, -1).astype(BF16), hs, hs_tail,
                      wpa, wpb, wo, fw, DEC_BATCH)

    return (y_prompt.reshape(BATCH, SEQ, D_MODEL), y_sample.reshape(DEC_BATCH, 1, D_MODEL),
            k_rows_p, v_rows_p, conv_p, ssm_p[None],
            k_s.reshape(1, DEC_BATCH, 1, A_HEADS, LANES), v_s.reshape(1, DEC_BATCH, 1, A_HEADS, LANES),
            conv_s.reshape(1, DEC_BATCH, CONV_W - 1, B_QKV), ssm_s[None])
```

```python
import functools
import math

import jax
import jax.numpy as jnp
from jax import lax
from jax.experimental import pallas as pl
from jax.experimental.pallas import tpu as pltpu

F32 = jnp.float32
BF16 = jnp.bfloat16

D_MODEL = 2048
BATCH = 4
SEQ = 2048
DEC_BATCH = 32
PAST_LEN = 8192
PAGE_SIZE = 128
N_PAGES = PAST_LEN // PAGE_SIZE
N_META = 16
A_HEADS = 8
A_DH = 64
A_DV = 128
ROPE_DIM = 16
ROPE_THETA = 500000.0
B_HEADS = 8
B_DK = 128
B_DV = 128
B_QKV = 3072
CONV_W = 4
CHUNK = 64
EPS = 1e-6
NEG = -1e30
LAMBDA_INIT = 0.8 - 0.6 * math.exp(-0.3 * 0)

O_QA, O_KA, O_VA, O_ZA, O_QKVB, O_ZB, O_A, O_GA, O_GB, D_IN = (
    0, 1024, 2048, 3072, 4096, 7168, 8192, 8208, 10256, 12304)
LANES = 128
N_MAIN = (D_IN // LANES) * LANES
GATE_SHIFT = O_GA % LANES
VMEM_LIMIT = 56 * 1024 * 1024


def _dot(a, b):
    return jnp.dot(a, b, preferred_element_type=F32)


def _dot_nt(a, b):
    return lax.dot_general(a, b, (((1,), (1,)), ((), ())), preferred_element_type=F32)


def _dot_f32(a, b):
    return jnp.dot(a, b, preferred_element_type=F32, precision=lax.Precision.HIGHEST)


def _sigmoid(x):
    return 1.0 / (1.0 + jnp.exp(-x))


def _silu(x):
    return x * _sigmoid(x)


def _softplus(x):
    return jnp.maximum(x, 0.0) + jnp.log1p(jnp.exp(-jnp.abs(x)))


def _rmsnorm(x, w):
    return x * lax.rsqrt(jnp.mean(x * x, axis=-1, keepdims=True) + EPS) * w


def _rope_tables(pos):
    r = pos.shape[0]
    inv_freq = ROPE_THETA ** (-jnp.arange(0, ROPE_DIM, 2, dtype=F32) / ROPE_DIM)
    ang = pos.astype(F32)[:, None] * inv_freq[None, :]
    cos, sin = jnp.cos(ang), jnp.sin(ang)
    half = ROPE_DIM // 2
    rest = A_DH - ROPE_DIM
    c = jnp.concatenate([cos, cos, jnp.ones((r, rest), F32)], axis=1)
    sa = jnp.concatenate([jnp.zeros((r, half), F32), sin, jnp.zeros((r, rest), F32)], axis=1)
    sb = jnp.concatenate([-sin, jnp.zeros((r, half + rest), F32)], axis=1)
    return tuple(jnp.tile(t, (1, LANES // A_DH)) for t in (c, sa, sb))


def _rope_tile(t, c, sa, sb):
    out = []
    for i in range(t.shape[1] // LANES):
        x = t[:, i * LANES:(i + 1) * LANES]
        out.append(x * c + pltpu.roll(x, ROPE_DIM // 2, 1) * sa
                   + pltpu.roll(x, LANES - ROPE_DIM // 2, 1) * sb)
    return jnp.concatenate(out, axis=1) if len(out) > 1 else out[0]


IP_TM = 1024
IP_TN = 1024
IP_ROWS = 256


def _inproj_kernel(x_ref, nw_ref, w_ref, wt_ref, c_ref, sa_ref, sb_ref, h_ref, ht_ref, xn_ref):
    j = pl.program_id(1)

    @pl.when(j == 0)
    def _():
        def body(r, carry):
            rows = pl.ds(pl.multiple_of(r * IP_ROWS, IP_ROWS), IP_ROWS)
            xn_ref[rows, :] = _rmsnorm(x_ref[rows, :], nw_ref[...]).astype(BF16)
            return carry
        lax.fori_loop(0, IP_TM // IP_ROWS, body, 0)
        ht_ref[...] = _dot_nt(xn_ref[...], wt_ref[...])

    h_ref[...] = _dot_nt(xn_ref[...], w_ref[...])

    @pl.when(j < O_VA // IP_TN)
    def _():
        def body(r, carry):
            rows = pl.ds(pl.multiple_of(r * IP_ROWS, IP_ROWS), IP_ROWS)
            h_ref[rows, :] = _rope_tile(h_ref[rows, :], c_ref[rows, :], sa_ref[rows, :], sb_ref[rows, :])
            return carry
        lax.fori_loop(0, IP_TM // IP_ROWS, body, 0)


def _inproj(x, nw, w_bf, w_tail, tabs):
    m = x.shape[0]
    per_seq = SEQ // IP_TM
    tab_spec = pl.BlockSpec((IP_TM, LANES), lambda i, j: (i % per_seq, 0))
    return pl.pallas_call(
        _inproj_kernel,
        grid=(m // IP_TM, N_MAIN // IP_TN),
        in_specs=[
            pl.BlockSpec((IP_TM, D_MODEL), lambda i, j: (i, 0)),
            pl.BlockSpec((1, D_MODEL), lambda i, j: (0, 0)),
            pl.BlockSpec((IP_TN, D_MODEL), lambda i, j: (j, 0)),
            pl.BlockSpec((LANES, D_MODEL), lambda i, j: (0, 0)),
            tab_spec, tab_spec, tab_spec,
        ],
        out_specs=[
            pl.BlockSpec((IP_TM, IP_TN), lambda i, j: (i, j)),
            pl.BlockSpec((IP_TM, LANES), lambda i, j: (i, 0)),
        ],
        out_shape=[jax.ShapeDtypeStruct((m, N_MAIN), F32),
                   jax.ShapeDtypeStruct((m, LANES), F32)],
        scratch_shapes=[pltpu.VMEM((IP_TM, D_MODEL), BF16)],
        compiler_params=pltpu.CompilerParams(
            dimension_semantics=("arbitrary", "arbitrary"), vmem_limit_bytes=VMEM_LIMIT),
        name="inproj",
    )(x, nw, w_bf, w_tail, *tabs)


AUX_ROWS = 64
AUX_TN = 512


def _aux_inproj_kernel(x_ref, nw_ref, w_ref, c_ref, sa_ref, sb_ref, o_ref, wbf_ref, wtail_ref, xs_ref):
    j = pl.program_id(0)

    @pl.when(j == 0)
    def _():
        xn = _rmsnorm(x_ref[...], nw_ref[...])
        hi = xn.astype(BF16)
        xs_ref[0:AUX_ROWS, :] = hi
        xs_ref[AUX_ROWS:2 * AUX_ROWS, :] = (xn - hi.astype(F32)).astype(BF16)

    w = w_ref[...]
    w_hi = w.astype(BF16)
    wbf_ref[...] = w_hi

    @pl.when(j == pl.num_programs(0) - 1)
    def _():
        row = lax.broadcasted_iota(jnp.int32, (LANES, D_MODEL), 0)
        wtail_ref[...] = jnp.where(row < D_IN - N_MAIN, w_hi[0:LANES, :], jnp.zeros((), BF16))

    w_lo = (w - w_hi.astype(F32)).astype(BF16)
    r1 = _dot_nt(xs_ref[...], w_hi)
    r2 = _dot_nt(xs_ref[0:AUX_ROWS, :], w_lo)
    acc = r1[0:AUX_ROWS] + (r1[AUX_ROWS:] + r2)
    is_rope = j < O_VA // AUX_TN

    @pl.when(is_rope)
    def _():
        o_ref[...] = _rope_tile(acc, c_ref[...], sa_ref[...], sb_ref[...])

    @pl.when(jnp.logical_not(is_rope))
    def _():
        o_ref[...] = acc


def _aux_inproj(xa, nw, w, tabs):
    tab_spec = pl.BlockSpec((AUX_ROWS, LANES), lambda j: (0, 0))
    return pl.pallas_call(
        _aux_inproj_kernel,
        grid=(pl.cdiv(D_IN, AUX_TN),),
        in_specs=[
            pl.BlockSpec((AUX_ROWS, D_MODEL), lambda j: (0, 0)),
            pl.BlockSpec((1, D_MODEL), lambda j: (0, 0)),
            pl.BlockSpec((AUX_TN, D_MODEL), lambda j: (j, 0)),
            tab_spec, tab_spec, tab_spec,
        ],
        out_specs=[pl.BlockSpec((AUX_ROWS, AUX_TN), lambda j: (0, j)),
                   pl.BlockSpec((AUX_TN, D_MODEL), lambda j: (j, 0)),
                   pl.BlockSpec((LANES, D_MODEL), lambda j: (0, 0))],
        out_shape=[jax.ShapeDtypeStruct((AUX_ROWS, D_IN), F32),
                   jax.ShapeDtypeStruct((D_IN, D_MODEL), BF16),
                   jax.ShapeDtypeStruct((LANES, D_MODEL), BF16)],
        scratch_shapes=[pltpu.VMEM((2 * AUX_ROWS, D_MODEL), BF16)],
        compiler_params=pltpu.CompilerParams(
            dimension_semantics=("arbitrary",), vmem_limit_bytes=VMEM_LIMIT),
        name="aux_inproj",
    )(xa, nw, w, *tabs)


AT_TQ = 512
AT_TK = 512
LOG2E = math.log2(math.e)


def _diff_lambda(lq1, lk1, lq2, lk2):
    a = jnp.exp(jnp.sum(lq1 * lk1, axis=-1, keepdims=True))
    b = jnp.exp(jnp.sum(lq2 * lk2, axis=-1, keepdims=True))
    return a - b + LAMBDA_INIT


def _attn_kernel(lq1_ref, lk1_ref, lq2_ref, lk2_ref, q_ref, k_ref, v_ref, km_ref, vm_ref,
                 za_ref, sw_ref, o_ref, kb_ref, vb_ref, kmb_ref, vmb_ref, q2_ref, s_ref, sm_ref,
                 m_ref, l_ref, acc_ref):
    qi = pl.program_id(2)
    tq, tk = AT_TQ, AT_TK
    n_tiles = tk // LANES

    @pl.when(qi == 0)
    def _():
        kb_ref[...] = k_ref[...].astype(BF16)
        vb_ref[...] = v_ref[...].astype(BF16)
        pad = jnp.zeros((LANES - N_META, LANES), BF16)
        kmb_ref[...] = jnp.concatenate([km_ref[...].astype(BF16), pad], axis=0)
        vmb_ref[...] = jnp.concatenate([vm_ref[...].astype(BF16), pad], axis=0)

    q = q_ref[...] * (A_DH ** -0.5 * LOG2E)
    lane = lax.broadcasted_iota(jnp.int32, (tq, LANES), 1)
    q2_ref[0:tq, :] = jnp.where(lane < A_DH, q, 0.0).astype(BF16)
    q2_ref[tq:2 * tq, :] = jnp.where(lane >= A_DH, q, 0.0).astype(BF16)

    def tile_max(m, s):
        for c in range(s.shape[1] // LANES):
            m = jnp.maximum(m, s[:, c * LANES:(c + 1) * LANES])
        return m

    lane2 = lax.broadcasted_iota(jnp.int32, (2 * tq, LANES), 1)
    s = jnp.where(lane2 < N_META, _dot_nt(q2_ref[...], kmb_ref[...]), NEG)
    sm_ref[...] = s
    m_ref[...] = s

    def scores(j):
        rows = pl.ds(pl.multiple_of(j * tk, tk), tk)
        return _dot_nt(q2_ref[...], kb_ref[rows, :])

    def pass1(j, carry):
        s = scores(j)
        s_ref[j] = s
        m_ref[...] = tile_max(m_ref[...], s)
        return carry
    lax.fori_loop(0, qi, pass1, 0)

    r = lax.broadcasted_iota(jnp.int32, (2 * tq, tk), 0)
    r = jnp.where(r >= tq, r - tq, r)
    c = lax.broadcasted_iota(jnp.int32, (2 * tq, tk), 1)
    s = jnp.where(c <= r, scores(qi), NEG)
    s_ref[qi] = s
    m = jnp.max(tile_max(m_ref[...], s), axis=-1, keepdims=True)
    m_ref[...] = jnp.broadcast_to(m, (2 * tq, LANES))

    p = jnp.exp2(sm_ref[...] - m_ref[...])
    l_ref[...] = p
    acc_ref[...] = _dot(p.astype(BF16), vmb_ref[...])

    def pass2(j, carry):
        mb = m_ref[...]
        lsum = l_ref[...]
        ps = []
        for t in range(n_tiles):
            p = jnp.exp2(s_ref[j, :, t * LANES:(t + 1) * LANES] - mb)
            lsum = lsum + p
            ps.append(p.astype(BF16))
        l_ref[...] = lsum
        rows = pl.ds(pl.multiple_of(j * tk, tk), tk)
        acc_ref[...] += _dot(jnp.concatenate(ps, axis=1), vb_ref[rows, :])
        return carry
    lax.fori_loop(0, qi + 1, pass2, 0)

    lam = _diff_lambda(lq1_ref[...], lk1_ref[...], lq2_ref[...], lk2_ref[...])
    l = jnp.sum(l_ref[...], axis=-1, keepdims=True)
    o = acc_ref[0:tq, :] / l[0:tq] - lam * (acc_ref[tq:2 * tq, :] / l[tq:2 * tq])
    y = _rmsnorm(o, sw_ref[...]) * (1.0 - LAMBDA_INIT)
    o_ref[...] = (y * _silu(za_ref[...])).astype(BF16)


def _attention(lams, h_main, k_meta, v_meta, subln_w):
    nq = SEQ // AT_TQ
    lam_spec = pl.BlockSpec((1, A_DH), lambda b, h, i: (0, 0))
    return pl.pallas_call(
        _attn_kernel,
        grid=(BATCH, A_HEADS, nq),
        in_specs=[
            lam_spec, lam_spec, lam_spec, lam_spec,
            pl.BlockSpec((AT_TQ, LANES), lambda b, h, i: (b * nq + i, O_QA // LANES + h)),
            pl.BlockSpec((SEQ, LANES), lambda b, h, i: (b, O_KA // LANES + h)),
            pl.BlockSpec((SEQ, LANES), lambda b, h, i: (b, O_VA // LANES + h)),
            pl.BlockSpec((N_META, LANES), lambda b, h, i: (0, h)),
            pl.BlockSpec((N_META, LANES), lambda b, h, i: (0, h)),
            pl.BlockSpec((AT_TQ, LANES), lambda b, h, i: (b * nq + i, O_ZA // LANES + h)),
            pl.BlockSpec((1, A_DV), lambda b, h, i: (0, 0)),
        ],
        out_specs=pl.BlockSpec((AT_TQ, LANES), lambda b, h, i: (b * nq + i, h)),
        out_shape=jax.ShapeDtypeStruct((BATCH * SEQ, A_HEADS * A_DV), BF16),
        scratch_shapes=[
            pltpu.VMEM((SEQ, LANES), BF16), pltpu.VMEM((SEQ, LANES), BF16),
            pltpu.VMEM((LANES, LANES), BF16), pltpu.VMEM((LANES, LANES), BF16),
            pltpu.VMEM((2 * AT_TQ, LANES), BF16),
            pltpu.VMEM((SEQ // AT_TK, 2 * AT_TQ, AT_TK), F32),
            pltpu.VMEM((2 * AT_TQ, LANES), F32),
            pltpu.VMEM((2 * AT_TQ, LANES), F32), pltpu.VMEM((2 * AT_TQ, LANES), F32),
            pltpu.VMEM((2 * AT_TQ, A_DV), F32),
        ],
        compiler_params=pltpu.CompilerParams(
            dimension_semantics=("arbitrary", "arbitrary", "arbitrary"),
            vmem_limit_bytes=VMEM_LIMIT),
        name="diff_attn",
    )(*lams, h_main, h_main, h_main, k_meta, v_meta, h_main, subln_w)


def _unit_lower_inverse_minus_eye(a_list, i_idx, j_idx):
    base = 8
    diag = (i_idx // base) == (j_idx // base)
    b = [jnp.where(diag, a, 0.0) for a in a_list]
    n = [-x for x in b]
    for _ in range(2):
        bb = [x.astype(BF16) for x in b]
        b = [_dot(x, x) for x in bb]
        nb = [_dot(x.astype(BF16), y.astype(BF16)) for x, y in zip(n, b)]
        n = [x + y + z for x, y, z in zip(n, b, nb)]
    s = base
    while s < CHUNK:
        join = ((i_idx // (2 * s)) == (j_idx // (2 * s))) & ((i_idx // s) % 2 == 1) & ((j_idx // s) % 2 == 0)
        a_s = [jnp.where(join, a, 0.0) for a in a_list]
        x = [p + _dot(q.astype(BF16), p.astype(BF16)) for p, q in zip(a_s, n)]
        xn = [_dot(p.astype(BF16), q.astype(BF16)) for p, q in zip(x, n)]
        n = [q - (p + r) for q, p, r in zip(n, x, xn)]
        s *= 2
    return n


def _gdn_kernel(xq_ref, xk_ref, xv_ref, ab_ref, abt_ref, zb_ref, cw_ref, alog_ref, dtb_ref,
                alogc_ref, dtbc_ref, gw_ref, halo_ref, s0_ref, yb_ref, sfin_ref, ext_ref, s_ref,
                *, masked_rows):
    c = pl.program_id(1)
    hist = 8

    @pl.when(c == 0)
    def _():
        ext_ref[0:hist, :] = halo_ref[...]
        s_ref[...] = s0_ref[...]

    x = jnp.concatenate([xq_ref[...], xk_ref[...], xv_ref[...]], axis=1)
    ext_ref[hist:hist + CHUNK, :] = x
    y = cw_ref[CONV_W - 1:CONV_W, :] * x
    for t in range(CONV_W - 1):
        lo = hist - (CONV_W - 1) + t
        y = y + cw_ref[t:t + 1, :] * ext_ref[lo:lo + CHUNK, :]
    y = _silu(y)
    ext_ref[0:hist, :] = x[CHUNK - hist:CHUNK, :]

    i_idx = lax.broadcasted_iota(jnp.int32, (CHUNK, CHUNK), 0)
    j_idx = lax.broadcasted_iota(jnp.int32, (CHUNK, CHUNK), 1)
    tril = i_idx >= j_idx
    strict = i_idx > j_idx
    tril_f = tril.astype(F32)
    triu_f = (i_idx <= j_idx).astype(F32)

    ab = ab_ref[...]
    g_c = -jnp.exp(alog_ref[...]) * _softplus(ab + dtb_ref[...])
    beta_c = _sigmoid(ab)
    abt = abt_ref[...]
    g_r = -jnp.exp(alogc_ref[...]) * _softplus(abt + dtbc_ref[...])
    if masked_rows:
        row_ok = lax.broadcasted_iota(jnp.int32, (CHUNK, LANES), 0) >= masked_rows
        col_ok = lax.broadcasted_iota(jnp.int32, (2 * B_HEADS, CHUNK), 1) >= masked_rows
        g_c = jnp.where(row_ok, g_c, 0.0)
        beta_c = jnp.where(row_ok, beta_c, 0.0)
        g_r = jnp.where(col_ok, g_r, 0.0)
    gc_c = _dot_f32(tril_f, g_c)
    gc_r = _dot_f32(g_r, triu_f)

    nk = B_HEADS * B_DK
    heads = range(B_HEADS)

    def l2n(t):
        return t * lax.rsqrt(jnp.sum(t * t, axis=-1, keepdims=True) + EPS)

    qn = [l2n(y[:, h * B_DK:(h + 1) * B_DK]) * (B_DK ** -0.5) for h in heads]
    kn = [l2n(y[:, nk + h * B_DK:nk + (h + 1) * B_DK]) for h in heads]
    vh = [y[:, 2 * nk + h * B_DV:2 * nk + (h + 1) * B_DV] for h in heads]
    bcol = [beta_c[:, B_HEADS + h:B_HEADS + h + 1] for h in heads]
    gcc = [gc_c[:, h:h + 1] for h in heads]
    decay = [jnp.where(tril, jnp.exp(jnp.where(tril, gcc[h] - gc_r[h:h + 1, :], 0.0)), 0.0)
             for h in heads]
    kbeta = [kn[h] * bcol[h] for h in heads]
    kn_b = [t.astype(BF16) for t in kn]
    kk = [_dot_nt(kbeta[h].astype(BF16), kn_b[h]) for h in heads]
    qk = [_dot_nt(qn[h].astype(BF16), kn_b[h]) for h in heads]
    a = [jnp.where(strict, kk[h] * decay[h], 0.0) for h in heads]
    n = _unit_lower_inverse_minus_eye(a, i_idx, j_idx)
    egc = [jnp.exp(t) for t in gcc]
    rhs = [jnp.concatenate([vh[h] * bcol[h], kbeta[h] * egc[h]], axis=1) for h in heads]
    nr = [_dot(n[h].astype(BF16), rhs[h].astype(BF16)) for h in heads]
    sol = [rhs[h] + nr[h] for h in heads]
    st = [s_ref[h] for h in heads]
    st_b = [t.astype(BF16) for t in st]
    ws = [_dot(sol[h][:, B_DV:B_DV + B_DK].astype(BF16), st_b[h]) for h in heads]
    qs = [_dot((qn[h] * egc[h]).astype(BF16), st_b[h]) for h in heads]
    v_new_b = [(sol[h][:, 0:B_DV] - ws[h]).astype(BF16) for h in heads]
    av = [_dot((qk[h] * decay[h]).astype(BF16), v_new_b[h]) for h in heads]
    g_last = [t[CHUNK - 1:CHUNK, :] for t in gcc]
    ke_t = [(kn[h] * jnp.exp(g_last[h] - gcc[h])).T.astype(BF16) for h in heads]
    kv = [_dot(ke_t[h], v_new_b[h]) for h in heads]
    for h in heads:
        s_ref[h] = st[h] * jnp.exp(g_last[h]) + kv[h]
    for h in heads:
        zb = zb_ref[:, h * B_DV:(h + 1) * B_DV]
        yb_ref[:, h * B_DV:(h + 1) * B_DV] = (_rmsnorm(qs[h] + av[h], gw_ref[...]) * _silu(zb)).astype(BF16)

    sfin_ref[...] = s_ref[...]


def _gdn_chunks(x, x_col, ab, ab_col, abt, zsrc, zb_col, cw, gvecs, gw, halo, s0, n_seq, n_chunk,
                masked_rows):
    alog_l, dtb_l, alog_c, dtb_c = gvecs
    const2 = lambda b, c: (0, 0)
    kern = functools.partial(_gdn_kernel, masked_rows=masked_rows)
    part = B_QKV // 3

    def x_spec(k):
        return pl.BlockSpec((CHUNK, part), lambda b, c: (b * n_chunk + c, x_col + k))

    return pl.pallas_call(
        kern,
        grid=(n_seq, n_chunk),
        in_specs=[
            x_spec(0), x_spec(1), x_spec(2),
            pl.BlockSpec((CHUNK, LANES), lambda b, c: (b * n_chunk + c, ab_col)),
            pl.BlockSpec((None, None, 2 * B_HEADS, CHUNK), lambda b, c: (b, c, 0, 0)),
            pl.BlockSpec((CHUNK, B_HEADS * B_DV), lambda b, c: (b * n_chunk + c, zb_col)),
            pl.BlockSpec((CONV_W, B_QKV), const2),
            pl.BlockSpec((1, LANES), const2), pl.BlockSpec((1, LANES), const2),
            pl.BlockSpec((2 * B_HEADS, 1), const2), pl.BlockSpec((2 * B_HEADS, 1), const2),
            pl.BlockSpec((1, B_DV), const2),
            pl.BlockSpec((8, B_QKV), const2),
            pl.BlockSpec((B_HEADS, B_DK, B_DV), lambda b, c: (0, 0, 0)),
        ],
        out_specs=[
            pl.BlockSpec((CHUNK, B_HEADS * B_DV), lambda b, c: (b * n_chunk + c, 0)),
            pl.BlockSpec((None, B_HEADS, B_DK, B_DV), lambda b, c: (b, 0, 0, 0)),
        ],
        out_shape=[jax.ShapeDtypeStruct((n_seq * n_chunk * CHUNK, B_HEADS * B_DV), BF16),
                   jax.ShapeDtypeStruct((n_seq, B_HEADS, B_DK, B_DV), F32)],
        scratch_shapes=[pltpu.VMEM((8 + CHUNK, B_QKV), F32),
                        pltpu.VMEM((B_HEADS, B_DK, B_DV), F32)],
        compiler_params=pltpu.CompilerParams(
            dimension_semantics=("arbitrary", "arbitrary"), vmem_limit_bytes=VMEM_LIMIT),
        name="gdn_chunks",
    )(x, x, x, ab, abt, zsrc, cw, alog_l, dtb_l, alog_c, dtb_c, gw, halo, s0)


def _shift_lanes(x, k):
    nblk = x.shape[1] // LANES
    r = [pltpu.roll(x[:, c * LANES:(c + 1) * LANES], LANES - k, 1) for c in range(nblk)]
    lane = lax.broadcasted_iota(jnp.int32, (x.shape[0], LANES), 1)
    return jnp.concatenate([jnp.where(lane < LANES - k, r[c], r[c + 1]) for c in range(nblk - 1)],
                           axis=1)


def _merge_kernel(x_ref, ya_ref, yb_ref, g4_ref, g5_ref, gt_ref, wpa_ref, wpb_ref, wo_ref, fw_ref,
                  y_ref):
    g5 = g5_ref[...]
    ga = _shift_lanes(jnp.concatenate([g4_ref[...], g5[:, 0:LANES]], axis=1), GATE_SHIFT)
    gb = _shift_lanes(jnp.concatenate([g5, gt_ref[...]], axis=1), GATE_SHIFT)
    pa = _dot(ya_ref[...], wpa_ref[...])
    pb = _dot(yb_ref[...], wpb_ref[...])
    mixed = _sigmoid(ga) * pa + _sigmoid(gb) * pb
    hp = x_ref[...] + _dot(mixed.astype(BF16), wo_ref[...])
    y_ref[...] = _rmsnorm(hp, fw_ref[...])


def _merge(x, ya, yb, hsrc, htail, wpa, wpb, wo, fw, tm):
    m = x.shape[0]
    once = pl.Buffered(1)
    return pl.pallas_call(
        _merge_kernel,
        grid=(m // tm,),
        in_specs=[
            pl.BlockSpec((tm, D_MODEL), lambda i: (i, 0)),
            pl.BlockSpec((tm, A_HEADS * A_DV), lambda i: (i, 0)),
            pl.BlockSpec((tm, B_HEADS * B_DV), lambda i: (i, 0)),
            pl.BlockSpec((tm, D_MODEL), lambda i: (i, O_A // D_MODEL)),
            pl.BlockSpec((tm, D_MODEL), lambda i: (i, O_A // D_MODEL + 1)),
            pl.BlockSpec((tm, LANES), lambda i: (i, 0)),
            pl.BlockSpec((A_HEADS * A_DV, D_MODEL), lambda i: (0, 0), pipeline_mode=once),
            pl.BlockSpec((B_HEADS * B_DV, D_MODEL), lambda i: (0, 0), pipeline_mode=once),
            pl.BlockSpec((D_MODEL, D_MODEL), lambda i: (0, 0), pipeline_mode=once),
            pl.BlockSpec((1, D_MODEL), lambda i: (0, 0)),
        ],
        out_specs=pl.BlockSpec((tm, D_MODEL), lambda i: (i, 0)),
        out_shape=jax.ShapeDtypeStruct((m, D_MODEL), F32),
        compiler_params=pltpu.CompilerParams(
            dimension_semantics=("arbitrary",), vmem_limit_bytes=VMEM_LIMIT),
        name="merge",
    )(x, ya, yb, hsrc, hsrc, htail, wpa, wpb, wo, fw)


DA_NBUF = 16
DA_GRP = 4
DA_G = N_PAGES // DA_NBUF
TOK_TILES = PAGE_SIZE * A_HEADS // LANES


def _lane_group_reduce(x, op):
    s = A_HEADS
    while s < LANES:
        x = op(x, pltpu.roll(x, s, 1))
        s *= 2
    return x


def _decode_attn_kernel(pt_ref, lq1_ref, lk1_ref, lq2_ref, lk2_ref, q_ref, kn_ref, vn_ref,
                        za_ref, sw_ref, ck_ref, cv_ref, o_ref, buf_ref, sem_ref, sc_ref):
    b = pl.program_id(0)
    n_seq = pl.num_programs(0)

    def page_copy(src_ref, seq, page_idx, slot):
        return pltpu.make_async_copy(src_ref.at[0, pt_ref[seq, page_idx]], buf_ref.at[slot],
                                     sem_ref.at[slot])

    def wait_slot(slot):
        page_copy(ck_ref, b, 0, slot).wait()

    @pl.when(b == 0)
    def _():
        for s in range(DA_NBUF):
            page_copy(ck_ref, 0, s, s).start(priority=s % 2)

    sub = lax.broadcasted_iota(jnp.int32, (A_HEADS, LANES), 0)
    lane = lax.broadcasted_iota(jnp.int32, (A_HEADS, LANES), 1)
    head_of_lane = lane % A_HEADS
    hmask = head_of_lane == sub
    half_sel = jnp.where((sub == 0) & (lane < A_DH), 1.0,
                         jnp.where((sub == 1) & (lane >= A_DH), 1.0, 0.0)).astype(BF16)

    q = q_ref[...]

    def k_turn(g, carry):
        for s0 in range(0, DA_NBUF, DA_GRP):
            slots = range(s0, s0 + DA_GRP)
            for s in slots:
                wait_slot(s)
            prods = [(buf_ref[s] * q[None]).reshape(PAGE_SIZE * A_HEADS, LANES).astype(BF16)
                     for s in slots]
            scs = [_dot_nt(half_sel, p) for p in prods]
            for s, sc in zip(slots, scs):
                base = (g * DA_NBUF + s) * TOK_TILES
                for c in range(TOK_TILES):
                    sc_ref[base + c] = sc[:, c * LANES:(c + 1) * LANES]

            @pl.when(g < DA_G - 1)
            def _():
                for s in slots:
                    page_copy(ck_ref, b, (g + 1) * DA_NBUF + s, s).start(priority=s % 2)

            @pl.when(g == DA_G - 1)
            def _():
                for s in slots:
                    page_copy(cv_ref, b, s, s).start(priority=s % 2)
        return carry
    lax.fori_loop(0, DA_G, k_turn, 0)

    lam = _diff_lambda(lq1_ref[...], lk1_ref[...], lq2_ref[...], lk2_ref[...])
    prod = q * kn_ref[...]
    hs1 = jnp.sum(jnp.where(lane < A_DH, prod, 0.0), axis=1, keepdims=True)
    hs2 = jnp.sum(jnp.where(lane >= A_DH, prod, 0.0), axis=1, keepdims=True)
    row1 = jnp.sum(jnp.where(hmask, hs1, 0.0), axis=0, keepdims=True)
    row2 = jnp.sum(jnp.where(hmask, hs2, 0.0), axis=0, keepdims=True)
    s_new = jnp.where(sub == 0, row1, jnp.where(sub == 1, row2, 0.0))
    sc = sc_ref[...]
    mx = _lane_group_reduce(jnp.max(sc, axis=0), jnp.maximum)
    mx = jnp.maximum(mx, s_new)
    p = jnp.exp(sc - mx[None])
    p_new = jnp.exp(s_new - mx)
    den = _lane_group_reduce(jnp.sum(p, axis=0), jnp.add) + p_new
    coef = jnp.where(sub == 0, 1.0 / den, jnp.where(sub == 1, -lam / den, 0.0))
    sc_ref[...] = p * coef[None]
    wn = jnp.sum(p_new * coef, axis=0, keepdims=True)

    def page_weights(page):
        tiles = []
        for c in range(TOK_TILES):
            r = jnp.sum(sc_ref[page * TOK_TILES + c], axis=0, keepdims=True)
            tiles.append(jnp.where(hmask, r, 0.0))
        w = jnp.concatenate(tiles, axis=1)
        w_hi = w.astype(BF16)
        w_lo = (w - w_hi.astype(F32)).astype(BF16)
        return jnp.concatenate([w_hi, w_lo], axis=0)

    def v_turn(g, acc):
        for s0 in range(0, DA_NBUF, DA_GRP):
            slots = range(s0, s0 + DA_GRP)
            ws = [page_weights(g * DA_NBUF + s) for s in slots]
            for s in slots:
                wait_slot(s)
            vs = [buf_ref[s].reshape(PAGE_SIZE * A_HEADS, LANES).astype(BF16) for s in slots]
            rs = [_dot(w, v) for w, v in zip(ws, vs)]
            for r in rs:
                acc = acc + (r[0:A_HEADS] + r[A_HEADS:])

            @pl.when(g < DA_G - 1)
            def _():
                for s in slots:
                    page_copy(cv_ref, b, (g + 1) * DA_NBUF + s, s).start(priority=s % 2)

            @pl.when((g == DA_G - 1) & (b < n_seq - 1))
            def _():
                for s in slots:
                    page_copy(ck_ref, b + 1, s, s).start(priority=s % 2)
        return acc
    acc = lax.fori_loop(0, DA_G, v_turn, jnp.zeros((A_HEADS, A_DV), F32))

    w_new = jnp.sum(jnp.where(hmask, wn, 0.0), axis=1, keepdims=True) * (A_HEADS / LANES)
    o = acc + w_new * vn_ref[...]
    y = _rmsnorm(o, sw_ref[...]) * (1.0 - LAMBDA_INIT)
    o_ref[...] = y * _silu(za_ref[...])


def _decode_attention(page_table, lams, q_s, k_new, v_new, za_s, subln_w, cache_k, cache_v):
    lam_spec = pl.BlockSpec((1, A_DH), lambda b, pt: (0, 0))
    row_spec = pl.BlockSpec((None, A_HEADS, LANES), lambda b, pt: (b, 0, 0))
    hbm_spec = pl.BlockSpec(memory_space=pl.ANY)
    grid_spec = pltpu.PrefetchScalarGridSpec(
        num_scalar_prefetch=1,
        grid=(DEC_BATCH,),
        in_specs=[lam_spec, lam_spec, lam_spec, lam_spec, row_spec, row_spec, row_spec, row_spec,
                  pl.BlockSpec((1, A_DV), lambda b, pt: (0, 0)), hbm_spec, hbm_spec],
        out_specs=row_spec,
        scratch_shapes=[pltpu.VMEM((DA_NBUF, PAGE_SIZE, A_HEADS, LANES), F32),
                        pltpu.SemaphoreType.DMA((DA_NBUF,)),
                        pltpu.VMEM((N_PAGES * TOK_TILES, A_HEADS, LANES), F32)],
    )
    return pl.pallas_call(
        _decode_attn_kernel,
        grid_spec=grid_spec,
        out_shape=jax.ShapeDtypeStruct((DEC_BATCH, A_HEADS, A_DV), F32),
        compiler_params=pltpu.CompilerParams(
            dimension_semantics=("arbitrary",), vmem_limit_bytes=VMEM_LIMIT),
        name="decode_attn",
    )(page_table, *lams, q_s, k_new, v_new, za_s, subln_w, cache_k, cache_v)


FD_PAGES = N_PAGES // 2
FD_STEPS = 4
FD_NBUF = 2 * FD_PAGES


def _gdn_decode_kernel(pt_ref, xq_ref, xk_ref, xv_ref, ab_ref, abt_ref, zb_ref, cw_ref, alog_ref,
                       dtb_ref, alogc_ref, dtbc_ref, gw_ref, halo_ref, s0_ref,
                       lq1_ref, lk1_ref, lq2_ref, lk2_ref, q_ref, kn_ref, vn_ref, za_ref, sw_ref,
                       ck_ref, cv_ref, yb_ref, sfin_ref, o_ref,
                       ext_ref, s_ref, buf_ref, sem_ref, sc_ref, acc_ref, wn_ref):
    n_chunk = pl.num_programs(1)
    t = pl.program_id(0) * n_chunk + pl.program_id(1)
    n_steps = pl.num_programs(0) * n_chunk
    role = t % FD_STEPS

    def start_step(step):
        seq = step // FD_STEPS
        r = step % FD_STEPS
        half = (step % 2) * FD_PAGES
        first = (r % 2) * FD_PAGES

        def copies(src_ref):
            for s in range(FD_PAGES):
                pltpu.make_async_copy(src_ref.at[0, pt_ref[seq, first + s]], buf_ref.at[half + s],
                                      sem_ref.at[half + s]).start(priority=s % 2)

        @pl.when(r < 2)
        def _():
            copies(ck_ref)

        @pl.when(r >= 2)
        def _():
            copies(cv_ref)

    def wait_slot(slot):
        pltpu.make_async_copy(ck_ref.at[0, 0], buf_ref.at[slot], sem_ref.at[slot]).wait()

    @pl.when(t == 0)
    def _():
        start_step(0)

    @pl.when(t + 1 < n_steps)
    def _():
        start_step(t + 1)

    _gdn_kernel(xq_ref, xk_ref, xv_ref, ab_ref, abt_ref, zb_ref, cw_ref, alog_ref, dtb_ref,
                alogc_ref, dtbc_ref, gw_ref, halo_ref, s0_ref, yb_ref, sfin_ref, ext_ref, s_ref,
                masked_rows=0)

    half = (t % 2) * FD_PAGES
    sub = lax.broadcasted_iota(jnp.int32, (A_HEADS, LANES), 0)
    lane = lax.broadcasted_iota(jnp.int32, (A_HEADS, LANES), 1)
    hmask = (lane % A_HEADS) == sub
    half_sel = jnp.where((sub == 0) & (lane < A_DH), 1.0,
                         jnp.where((sub == 1) & (lane >= A_DH), 1.0, 0.0)).astype(BF16)
    n_grp = FD_PAGES // DA_GRP

    @pl.when(role < 2)
    def _():
        q = q_ref[...]

        def k_group(g, carry):
            slots = [half + g * DA_GRP + i for i in range(DA_GRP)]
            for s in slots:
                wait_slot(s)
            prods = [(buf_ref[s] * q[None]).reshape(PAGE_SIZE * A_HEADS, LANES).astype(BF16)
                     for s in slots]
            scs = [_dot_nt(half_sel, p) for p in prods]
            for i, sc in enumerate(scs):
                base = (role * FD_PAGES + g * DA_GRP + i) * TOK_TILES
                for c in range(TOK_TILES):
                    sc_ref[base + c] = sc[:, c * LANES:(c + 1) * LANES]
            return carry
        lax.fori_loop(0, n_grp, k_group, 0)

    @pl.when(role == 1)
    def _():
        lam = _diff_lambda(lq1_ref[...], lk1_ref[...], lq2_ref[...], lk2_ref[...])
        prod = q_ref[...] * kn_ref[...]
        hs1 = jnp.sum(jnp.where(lane < A_DH, prod, 0.0), axis=1, keepdims=True)
        hs2 = jnp.sum(jnp.where(lane >= A_DH, prod, 0.0), axis=1, keepdims=True)
        row1 = jnp.sum(jnp.where(hmask, hs1, 0.0), axis=0, keepdims=True)
        row2 = jnp.sum(jnp.where(hmask, hs2, 0.0), axis=0, keepdims=True)
        s_new = jnp.where(sub == 0, row1, jnp.where(sub == 1, row2, 0.0))
        sc = sc_ref[...]
        mx = _lane_group_reduce(jnp.max(sc, axis=0), jnp.maximum)
        mx = jnp.maximum(mx, s_new)
        p = jnp.exp(sc - mx[None])
        p_new = jnp.exp(s_new - mx)
        den = _lane_group_reduce(jnp.sum(p, axis=0), jnp.add) + p_new
        coef = jnp.where(sub == 0, 1.0 / den, jnp.where(sub == 1, -lam / den, 0.0))
        sc_ref[...] = p * coef[None]
        wn_ref[...] = jnp.sum(p_new * coef, axis=0, keepdims=True)
        acc_ref[...] = jnp.zeros_like(acc_ref)

    @pl.when(role >= 2)
    def _():
        def page_weights(page):
            tiles = []
            for c in range(TOK_TILES):
                r = jnp.sum(sc_ref[page * TOK_TILES + c], axis=0, keepdims=True)
                tiles.append(jnp.where(hmask, r, 0.0))
            w = jnp.concatenate(tiles, axis=1)
            w_hi = w.astype(BF16)
            w_lo = (w - w_hi.astype(F32)).astype(BF16)
            return jnp.concatenate([w_hi, w_lo], axis=0)

        def v_group(g, acc):
            first = (role - 2) * FD_PAGES + g * DA_GRP
            ws = [page_weights(first + i) for i in range(DA_GRP)]
            slots = [half + g * DA_GRP + i for i in range(DA_GRP)]
            for s in slots:
                wait_slot(s)
            vs = [buf_ref[s].reshape(PAGE_SIZE * A_HEADS, LANES).astype(BF16) for s in slots]
            rs = [_dot(w, v) for w, v in zip(ws, vs)]
            for r in rs:
                acc = acc + (r[0:A_HEADS] + r[A_HEADS:])
            return acc
        acc_ref[...] = lax.fori_loop(0, n_grp, v_group, acc_ref[...])

    @pl.when(role == FD_STEPS - 1)
    def _():
        w_new = jnp.sum(jnp.where(hmask, wn_ref[...], 0.0), axis=1, keepdims=True) * (A_HEADS / LANES)
        o = acc_ref[...] + w_new * vn_ref[...]
        y = _rmsnorm(o, sw_ref[...]) * (1.0 - LAMBDA_INIT)
        o_ref[...] = y * _silu(za_ref[...])


def _gdn_decode(page_table, x, ab_src, abt, cw, gvecs, gw, halo, s0,
                lams, q_s, k_new, v_new, za_s, subln_w, cache_k, cache_v):
    alog_l, dtb_l, alog_c, dtb_c = gvecs
    n_chunk = SEQ // CHUNK
    assert BATCH * n_chunk == DEC_BATCH * FD_STEPS
    part = B_QKV // 3
    row = lambda b, c, pt: (b * n_chunk + c, 0)
    const2 = lambda b, c, pt: (0, 0)
    seq3 = lambda b, c, pt: ((b * n_chunk + c) // FD_STEPS, 0, 0)

    def col(width, k):
        return pl.BlockSpec((CHUNK, width), lambda b, c, pt: (b * n_chunk + c, k))

    lam_spec = pl.BlockSpec((1, A_DH), const2)
    seq_spec = pl.BlockSpec((None, A_HEADS, LANES), seq3)
    hbm_spec = pl.BlockSpec(memory_space=pl.ANY)
    grid_spec = pltpu.PrefetchScalarGridSpec(
        num_scalar_prefetch=1,
        grid=(BATCH, n_chunk),
        in_specs=[
            col(part, O_QKVB // part), col(part, O_QKVB // part + 1), col(part, O_QKVB // part + 2),
            col(LANES, O_A // LANES),
            pl.BlockSpec((None, None, 2 * B_HEADS, CHUNK), lambda b, c, pt: (b, c, 0, 0)),
            col(part, O_ZB // part),
            pl.BlockSpec((CONV_W, B_QKV), const2),
            pl.BlockSpec((1, LANES), const2), pl.BlockSpec((1, LANES), const2),
            pl.BlockSpec((2 * B_HEADS, 1), const2), pl.BlockSpec((2 * B_HEADS, 1), const2),
            pl.BlockSpec((1, B_DV), const2),
            pl.BlockSpec((8, B_QKV), const2),
            pl.BlockSpec((B_HEADS, B_DK, B_DV), lambda b, c, pt: (0, 0, 0)),
            lam_spec, lam_spec, lam_spec, lam_spec, seq_spec, seq_spec, seq_spec, seq_spec,
            pl.BlockSpec((1, A_DV), const2), hbm_spec, hbm_spec,
        ],
        out_specs=[
            pl.BlockSpec((CHUNK, B_HEADS * B_DV), row),
            pl.BlockSpec((None, B_HEADS, B_DK, B_DV), lambda b, c, pt: (b, 0, 0, 0)),
            seq_spec,
        ],
        scratch_shapes=[
            pltpu.VMEM((8 + CHUNK, B_QKV), F32),
            pltpu.VMEM((B_HEADS, B_DK, B_DV), F32),
            pltpu.VMEM((FD_NBUF, PAGE_SIZE, A_HEADS, LANES), F32),
            pltpu.SemaphoreType.DMA((FD_NBUF,)),
            pltpu.VMEM((N_PAGES * TOK_TILES, A_HEADS, LANES), F32),
            pltpu.VMEM((A_HEADS, A_DV), F32),
            pltpu.VMEM((1, LANES), F32),
        ],
    )
    return pl.pallas_call(
        _gdn_decode_kernel,
        grid_spec=grid_spec,
        out_shape=[jax.ShapeDtypeStruct((BATCH * SEQ, B_HEADS * B_DV), BF16),
                   jax.ShapeDtypeStruct((BATCH, B_HEADS, B_DK, B_DV), F32),
                   jax.ShapeDtypeStruct((DEC_BATCH, A_HEADS, A_DV), F32)],
        compiler_params=pltpu.CompilerParams(
            dimension_semantics=("arbitrary", "arbitrary"), vmem_limit_bytes=VMEM_LIMIT),
        name="gdn_decode",
    )(page_table, x, x, x, ab_src, abt, x, cw, alog_l, dtb_l, alog_c, dtb_c, gw, halo, s0,
      *lams, q_s, k_new, v_new, za_s, subln_w, cache_k, cache_v)


def _gdn_step_kernel(sc_ref, x_ref, cw_ref, ab_ref, alog_ref, dtb_ref, zb_ref, gw_ref, s_ref,
                     yb_ref, sout_ref, cout_ref):
    x = x_ref[...]
    y = cw_ref[CONV_W - 1] * x
    for t in range(CONV_W - 1):
        y = y + cw_ref[t] * sc_ref[t]
    y = _silu(y)
    for t in range(CONV_W - 2):
        cout_ref[t] = sc_ref[t + 1]
    cout_ref[CONV_W - 2] = x

    q = y[0:B_HEADS]
    k = y[B_HEADS:2 * B_HEADS]
    v = y[2 * B_HEADS:3 * B_HEADS]
    qn = q * lax.rsqrt(jnp.sum(q * q, axis=-1, keepdims=True) + EPS) * (B_DK ** -0.5)
    kn = k * lax.rsqrt(jnp.sum(k * k, axis=-1, keepdims=True) + EPS)
    qt = qn.T
    kt = kn.T
    ab = ab_ref[...]
    g = -jnp.exp(alog_ref[...]) * _softplus(ab + dtb_ref[...])
    beta = _sigmoid(ab)
    for h in range(B_HEADS):
        kcol = kt[:, h:h + 1]
        qcol = qt[:, h:h + 1]
        st = s_ref[h] * jnp.exp(g[:, h:h + 1])
        kv = jnp.sum(st * kcol, axis=0, keepdims=True)
        d = (v[h:h + 1] - kv) * beta[:, B_HEADS + h:B_HEADS + h + 1]
        st = st + kcol * d
        sout_ref[h] = st
        o = jnp.sum(st * qcol, axis=0, keepdims=True)
        yb_ref[h:h + 1, :] = _rmsnorm(o, gw_ref[...]) * _silu(zb_ref[h:h + 1, :])


def _gdn_step(state_conv, x, cw, ab, alog_l, dtb_l, zb, gw, state_ssm):
    rows = B_QKV // LANES
    seq3 = lambda b: (b, 0, 0)
    seq4 = lambda b: (b, 0, 0, 0)
    const2 = lambda b: (0, 0)
    return pl.pallas_call(
        _gdn_step_kernel,
        grid=(DEC_BATCH,),
        in_specs=[
            pl.BlockSpec((None, CONV_W - 1, rows, LANES), seq4),
            pl.BlockSpec((None, rows, LANES), seq3),
            pl.BlockSpec((CONV_W, rows, LANES), lambda b: (0, 0, 0)),
            pl.BlockSpec((None, 1, LANES), seq3),
            pl.BlockSpec((1, LANES), const2), pl.BlockSpec((1, LANES), const2),
            pl.BlockSpec((None, B_HEADS, B_DV), seq3),
            pl.BlockSpec((1, B_DV), const2),
            pl.BlockSpec((None, B_HEADS, B_DK, B_DV), seq4),
        ],
        out_specs=[
            pl.BlockSpec((None, B_HEADS, B_DV), seq3),
            pl.BlockSpec((None, B_HEADS, B_DK, B_DV), seq4),
            pl.BlockSpec((None, CONV_W - 1, rows, LANES), seq4),
        ],
        out_shape=[jax.ShapeDtypeStruct((DEC_BATCH, B_HEADS, B_DV), F32),
                   jax.ShapeDtypeStruct((DEC_BATCH, B_HEADS, B_DK, B_DV), F32),
                   jax.ShapeDtypeStruct((DEC_BATCH, CONV_W - 1, rows, LANES), F32)],
        compiler_params=pltpu.CompilerParams(
            dimension_semantics=("arbitrary",), vmem_limit_bytes=VMEM_LIMIT),
        name="gdn_step",
    )(state_conv, x, cw, ab, alog_l, dtb_l, zb, gw, state_ssm)


def kernel(x_prompt, x_sample, cache_k, cache_v, state_conv, state_ssm, page_table, meta_tokens,
           norm_w, w_in, lambda_q1, lambda_k1, lambda_q2, lambda_k2, subln_w, conv_w, a_log,
           dt_bias, gdn_norm_w, w_pa, w_pb, w_o, final_norm_w):
    assert x_prompt.shape == (BATCH, SEQ, D_MODEL) and x_sample.shape == (DEC_BATCH, 1, D_MODEL)
    assert w_in.shape == (1, D_MODEL, D_IN) and page_table.shape == (DEC_BATCH, N_PAGES)
    w = w_in[0].T
    nw = norm_w
    lams = (lambda_q1, lambda_k1, lambda_q2, lambda_k2)
    fw = final_norm_w.reshape(1, D_MODEL)
    wpa, wpb, wo = w_pa[0].astype(BF16), w_pb[0].astype(BF16), w_o[0].astype(BF16)
    cw = conv_w[0]

    def lanes8(v, off):
        return jnp.zeros((1, LANES), F32).at[0, off:off + B_HEADS].set(v)

    alog_l, dtb_l = lanes8(a_log[0], 0), lanes8(dt_bias[0], 0)
    alog_c, dtb_c = alog_l[0, 0:2 * B_HEADS].reshape(-1, 1), dtb_l[0, 0:2 * B_HEADS].reshape(-1, 1)
    gvecs = (alog_l, dtb_l, alog_c, dtb_c)

    xa = jnp.concatenate([x_sample[:, 0, :], meta_tokens,
                          jnp.zeros((AUX_ROWS - DEC_BATCH - N_META, D_MODEL), F32)], axis=0)
    pos_a = jnp.concatenate([jnp.full((DEC_BATCH,), PAST_LEN), jnp.arange(N_META),
                             jnp.zeros((AUX_ROWS - DEC_BATCH - N_META,), jnp.int32)])
    h_aux, w_bf, w_tail = _aux_inproj(xa, nw, w, _rope_tables(pos_a))
    hs, hm = h_aux[0:DEC_BATCH], h_aux[DEC_BATCH:DEC_BATCH + N_META]

    tabs_p = _rope_tables(N_META + jnp.arange(SEQ))
    h_main, h_tail = _inproj(x_prompt.reshape(BATCH * SEQ, D_MODEL), nw, w_bf, w_tail, tabs_p)

    pad_rows = CHUNK - N_META
    x_meta = jnp.pad(hm[:, O_QKVB:O_ZB], ((pad_rows, 0), (0, 0)))
    ab_meta = jnp.pad(hm[:, O_A:O_GA], ((pad_rows, 0), (0, LANES - 2 * B_HEADS)))
    abt_meta = ab_meta[:, 0:2 * B_HEADS].T.reshape(1, 1, 2 * B_HEADS, CHUNK)
    _, s_meta = _gdn_chunks(
        x_meta, 0, ab_meta, 0, abt_meta, jnp.zeros((CHUNK, B_HEADS * B_DV), F32), 0, cw, gvecs,
        gdn_norm_w, jnp.zeros((8, B_QKV), F32), jnp.zeros((B_HEADS, B_DK, B_DV), F32), 1, 1, pad_rows)
    k_meta, v_meta = hm[:, O_KA:O_VA], hm[:, O_VA:O_ZA]

    n_chunk = SEQ // CHUNK
    part = B_QKV // 3
    abt = h_main[:, O_A:O_GA].reshape(BATCH, n_chunk, CHUNK, 2 * B_HEADS).transpose(0, 1, 3, 2)
    heads = lambda t: t.reshape(DEC_BATCH, A_HEADS, LANES)
    q_s = heads(hs[:, O_QA:O_KA]) * (A_DH ** -0.5)
    k_s, v_s = heads(hs[:, O_KA:O_VA]), heads(hs[:, O_VA:O_ZA])
    yb, ssm_p, ya_s = _gdn_decode(
        page_table, h_main, h_main, abt, cw, gvecs, gdn_norm_w, x_meta[CHUNK - 8:CHUNK], s_meta[0],
        lams, q_s, k_s, v_s, heads(hs[:, O_ZA:O_QKVB]), subln_w, cache_k, cache_v)
    ya = _attention(lams, h_main, k_meta, v_meta, subln_w)
    y_prompt = _merge(x_prompt.reshape(BATCH * SEQ, D_MODEL), ya, yb, h_main, h_tail,
                      wpa, wpb, wo, fw, 256)

    def with_meta(rows_meta, rows_tok):
        meta = jnp.broadcast_to(rows_meta.reshape(1, N_META, A_HEADS, LANES),
                                (BATCH, N_META, A_HEADS, LANES))
        return jnp.concatenate([meta, rows_tok.reshape(BATCH, SEQ, A_HEADS, LANES)], axis=1)[None]

    k_rows_p = with_meta(k_meta, h_main[:, O_KA:O_VA])
    v_rows_p = with_meta(v_meta, h_main[:, O_VA:O_ZA])
    conv_p = h_main.reshape(BATCH, SEQ, N_MAIN)[:, SEQ - (CONV_W - 1):, O_QKVB:O_ZB][None]

    rows = B_QKV // LANES
    ab_s = jnp.pad(hs[:, O_A:O_GA], ((0, 0), (0, LANES - 2 * B_HEADS))).reshape(DEC_BATCH, 1, LANES)
    yb_s, ssm_s, conv_s = _gdn_step(
        state_conv[0].reshape(DEC_BATCH, CONV_W - 1, rows, LANES),
        hs[:, O_QKVB:O_ZB].reshape(DEC_BATCH, rows, LANES), cw.reshape(CONV_W, rows, LANES),
        ab_s, alog_l, dtb_l, heads(hs[:, O_ZB:O_A]), gdn_norm_w, state_ssm[0])
    hs_tail = jnp.pad(hs[:, N_MAIN:], ((0, 0), (0, LANES - (D_IN - N_MAIN))))
    y_sample = _merge(x_sample[:, 0, :], ya_s.reshape(DEC_BATCH, -1).astype(BF16),
                      yb_s.reshape(DEC_BATCH, -1).astype(BF16), hs, hs_tail,
                      wpa, wpb, wo, fw, DEC_BATCH)

    return (y_prompt.reshape(BATCH, SEQ, D_MODEL), y_sample.reshape(DEC_BATCH, 1, D_MODEL),
            k_rows_p, v_rows_p, conv_p, ssm_p[None],
            k_s.reshape(1, DEC_BATCH, 1, A_HEADS, LANES), v_s.reshape(1, DEC_BATCH, 1, A_HEADS, LANES),
            conv_s.reshape(1, DEC_BATCH, CONV_W - 1, B_QKV), ssm_s[None])
```

```python
import functools
import math

import jax
import jax.numpy as jnp
from jax import lax
from jax.experimental import pallas as pl
from jax.experimental.pallas import tpu as pltpu

F32 = jnp.float32
BF16 = jnp.bfloat16

D_MODEL = 2048
BATCH = 4
SEQ = 2048
DEC_BATCH = 32
PAST_LEN = 8192
PAGE_SIZE = 128
N_PAGES = PAST_LEN // PAGE_SIZE
N_META = 16
A_HEADS = 8
A_DH = 64
A_DV = 128
ROPE_DIM = 16
ROPE_THETA = 500000.0
B_HEADS = 8
B_DK = 128
B_DV = 128
B_QKV = 3072
CONV_W = 4
CHUNK = 64
EPS = 1e-6
NEG = -1e30
LAMBDA_INIT = 0.8 - 0.6 * math.exp(-0.3 * 0)

O_QA, O_KA, O_VA, O_ZA, O_QKVB, O_ZB, O_A, O_GA, O_GB, D_IN = (
    0, 1024, 2048, 3072, 4096, 7168, 8192, 8208, 10256, 12304)
LANES = 128
N_MAIN = (D_IN // LANES) * LANES
GATE_SHIFT = O_GA % LANES
VMEM_LIMIT = 56 * 1024 * 1024


def _dot(a, b):
    return jnp.dot(a, b, preferred_element_type=F32)


def _dot_nt(a, b):
    return lax.dot_general(a, b, (((1,), (1,)), ((), ())), preferred_element_type=F32)


def _dot_f32(a, b):
    return jnp.dot(a, b, preferred_element_type=F32, precision=lax.Precision.HIGHEST)


def _sigmoid(x):
    return 1.0 / (1.0 + jnp.exp(-x))


def _silu(x):
    return x * _sigmoid(x)


def _softplus(x):
    return jnp.maximum(x, 0.0) + jnp.log1p(jnp.exp(-jnp.abs(x)))


def _rmsnorm(x, w):
    return x * lax.rsqrt(jnp.mean(x * x, axis=-1, keepdims=True) + EPS) * w


def _rope_tables(pos):
    r = pos.shape[0]
    inv_freq = ROPE_THETA ** (-jnp.arange(0, ROPE_DIM, 2, dtype=F32) / ROPE_DIM)
    ang = pos.astype(F32)[:, None] * inv_freq[None, :]
    cos, sin = jnp.cos(ang), jnp.sin(ang)
    half = ROPE_DIM // 2
    rest = A_DH - ROPE_DIM
    c = jnp.concatenate([cos, cos, jnp.ones((r, rest), F32)], axis=1)
    sa = jnp.concatenate([jnp.zeros((r, half), F32), sin, jnp.zeros((r, rest), F32)], axis=1)
    sb = jnp.concatenate([-sin, jnp.zeros((r, half + rest), F32)], axis=1)
    return tuple(jnp.tile(t, (1, LANES // A_DH)) for t in (c, sa, sb))


def _rope_tile(t, c, sa, sb):
    out = []
    for i in range(t.shape[1] // LANES):
        x = t[:, i * LANES:(i + 1) * LANES]
        out.append(x * c + pltpu.roll(x, ROPE_DIM // 2, 1) * sa
                   + pltpu.roll(x, LANES - ROPE_DIM // 2, 1) * sb)
    return jnp.concatenate(out, axis=1) if len(out) > 1 else out[0]


IP_TM = 1024
IP_TN = 1024
IP_ROWS = 256


def _inproj_kernel(x_ref, nw_ref, w_ref, wt_ref, c_ref, sa_ref, sb_ref, h_ref, ht_ref, xn_ref):
    j = pl.program_id(1)

    @pl.when(j == 0)
    def _():
        def body(r, carry):
            rows = pl.ds(pl.multiple_of(r * IP_ROWS, IP_ROWS), IP_ROWS)
            xn_ref[rows, :] = _rmsnorm(x_ref[rows, :], nw_ref[...]).astype(BF16)
            return carry
        lax.fori_loop(0, IP_TM // IP_ROWS, body, 0)
        ht_ref[...] = _dot_nt(xn_ref[...], wt_ref[...])

    h_ref[...] = _dot_nt(xn_ref[...], w_ref[...])

    @pl.when(j < O_VA // IP_TN)
    def _():
        def body(r, carry):
            rows = pl.ds(pl.multiple_of(r * IP_ROWS, IP_ROWS), IP_ROWS)
            h_ref[rows, :] = _rope_tile(h_ref[rows, :], c_ref[rows, :], sa_ref[rows, :], sb_ref[rows, :])
            return carry
        lax.fori_loop(0, IP_TM // IP_ROWS, body, 0)


def _inproj(x, nw, w_bf, w_tail, tabs):
    m = x.shape[0]
    per_seq = SEQ // IP_TM
    tab_spec = pl.BlockSpec((IP_TM, LANES), lambda i, j: (i % per_seq, 0))
    return pl.pallas_call(
        _inproj_kernel,
        grid=(m // IP_TM, N_MAIN // IP_TN),
        in_specs=[
            pl.BlockSpec((IP_TM, D_MODEL), lambda i, j: (i, 0)),
            pl.BlockSpec((1, D_MODEL), lambda i, j: (0, 0)),
            pl.BlockSpec((IP_TN, D_MODEL), lambda i, j: (j, 0)),
            pl.BlockSpec((LANES, D_MODEL), lambda i, j: (0, 0)),
            tab_spec, tab_spec, tab_spec,
        ],
        out_specs=[
            pl.BlockSpec((IP_TM, IP_TN), lambda i, j: (i, j)),
            pl.BlockSpec((IP_TM, LANES), lambda i, j: (i, 0)),
        ],
        out_shape=[jax.ShapeDtypeStruct((m, N_MAIN), F32),
                   jax.ShapeDtypeStruct((m, LANES), F32)],
        scratch_shapes=[pltpu.VMEM((IP_TM, D_MODEL), BF16)],
        compiler_params=pltpu.CompilerParams(
            dimension_semantics=("arbitrary", "arbitrary"), vmem_limit_bytes=VMEM_LIMIT),
        name="inproj",
    )(x, nw, w_bf, w_tail, *tabs)


AUX_ROWS = 64
AUX_TN = 512


def _aux_inproj_kernel(x_ref, nw_ref, w_ref, c_ref, sa_ref, sb_ref, o_ref, wbf_ref, wtail_ref, xs_ref):
    j = pl.program_id(0)

    @pl.when(j == 0)
    def _():
        xn = _rmsnorm(x_ref[...], nw_ref[...])
        hi = xn.astype(BF16)
        xs_ref[0:AUX_ROWS, :] = hi
        xs_ref[AUX_ROWS:2 * AUX_ROWS, :] = (xn - hi.astype(F32)).astype(BF16)

    w = w_ref[...]
    w_hi = w.astype(BF16)
    wbf_ref[...] = w_hi

    @pl.when(j == pl.num_programs(0) - 1)
    def _():
        row = lax.broadcasted_iota(jnp.int32, (LANES, D_MODEL), 0)
        wtail_ref[...] = jnp.where(row < D_IN - N_MAIN, w_hi[0:LANES, :], jnp.zeros((), BF16))

    w_lo = (w - w_hi.astype(F32)).astype(BF16)
    r1 = _dot_nt(xs_ref[...], w_hi)
    r2 = _dot_nt(xs_ref[0:AUX_ROWS, :], w_lo)
    acc = r1[0:AUX_ROWS] + (r1[AUX_ROWS:] + r2)
    is_rope = j < O_VA // AUX_TN

    @pl.when(is_rope)
    def _():
        o_ref[...] = _rope_tile(acc, c_ref[...], sa_ref[...], sb_ref[...])

    @pl.when(jnp.logical_not(is_rope))
    def _():
        o_ref[...] = acc


def _aux_inproj(xa, nw, w, tabs):
    tab_spec = pl.BlockSpec((AUX_ROWS, LANES), lambda j: (0, 0))
    return pl.pallas_call(
        _aux_inproj_kernel,
        grid=(pl.cdiv(D_IN, AUX_TN),),
        in_specs=[
            pl.BlockSpec((AUX_ROWS, D_MODEL), lambda j: (0, 0)),
            pl.BlockSpec((1, D_MODEL), lambda j: (0, 0)),
            pl.BlockSpec((AUX_TN, D_MODEL), lambda j: (j, 0)),
            tab_spec, tab_spec, tab_spec,
        ],
        out_specs=[pl.BlockSpec((AUX_ROWS, AUX_TN), lambda j: (0, j)),
                   pl.BlockSpec((AUX_TN, D_MODEL), lambda j: (j, 0)),
                   pl.BlockSpec((LANES, D_MODEL), lambda j: (0, 0))],
        out_shape=[jax.ShapeDtypeStruct((AUX_ROWS, D_IN), F32),
                   jax.ShapeDtypeStruct((D_IN, D_MODEL), BF16),
                   jax.ShapeDtypeStruct((LANES, D_MODEL), BF16)],
        scratch_shapes=[pltpu.VMEM((2 * AUX_ROWS, D_MODEL), BF16)],
        compiler_params=pltpu.CompilerParams(
            dimension_semantics=("arbitrary",), vmem_limit_bytes=VMEM_LIMIT),
        name="aux_inproj",
    )(xa, nw, w, *tabs)


AT_TQ = 512
AT_TK = 512
AT_HP = 2
LOG2E = math.log2(math.e)


def _diff_lambda(lq1, lk1, lq2, lk2):
    a = jnp.exp(jnp.sum(lq1 * lk1, axis=-1, keepdims=True))
    b = jnp.exp(jnp.sum(lq2 * lk2, axis=-1, keepdims=True))
    return a - b + LAMBDA_INIT


def _attn_kernel(lq1_ref, lk1_ref, lq2_ref, lk2_ref, q_ref, k_ref, v_ref, km_ref, vm_ref,
                 za_ref, sw_ref, o_ref, kb_ref, vb_ref, kmb_ref, vmb_ref, q2_ref, s_ref, sm_ref,
                 m_ref, l_ref, acc_ref):
    qi = pl.program_id(2)
    tq, tk = AT_TQ, AT_TK
    n_tiles = tk // LANES
    heads = range(AT_HP)
    hl = lambda h: slice(h * LANES, (h + 1) * LANES)

    @pl.when(qi == 0)
    def _():
        pad = jnp.zeros((LANES - N_META, LANES), BF16)
        for h in heads:
            kb_ref[h] = k_ref[:, hl(h)].astype(BF16)
            vb_ref[h] = v_ref[:, hl(h)].astype(BF16)
            kmb_ref[h] = jnp.concatenate([km_ref[:, hl(h)].astype(BF16), pad], axis=0)
            vmb_ref[h] = jnp.concatenate([vm_ref[:, hl(h)].astype(BF16), pad], axis=0)

    lane = lax.broadcasted_iota(jnp.int32, (tq, LANES), 1)
    for h in heads:
        q = q_ref[:, hl(h)] * (A_DH ** -0.5 * LOG2E)
        q2_ref[h, 0:tq, :] = jnp.where(lane < A_DH, q, 0.0).astype(BF16)
        q2_ref[h, tq:2 * tq, :] = jnp.where(lane >= A_DH, q, 0.0).astype(BF16)

    def tile_max(m, s):
        for c in range(s.shape[1] // LANES):
            m = jnp.maximum(m, s[:, c * LANES:(c + 1) * LANES])
        return m

    lane2 = lax.broadcasted_iota(jnp.int32, (2 * tq, LANES), 1)
    sm = [jnp.where(lane2 < N_META, _dot_nt(q2_ref[h], kmb_ref[h]), NEG) for h in heads]
    for h in heads:
        sm_ref[h] = sm[h]
        m_ref[h] = sm[h]

    def scores(j):
        rows = pl.ds(pl.multiple_of(j * tk, tk), tk)
        return [_dot_nt(q2_ref[h], kb_ref[h, rows, :]) for h in heads]

    def pass1(j, carry):
        s = scores(j)
        for h in heads:
            s_ref[h, j] = s[h]
            m_ref[h] = tile_max(m_ref[h], s[h])
        return carry
    lax.fori_loop(0, qi, pass1, 0)

    r = lax.broadcasted_iota(jnp.int32, (2 * tq, tk), 0)
    r = jnp.where(r >= tq, r - tq, r)
    c = lax.broadcasted_iota(jnp.int32, (2 * tq, tk), 1)
    causal = c <= r
    s = [jnp.where(causal, t, NEG) for t in scores(qi)]
    for h in heads:
        s_ref[h, qi] = s[h]
    m = [jnp.max(tile_max(m_ref[h], s[h]), axis=-1, keepdims=True) for h in heads]
    for h in heads:
        m_ref[h] = jnp.broadcast_to(m[h], (2 * tq, LANES))

    p = [jnp.exp2(sm_ref[h] - m_ref[h]) for h in heads]
    for h in heads:
        l_ref[h] = p[h]
        acc_ref[h] = _dot(p[h].astype(BF16), vmb_ref[h])

    def pass2(j, carry):
        rows = pl.ds(pl.multiple_of(j * tk, tk), tk)
        pb = []
        for h in heads:
            mb = m_ref[h]
            lsum = l_ref[h]
            ps = []
            for t in range(n_tiles):
                p = jnp.exp2(s_ref[h, j, :, t * LANES:(t + 1) * LANES] - mb)
                lsum = lsum + p
                ps.append(p.astype(BF16))
            l_ref[h] = lsum
            pb.append(jnp.concatenate(ps, axis=1))
        pv = [_dot(pb[h], vb_ref[h, rows, :]) for h in heads]
        for h in heads:
            acc_ref[h] += pv[h]
        return carry
    lax.fori_loop(0, qi + 1, pass2, 0)

    lam = _diff_lambda(lq1_ref[...], lk1_ref[...], lq2_ref[...], lk2_ref[...])
    for h in heads:
        l = jnp.sum(l_ref[h], axis=-1, keepdims=True)
        o = acc_ref[h, 0:tq, :] / l[0:tq] - lam * (acc_ref[h, tq:2 * tq, :] / l[tq:2 * tq])
        y = _rmsnorm(o, sw_ref[...]) * (1.0 - LAMBDA_INIT)
        o_ref[:, hl(h)] = (y * _silu(za_ref[:, hl(h)])).astype(BF16)


def _attention(lams, h_main, k_meta, v_meta, subln_w):
    nq = SEQ // AT_TQ
    w = AT_HP * LANES
    lam_spec = pl.BlockSpec((1, A_DH), lambda b, g, i: (0, 0))
    return pl.pallas_call(
        _attn_kernel,
        grid=(BATCH, A_HEADS // AT_HP, nq),
        in_specs=[
            lam_spec, lam_spec, lam_spec, lam_spec,
            pl.BlockSpec((AT_TQ, w), lambda b, g, i: (b * nq + i, O_QA // w + g)),
            pl.BlockSpec((SEQ, w), lambda b, g, i: (b, O_KA // w + g)),
            pl.BlockSpec((SEQ, w), lambda b, g, i: (b, O_VA // w + g)),
            pl.BlockSpec((N_META, w), lambda b, g, i: (0, g)),
            pl.BlockSpec((N_META, w), lambda b, g, i: (0, g)),
            pl.BlockSpec((AT_TQ, w), lambda b, g, i: (b * nq + i, O_ZA // w + g)),
            pl.BlockSpec((1, A_DV), lambda b, g, i: (0, 0)),
        ],
        out_specs=pl.BlockSpec((AT_TQ, w), lambda b, g, i: (b * nq + i, g)),
        out_shape=jax.ShapeDtypeStruct((BATCH * SEQ, A_HEADS * A_DV), BF16),
        scratch_shapes=[
            pltpu.VMEM((AT_HP, SEQ, LANES), BF16), pltpu.VMEM((AT_HP, SEQ, LANES), BF16),
            pltpu.VMEM((AT_HP, LANES, LANES), BF16), pltpu.VMEM((AT_HP, LANES, LANES), BF16),
            pltpu.VMEM((AT_HP, 2 * AT_TQ, LANES), BF16),
            pltpu.VMEM((AT_HP, SEQ // AT_TK, 2 * AT_TQ, AT_TK), F32),
            pltpu.VMEM((AT_HP, 2 * AT_TQ, LANES), F32),
            pltpu.VMEM((AT_HP, 2 * AT_TQ, LANES), F32), pltpu.VMEM((AT_HP, 2 * AT_TQ, LANES), F32),
            pltpu.VMEM((AT_HP, 2 * AT_TQ, A_DV), F32),
        ],
        compiler_params=pltpu.CompilerParams(
            dimension_semantics=("arbitrary", "arbitrary", "arbitrary"),
            vmem_limit_bytes=VMEM_LIMIT),
        name="diff_attn",
    )(*lams, h_main, h_main, h_main, k_meta, v_meta, h_main, subln_w)


def _unit_lower_inverse_minus_eye(a_list, i_idx, j_idx):
    base = 8
    diag = (i_idx // base) == (j_idx // base)
    b = [jnp.where(diag, a, 0.0) for a in a_list]
    n = [-x for x in b]
    for _ in range(2):
        bb = [x.astype(BF16) for x in b]
        b = [_dot(x, x) for x in bb]
        nb = [_dot(x.astype(BF16), y.astype(BF16)) for x, y in zip(n, b)]
        n = [x + y + z for x, y, z in zip(n, b, nb)]
    s = base
    while s < CHUNK:
        join = ((i_idx // (2 * s)) == (j_idx // (2 * s))) & ((i_idx // s) % 2 == 1) & ((j_idx // s) % 2 == 0)
        a_s = [jnp.where(join, a, 0.0) for a in a_list]
        x = [p + _dot(q.astype(BF16), p.astype(BF16)) for p, q in zip(a_s, n)]
        xn = [_dot(p.astype(BF16), q.astype(BF16)) for p, q in zip(x, n)]
        n = [q - (p + r) for q, p, r in zip(n, x, xn)]
        s *= 2
    return n


def _gdn_kernel(xq_ref, xk_ref, xv_ref, ab_ref, abt_ref, zb_ref, cw_ref, alog_ref, dtb_ref,
                alogc_ref, dtbc_ref, gw_ref, halo_ref, s0_ref, yb_ref, sfin_ref, ext_ref, s_ref,
                *, masked_rows):
    c = pl.program_id(1)
    hist = 8

    @pl.when(c == 0)
    def _():
        ext_ref[0:hist, :] = halo_ref[...]
        s_ref[...] = s0_ref[...]

    x = jnp.concatenate([xq_ref[...], xk_ref[...], xv_ref[...]], axis=1)
    ext_ref[hist:hist + CHUNK, :] = x
    y = cw_ref[CONV_W - 1:CONV_W, :] * x
    for t in range(CONV_W - 1):
        lo = hist - (CONV_W - 1) + t
        y = y + cw_ref[t:t + 1, :] * ext_ref[lo:lo + CHUNK, :]
    y = _silu(y)
    ext_ref[0:hist, :] = x[CHUNK - hist:CHUNK, :]

    i_idx = lax.broadcasted_iota(jnp.int32, (CHUNK, CHUNK), 0)
    j_idx = lax.broadcasted_iota(jnp.int32, (CHUNK, CHUNK), 1)
    tril = i_idx >= j_idx
    strict = i_idx > j_idx
    tril_f = tril.astype(F32)
    triu_f = (i_idx <= j_idx).astype(F32)

    ab = ab_ref[...]
    g_c = -jnp.exp(alog_ref[...]) * _softplus(ab + dtb_ref[...])
    beta_c = _sigmoid(ab)
    abt = abt_ref[...]
    g_r = -jnp.exp(alogc_ref[...]) * _softplus(abt + dtbc_ref[...])
    if masked_rows:
        row_ok = lax.broadcasted_iota(jnp.int32, (CHUNK, LANES), 0) >= masked_rows
        col_ok = lax.broadcasted_iota(jnp.int32, (2 * B_HEADS, CHUNK), 1) >= masked_rows
        g_c = jnp.where(row_ok, g_c, 0.0)
        beta_c = jnp.where(row_ok, beta_c, 0.0)
        g_r = jnp.where(col_ok, g_r, 0.0)
    gc_c = _dot_f32(tril_f, g_c)
    gc_r = _dot_f32(g_r, triu_f)

    nk = B_HEADS * B_DK
    heads = range(B_HEADS)

    def l2n(t):
        return t * lax.rsqrt(jnp.sum(t * t, axis=-1, keepdims=True) + EPS)

    qn = [l2n(y[:, h * B_DK:(h + 1) * B_DK]) * (B_DK ** -0.5) for h in heads]
    kn = [l2n(y[:, nk + h * B_DK:nk + (h + 1) * B_DK]) for h in heads]
    vh = [y[:, 2 * nk + h * B_DV:2 * nk + (h + 1) * B_DV] for h in heads]
    bcol = [beta_c[:, B_HEADS + h:B_HEADS + h + 1] for h in heads]
    gcc = [gc_c[:, h:h + 1] for h in heads]
    decay = [jnp.where(tril, jnp.exp(jnp.where(tril, gcc[h] - gc_r[h:h + 1, :], 0.0)), 0.0)
             for h in heads]
    kbeta = [kn[h] * bcol[h] for h in heads]
    kn_b = [t.astype(BF16) for t in kn]
    kk = [_dot_nt(kbeta[h].astype(BF16), kn_b[h]) for h in heads]
    qk = [_dot_nt(qn[h].astype(BF16), kn_b[h]) for h in heads]
    a = [jnp.where(strict, kk[h] * decay[h], 0.0) for h in heads]
    n = _unit_lower_inverse_minus_eye(a, i_idx, j_idx)
    egc = [jnp.exp(t) for t in gcc]
    rhs = [jnp.concatenate([vh[h] * bcol[h], kbeta[h] * egc[h]], axis=1) for h in heads]
    nr = [_dot(n[h].astype(BF16), rhs[h].astype(BF16)) for h in heads]
    sol = [rhs[h] + nr[h] for h in heads]
    st = [s_ref[h] for h in heads]
    st_b = [t.astype(BF16) for t in st]
    ws = [_dot(sol[h][:, B_DV:B_DV + B_DK].astype(BF16), st_b[h]) for h in heads]
    qs = [_dot((qn[h] * egc[h]).astype(BF16), st_b[h]) for h in heads]
    v_new_b = [(sol[h][:, 0:B_DV] - ws[h]).astype(BF16) for h in heads]
    av = [_dot((qk[h] * decay[h]).astype(BF16), v_new_b[h]) for h in heads]
    g_last = [t[CHUNK - 1:CHUNK, :] for t in gcc]
    ke_t = [(kn[h] * jnp.exp(g_last[h] - gcc[h])).T.astype(BF16) for h in heads]
    kv = [_dot(ke_t[h], v_new_b[h]) for h in heads]
    for h in heads:
        s_ref[h] = st[h] * jnp.exp(g_last[h]) + kv[h]
    for h in heads:
        zb = zb_ref[:, h * B_DV:(h + 1) * B_DV]
        yb_ref[:, h * B_DV:(h + 1) * B_DV] = (_rmsnorm(qs[h] + av[h], gw_ref[...]) * _silu(zb)).astype(BF16)

    @pl.when(c == pl.num_programs(1) - 1)
    def _():
        sfin_ref[...] = s_ref[...]


def _gdn_chunks(x, x_col, ab, ab_col, abt, zsrc, zb_col, cw, gvecs, gw, halo, s0, n_seq, n_chunk,
                masked_rows):
    alog_l, dtb_l, alog_c, dtb_c = gvecs
    const2 = lambda b, c: (0, 0)
    kern = functools.partial(_gdn_kernel, masked_rows=masked_rows)
    part = B_QKV // 3

    def x_spec(k):
        return pl.BlockSpec((CHUNK, part), lambda b, c: (b * n_chunk + c, x_col + k))

    return pl.pallas_call(
        kern,
        grid=(n_seq, n_chunk),
        in_specs=[
            x_spec(0), x_spec(1), x_spec(2),
            pl.BlockSpec((CHUNK, LANES), lambda b, c: (b * n_chunk + c, ab_col)),
            pl.BlockSpec((None, None, 2 * B_HEADS, CHUNK), lambda b, c: (b, c, 0, 0)),
            pl.BlockSpec((CHUNK, B_HEADS * B_DV), lambda b, c: (b * n_chunk + c, zb_col)),
            pl.BlockSpec((CONV_W, B_QKV), const2),
            pl.BlockSpec((1, LANES), const2), pl.BlockSpec((1, LANES), const2),
            pl.BlockSpec((2 * B_HEADS, 1), const2), pl.BlockSpec((2 * B_HEADS, 1), const2),
            pl.BlockSpec((1, B_DV), const2),
            pl.BlockSpec((8, B_QKV), const2),
            pl.BlockSpec((B_HEADS, B_DK, B_DV), lambda b, c: (0, 0, 0)),
        ],
        out_specs=[
            pl.BlockSpec((CHUNK, B_HEADS * B_DV), lambda b, c: (b * n_chunk + c, 0)),
            pl.BlockSpec((None, B_HEADS, B_DK, B_DV), lambda b, c: (b, 0, 0, 0)),
        ],
        out_shape=[jax.ShapeDtypeStruct((n_seq * n_chunk * CHUNK, B_HEADS * B_DV), BF16),
                   jax.ShapeDtypeStruct((n_seq, B_HEADS, B_DK, B_DV), F32)],
        scratch_shapes=[pltpu.VMEM((8 + CHUNK, B_QKV), F32),
                        pltpu.VMEM((B_HEADS, B_DK, B_DV), F32)],
        compiler_params=pltpu.CompilerParams(
            dimension_semantics=("arbitrary", "arbitrary"), vmem_limit_bytes=VMEM_LIMIT),
        name="gdn_chunks",
    )(x, x, x, ab, abt, zsrc, cw, alog_l, dtb_l, alog_c, dtb_c, gw, halo, s0)


def _shift_lanes(x, k):
    nblk = x.shape[1] // LANES
    r = [pltpu.roll(x[:, c * LANES:(c + 1) * LANES], LANES - k, 1) for c in range(nblk)]
    lane = lax.broadcasted_iota(jnp.int32, (x.shape[0], LANES), 1)
    return jnp.concatenate([jnp.where(lane < LANES - k, r[c], r[c + 1]) for c in range(nblk - 1)],
                           axis=1)


def _merge_kernel(x_ref, ya_ref, yb_ref, g4_ref, g5_ref, gt_ref, wpa_ref, wpb_ref, wo_ref, fw_ref,
                  y_ref):
    g5 = g5_ref[...]
    ga = _shift_lanes(jnp.concatenate([g4_ref[...], g5[:, 0:LANES]], axis=1), GATE_SHIFT)
    gb = _shift_lanes(jnp.concatenate([g5, gt_ref[...]], axis=1), GATE_SHIFT)
    pa = _dot(ya_ref[...], wpa_ref[...])
    pb = _dot(yb_ref[...], wpb_ref[...])
    mixed = _sigmoid(ga) * pa + _sigmoid(gb) * pb
    hp = x_ref[...] + _dot(mixed.astype(BF16), wo_ref[...])
    y_ref[...] = _rmsnorm(hp, fw_ref[...])


def _merge(x, ya, yb, hsrc, htail, wpa, wpb, wo, fw, tm):
    m = x.shape[0]
    once = pl.Buffered(1)
    return pl.pallas_call(
        _merge_kernel,
        grid=(m // tm,),
        in_specs=[
            pl.BlockSpec((tm, D_MODEL), lambda i: (i, 0)),
            pl.BlockSpec((tm, A_HEADS * A_DV), lambda i: (i, 0)),
            pl.BlockSpec((tm, B_HEADS * B_DV), lambda i: (i, 0)),
            pl.BlockSpec((tm, D_MODEL), lambda i: (i, O_A // D_MODEL)),
            pl.BlockSpec((tm, D_MODEL), lambda i: (i, O_A // D_MODEL + 1)),
            pl.BlockSpec((tm, LANES), lambda i: (i, 0)),
            pl.BlockSpec((A_HEADS * A_DV, D_MODEL), lambda i: (0, 0), pipeline_mode=once),
            pl.BlockSpec((B_HEADS * B_DV, D_MODEL), lambda i: (0, 0), pipeline_mode=once),
            pl.BlockSpec((D_MODEL, D_MODEL), lambda i: (0, 0), pipeline_mode=once),
            pl.BlockSpec((1, D_MODEL), lambda i: (0, 0)),
        ],
        out_specs=pl.BlockSpec((tm, D_MODEL), lambda i: (i, 0)),
        out_shape=jax.ShapeDtypeStruct((m, D_MODEL), F32),
        compiler_params=pltpu.CompilerParams(
            dimension_semantics=("arbitrary",), vmem_limit_bytes=VMEM_LIMIT),
        name="merge",
    )(x, ya, yb, hsrc, hsrc, htail, wpa, wpb, wo, fw)


DA_GRP = 8
HALF_TOK = PAGE_SIZE // 2
HALF_ROWS = HALF_TOK * A_HEADS
TOK_TILES = HALF_ROWS // LANES


def _lane_group_reduce(x, op):
    s = A_HEADS
    while s < LANES:
        x = op(x, pltpu.roll(x, s, 1))
        s *= 2
    return x


FD_PAGES = N_PAGES // 2
FD_STEPS = 4
FD_NBUF = 2 * FD_PAGES


def _gdn_decode_kernel(pt_ref, xq_ref, xk_ref, xv_ref, ab_ref, abt_ref, zb_ref, cw_ref, alog_ref,
                       dtb_ref, alogc_ref, dtbc_ref, gw_ref, halo_ref, s0_ref,
                       lq1_ref, lk1_ref, lq2_ref, lk2_ref, q_ref, kn_ref, vn_ref, za_ref, sw_ref,
                       ck_ref, cv_ref, yb_ref, sfin_ref, o_ref,
                       ext_ref, s_ref, buf_ref, sem_ref, sc_ref, acc_ref, wn_ref):
    n_chunk = pl.num_programs(1)
    t = pl.program_id(0) * n_chunk + pl.program_id(1)
    n_steps = pl.num_programs(0) * n_chunk
    role = t % FD_STEPS

    def start_step(step):
        seq = step // FD_STEPS
        r = step % FD_STEPS
        half = (step % 2) * FD_PAGES
        first = (r % 2) * FD_PAGES

        def copies(src_ref):
            for s in range(FD_PAGES):
                pltpu.make_async_copy(src_ref.at[0, pt_ref[seq, first + s]], buf_ref.at[half + s],
                                      sem_ref.at[half + s]).start(priority=s % 2)

        @pl.when(r < 2)
        def _():
            copies(ck_ref)

        @pl.when(r >= 2)
        def _():
            copies(cv_ref)

    def wait_slot(slot):
        pltpu.make_async_copy(ck_ref.at[0, 0], buf_ref.at[slot], sem_ref.at[slot]).wait()

    @pl.when(t == 0)
    def _():
        start_step(0)

    @pl.when(t + 1 < n_steps)
    def _():
        start_step(t + 1)

    _gdn_kernel(xq_ref, xk_ref, xv_ref, ab_ref, abt_ref, zb_ref, cw_ref, alog_ref, dtb_ref,
                alogc_ref, dtbc_ref, gw_ref, halo_ref, s0_ref, yb_ref, sfin_ref, ext_ref, s_ref,
                masked_rows=0)

    half = (t % 2) * FD_PAGES
    sub = lax.broadcasted_iota(jnp.int32, (A_HEADS, LANES), 0)
    lane = lax.broadcasted_iota(jnp.int32, (A_HEADS, LANES), 1)
    hmask = (lane % A_HEADS) == sub
    first_half = sub < 2
    second_half = (sub >= 2) & (sub < 4)
    map0 = (sub % 2) == 0
    sub2 = lax.broadcasted_iota(jnp.int32, (A_HEADS, 2 * LANES), 0)
    lane2 = lax.broadcasted_iota(jnp.int32, (A_HEADS, 2 * LANES), 1)
    half_sel = ((sub2 < 4) & (lane2 // A_DH == sub2)).astype(BF16)
    n_grp = FD_PAGES // DA_GRP

    def halves_on_lanes(page):
        return jnp.concatenate([page[0:HALF_TOK].reshape(HALF_ROWS, LANES),
                                page[HALF_TOK:].reshape(HALF_ROWS, LANES)], axis=1)

    def pair_halves(x, op):
        up = pltpu.roll(x, 2, 0)
        dn = pltpu.roll(x, A_HEADS - 2, 0)
        return jnp.where(first_half, op(x, dn), jnp.where(second_half, op(x, up), 0.0))

    @pl.when(role < 2)
    def _():
        q = q_ref[...]

        def k_group(g, carry):
            slots = [half + g * DA_GRP + i for i in range(DA_GRP)]
            for s in slots:
                wait_slot(s)
            prods = [halves_on_lanes(buf_ref[s] * q[None]).astype(BF16) for s in slots]
            scs = [_dot_nt(half_sel, p) for p in prods]
            for i, sc in enumerate(scs):
                base = (role * FD_PAGES + g * DA_GRP + i) * TOK_TILES
                for c in range(TOK_TILES):
                    sc_ref[base + c] = sc[:, c * LANES:(c + 1) * LANES]
            return carry
        lax.fori_loop(0, n_grp, k_group, 0)

    @pl.when(role == 1)
    def _():
        lam = _diff_lambda(lq1_ref[...], lk1_ref[...], lq2_ref[...], lk2_ref[...])
        prod = q_ref[...] * kn_ref[...]
        hs1 = jnp.sum(jnp.where(lane < A_DH, prod, 0.0), axis=1, keepdims=True)
        hs2 = jnp.sum(jnp.where(lane >= A_DH, prod, 0.0), axis=1, keepdims=True)
        row1 = jnp.sum(jnp.where(hmask, hs1, 0.0), axis=0, keepdims=True)
        row2 = jnp.sum(jnp.where(hmask, hs2, 0.0), axis=0, keepdims=True)
        s_new = jnp.where(sub < 4, jnp.where(map0, row1, row2), 0.0)
        sc = sc_ref[...]
        mx = pair_halves(_lane_group_reduce(jnp.max(sc, axis=0), jnp.maximum), jnp.maximum)
        mx = jnp.maximum(mx, s_new)
        p = jnp.exp(sc - mx[None])
        p_new = jnp.exp(s_new - mx)
        den = pair_halves(_lane_group_reduce(jnp.sum(p, axis=0), jnp.add), jnp.add) + p_new
        coef = jnp.where(sub < 4, jnp.where(map0, 1.0 / den, -lam / den), 0.0)
        sc_ref[...] = p * coef[None]
        wn_ref[...] = jnp.sum(jnp.where(first_half, p_new * coef, 0.0), axis=0, keepdims=True)
        acc_ref[...] = jnp.zeros_like(acc_ref)

    @pl.when(role >= 2)
    def _():
        def page_weights(page):
            wa, wb = [], []
            for c in range(TOK_TILES):
                tile = sc_ref[page * TOK_TILES + c]
                ra = jnp.sum(jnp.where(first_half, tile, 0.0), axis=0, keepdims=True)
                rb = jnp.sum(jnp.where(second_half, tile, 0.0), axis=0, keepdims=True)
                wa.append(jnp.where(hmask, ra, 0.0))
                wb.append(jnp.where(hmask, rb, 0.0))
            w = jnp.concatenate([jnp.concatenate(wa, axis=1), jnp.concatenate(wb, axis=1)], axis=0)
            w_hi = w.astype(BF16)
            w_lo = (w - w_hi.astype(F32)).astype(BF16)
            return jnp.concatenate([w_hi, w_lo], axis=0)

        def v_group(g, acc):
            first = (role - 2) * FD_PAGES + g * DA_GRP
            ws = [page_weights(first + i) for i in range(DA_GRP)]
            slots = [half + g * DA_GRP + i for i in range(DA_GRP)]
            for s in slots:
                wait_slot(s)
            vs = [halves_on_lanes(buf_ref[s]).astype(BF16) for s in slots]
            rs = [_dot(w, v) for w, v in zip(ws, vs)]
            for r in rs:
                acc = acc + ((r[0:8, 0:LANES] + r[8:16, LANES:]) + (r[16:24, 0:LANES] + r[24:32, LANES:]))
            return acc
        acc_ref[...] = lax.fori_loop(0, n_grp, v_group, acc_ref[...])

    @pl.when(role == FD_STEPS - 1)
    def _():
        w_new = jnp.sum(jnp.where(hmask, wn_ref[...], 0.0), axis=1, keepdims=True) * (A_HEADS / LANES)
        o = acc_ref[...] + w_new * vn_ref[...]
        y = _rmsnorm(o, sw_ref[...]) * (1.0 - LAMBDA_INIT)
        o_ref[...] = y * _silu(za_ref[...])


def _gdn_decode(page_table, x, ab_src, abt, cw, gvecs, gw, halo, s0,
                lams, q_s, k_new, v_new, za_s, subln_w, cache_k, cache_v):
    alog_l, dtb_l, alog_c, dtb_c = gvecs
    n_chunk = SEQ // CHUNK
    assert BATCH * n_chunk == DEC_BATCH * FD_STEPS
    part = B_QKV // 3
    row = lambda b, c, pt: (b * n_chunk + c, 0)
    const2 = lambda b, c, pt: (0, 0)
    seq3 = lambda b, c, pt: ((b * n_chunk + c) // FD_STEPS, 0, 0)

    def col(width, k):
        return pl.BlockSpec((CHUNK, width), lambda b, c, pt: (b * n_chunk + c, k))

    lam_spec = pl.BlockSpec((1, A_DH), const2)
    seq_spec = pl.BlockSpec((None, A_HEADS, LANES), seq3)
    hbm_spec = pl.BlockSpec(memory_space=pl.ANY)
    grid_spec = pltpu.PrefetchScalarGridSpec(
        num_scalar_prefetch=1,
        grid=(BATCH, n_chunk),
        in_specs=[
            col(part, O_QKVB // part), col(part, O_QKVB // part + 1), col(part, O_QKVB // part + 2),
            col(LANES, O_A // LANES),
            pl.BlockSpec((None, None, 2 * B_HEADS, CHUNK), lambda b, c, pt: (b, c, 0, 0)),
            col(part, O_ZB // part),
            pl.BlockSpec((CONV_W, B_QKV), const2),
            pl.BlockSpec((1, LANES), const2), pl.BlockSpec((1, LANES), const2),
            pl.BlockSpec((2 * B_HEADS, 1), const2), pl.BlockSpec((2 * B_HEADS, 1), const2),
            pl.BlockSpec((1, B_DV), const2),
            pl.BlockSpec((8, B_QKV), const2),
            pl.BlockSpec((B_HEADS, B_DK, B_DV), lambda b, c, pt: (0, 0, 0)),
            lam_spec, lam_spec, lam_spec, lam_spec, seq_spec, seq_spec, seq_spec, seq_spec,
            pl.BlockSpec((1, A_DV), const2), hbm_spec, hbm_spec,
        ],
        out_specs=[
            pl.BlockSpec((CHUNK, B_HEADS * B_DV), row),
            pl.BlockSpec((None, B_HEADS, B_DK, B_DV), lambda b, c, pt: (b, 0, 0, 0)),
            seq_spec,
        ],
        scratch_shapes=[
            pltpu.VMEM((8 + CHUNK, B_QKV), F32),
            pltpu.VMEM((B_HEADS, B_DK, B_DV), F32),
            pltpu.VMEM((FD_NBUF, PAGE_SIZE, A_HEADS, LANES), F32),
            pltpu.SemaphoreType.DMA((FD_NBUF,)),
            pltpu.VMEM((N_PAGES * TOK_TILES, A_HEADS, LANES), F32),
            pltpu.VMEM((A_HEADS, A_DV), F32),
            pltpu.VMEM((1, LANES), F32),
        ],
    )
    return pl.pallas_call(
        _gdn_decode_kernel,
        grid_spec=grid_spec,
        out_shape=[jax.ShapeDtypeStruct((BATCH * SEQ, B_HEADS * B_DV), BF16),
                   jax.ShapeDtypeStruct((BATCH, B_HEADS, B_DK, B_DV), F32),
                   jax.ShapeDtypeStruct((DEC_BATCH, A_HEADS, A_DV), F32)],
        compiler_params=pltpu.CompilerParams(
            dimension_semantics=("arbitrary", "arbitrary"), vmem_limit_bytes=VMEM_LIMIT),
        name="gdn_decode",
    )(page_table, x, x, x, ab_src, abt, x, cw, alog_l, dtb_l, alog_c, dtb_c, gw, halo, s0,
      *lams, q_s, k_new, v_new, za_s, subln_w, cache_k, cache_v)


GS_SEQS = 4


def _gdn_step_kernel(sc_ref, x_ref, cw_ref, ab_ref, alog_ref, dtb_ref, zb_ref, gw_ref, s_ref,
                     yb_ref, sout_ref, cout_ref):
    for i in range(GS_SEQS):
        x = x_ref[i]
        y = cw_ref[CONV_W - 1] * x
        for t in range(CONV_W - 1):
            y = y + cw_ref[t] * sc_ref[i, t]
        y = _silu(y)
        for t in range(CONV_W - 2):
            cout_ref[i, t] = sc_ref[i, t + 1]
        cout_ref[i, CONV_W - 2] = x

        q = y[0:B_HEADS]
        k = y[B_HEADS:2 * B_HEADS]
        v = y[2 * B_HEADS:3 * B_HEADS]
        qn = q * lax.rsqrt(jnp.sum(q * q, axis=-1, keepdims=True) + EPS) * (B_DK ** -0.5)
        kn = k * lax.rsqrt(jnp.sum(k * k, axis=-1, keepdims=True) + EPS)
        qt = qn.T
        kt = kn.T
        ab = ab_ref[i]
        g = -jnp.exp(alog_ref[...]) * _softplus(ab + dtb_ref[...])
        beta = _sigmoid(ab)
        for h in range(B_HEADS):
            kcol = kt[:, h:h + 1]
            qcol = qt[:, h:h + 1]
            st = s_ref[i, h] * jnp.exp(g[:, h:h + 1])
            kv = jnp.sum(st * kcol, axis=0, keepdims=True)
            d = (v[h:h + 1] - kv) * beta[:, B_HEADS + h:B_HEADS + h + 1]
            st = st + kcol * d
            sout_ref[i, h] = st
            o = jnp.sum(st * qcol, axis=0, keepdims=True)
            yb_ref[i, h:h + 1, :] = _rmsnorm(o, gw_ref[...]) * _silu(zb_ref[i, h:h + 1, :])


def _gdn_step(state_conv, x, cw, ab, alog_l, dtb_l, zb, gw, state_ssm):
    rows = B_QKV // LANES
    seq3 = lambda b: (b, 0, 0)
    seq4 = lambda b: (b, 0, 0, 0)
    const2 = lambda b: (0, 0)
    return pl.pallas_call(
        _gdn_step_kernel,
        grid=(DEC_BATCH // GS_SEQS,),
        in_specs=[
            pl.BlockSpec((GS_SEQS, CONV_W - 1, rows, LANES), seq4),
            pl.BlockSpec((GS_SEQS, rows, LANES), seq3),
            pl.BlockSpec((CONV_W, rows, LANES), lambda b: (0, 0, 0)),
            pl.BlockSpec((GS_SEQS, 1, LANES), seq3),
            pl.BlockSpec((1, LANES), const2), pl.BlockSpec((1, LANES), const2),
            pl.BlockSpec((GS_SEQS, B_HEADS, B_DV), seq3),
            pl.BlockSpec((1, B_DV), const2),
            pl.BlockSpec((GS_SEQS, B_HEADS, B_DK, B_DV), seq4),
        ],
        out_specs=[
            pl.BlockSpec((GS_SEQS, B_HEADS, B_DV), seq3),
            pl.BlockSpec((GS_SEQS, B_HEADS, B_DK, B_DV), seq4),
            pl.BlockSpec((GS_SEQS, CONV_W - 1, rows, LANES), seq4),
        ],
        out_shape=[jax.ShapeDtypeStruct((DEC_BATCH, B_HEADS, B_DV), F32),
                   jax.ShapeDtypeStruct((DEC_BATCH, B_HEADS, B_DK, B_DV), F32),
                   jax.ShapeDtypeStruct((DEC_BATCH, CONV_W - 1, rows, LANES), F32)],
        compiler_params=pltpu.CompilerParams(
            dimension_semantics=("arbitrary",), vmem_limit_bytes=VMEM_LIMIT),
        name="gdn_step",
    )(state_conv, x, cw, ab, alog_l, dtb_l, zb, gw, state_ssm)


def kernel(x_prompt, x_sample, cache_k, cache_v, state_conv, state_ssm, page_table, meta_tokens,
           norm_w, w_in, lambda_q1, lambda_k1, lambda_q2, lambda_k2, subln_w, conv_w, a_log,
           dt_bias, gdn_norm_w, w_pa, w_pb, w_o, final_norm_w):
    assert x_prompt.shape == (BATCH, SEQ, D_MODEL) and x_sample.shape == (DEC_BATCH, 1, D_MODEL)
    assert w_in.shape == (1, D_MODEL, D_IN) and page_table.shape == (DEC_BATCH, N_PAGES)
    w = w_in[0].T
    nw = norm_w
    lams = (lambda_q1, lambda_k1, lambda_q2, lambda_k2)
    fw = final_norm_w.reshape(1, D_MODEL)
    wpa, wpb, wo = w_pa[0].astype(BF16), w_pb[0].astype(BF16), w_o[0].astype(BF16)
    cw = conv_w[0]

    def lanes8(v, off):
        return jnp.zeros((1, LANES), F32).at[0, off:off + B_HEADS].set(v)

    alog_l, dtb_l = lanes8(a_log[0], 0), lanes8(dt_bias[0], 0)
    alog_c, dtb_c = alog_l[0, 0:2 * B_HEADS].reshape(-1, 1), dtb_l[0, 0:2 * B_HEADS].reshape(-1, 1)
    gvecs = (alog_l, dtb_l, alog_c, dtb_c)

    xa = jnp.concatenate([x_sample[:, 0, :], meta_tokens,
                          jnp.zeros((AUX_ROWS - DEC_BATCH - N_META, D_MODEL), F32)], axis=0)
    pos_a = jnp.concatenate([jnp.full((DEC_BATCH,), PAST_LEN), jnp.arange(N_META),
                             jnp.zeros((AUX_ROWS - DEC_BATCH - N_META,), jnp.int32)])
    h_aux, w_bf, w_tail = _aux_inproj(xa, nw, w, _rope_tables(pos_a))
    hs, hm = h_aux[0:DEC_BATCH], h_aux[DEC_BATCH:DEC_BATCH + N_META]

    tabs_p = _rope_tables(N_META + jnp.arange(SEQ))
    h_main, h_tail = _inproj(x_prompt.reshape(BATCH * SEQ, D_MODEL), nw, w_bf, w_tail, tabs_p)

    pad_rows = CHUNK - N_META
    x_meta = jnp.pad(hm[:, O_QKVB:O_ZB], ((pad_rows, 0), (0, 0)))
    ab_meta = jnp.pad(hm[:, O_A:O_GA], ((pad_rows, 0), (0, LANES - 2 * B_HEADS)))
    abt_meta = ab_meta[:, 0:2 * B_HEADS].T.reshape(1, 1, 2 * B_HEADS, CHUNK)
    _, s_meta = _gdn_chunks(
        x_meta, 0, ab_meta, 0, abt_meta, jnp.zeros((CHUNK, B_HEADS * B_DV), F32), 0, cw, gvecs,
        gdn_norm_w, jnp.zeros((8, B_QKV), F32), jnp.zeros((B_HEADS, B_DK, B_DV), F32), 1, 1, pad_rows)
    k_meta, v_meta = hm[:, O_KA:O_VA], hm[:, O_VA:O_ZA]

    n_chunk = SEQ // CHUNK
    part = B_QKV // 3
    abt = h_main[:, O_A:O_GA].reshape(BATCH, n_chunk, CHUNK, 2 * B_HEADS).transpose(0, 1, 3, 2)
    heads = lambda t: t.reshape(DEC_BATCH, A_HEADS, LANES)
    q_s = heads(hs[:, O_QA:O_KA]) * (A_DH ** -0.5)
    k_s, v_s = heads(hs[:, O_KA:O_VA]), heads(hs[:, O_VA:O_ZA])
    yb, ssm_p, ya_s = _gdn_decode(
        page_table, h_main, h_main, abt, cw, gvecs, gdn_norm_w, x_meta[CHUNK - 8:CHUNK], s_meta[0],
        lams, q_s, k_s, v_s, heads(hs[:, O_ZA:O_QKVB]), subln_w, cache_k, cache_v)
    ya = _attention(lams, h_main, k_meta, v_meta, subln_w)
    y_prompt = _merge(x_prompt.reshape(BATCH * SEQ, D_MODEL), ya, yb, h_main, h_tail,
                      wpa, wpb, wo, fw, 256)

    def with_meta(rows_meta, rows_tok):
        meta = jnp.broadcast_to(rows_meta.reshape(1, N_META, A_HEADS, LANES),
                                (BATCH, N_META, A_HEADS, LANES))
        return jnp.concatenate([meta, rows_tok.reshape(BATCH, SEQ, A_HEADS, LANES)], axis=1)[None]

    k_rows_p = with_meta(k_meta, h_main[:, O_KA:O_VA])
    v_rows_p = with_meta(v_meta, h_main[:, O_VA:O_ZA])
    conv_p = h_main.reshape(BATCH, SEQ, N_MAIN)[:, SEQ - (CONV_W - 1):, O_QKVB:O_ZB][None]

    rows = B_QKV // LANES
    ab_s = jnp.pad(hs[:, O_A:O_GA], ((0, 0), (0, LANES - 2 * B_HEADS))).reshape(DEC_BATCH, 1, LANES)
    yb_s, ssm_s, conv_s = _gdn_step(
        state_conv[0].reshape(DEC_BATCH, CONV_W - 1, rows, LANES),
        hs[:, O_QKVB:O_ZB].reshape(DEC_BATCH, rows, LANES), cw.reshape(CONV_W, rows, LANES),
        ab_s, alog_l, dtb_l, heads(hs[:, O_ZB:O_A]), gdn_norm_w, state_ssm[0])
    hs_tail = jnp.pad(hs[:, N_MAIN:], ((0, 0), (0, LANES - (D_IN - N_MAIN))))
    y_sample = _merge(x_sample[:, 0, :], ya_s.reshape(DEC_BATCH, -1).astype(BF16),
                      yb_s.reshape(DEC_BATCH, -1).astype(BF16), hs, hs_tail,
                      wpa, wpb, wo, fw, DEC_BATCH)

    return (y_prompt.reshape(BATCH, SEQ, D_MODEL), y_sample.reshape(DEC_BATCH, 1, D_MODEL),
            k_rows_p, v_rows_p, conv_p, ssm_p[None],
            k_s.reshape(1, DEC_BATCH, 1, A_HEADS, LANES), v_s.reshape(1, DEC_BATCH, 1, A_HEADS, LANES),
            conv_s.reshape(1, DEC_BATCH, CONV_W - 1, B_QKV), ssm_s[None])
```

```python
import functools
import math

import jax
import jax.numpy as jnp
from jax import lax
from jax.experimental import pallas as pl
from jax.experimental.pallas import tpu as pltpu

F32 = jnp.float32
BF16 = jnp.bfloat16

D_MODEL = 2048
BATCH = 4
SEQ = 2048
DEC_BATCH = 32
PAST_LEN = 8192
PAGE_SIZE = 128
N_PAGES = PAST_LEN // PAGE_SIZE
N_META = 16
A_HEADS = 8
A_DH = 64
A_DV = 128
ROPE_DIM = 16
ROPE_THETA = 500000.0
B_HEADS = 8
B_DK = 128
B_DV = 128
B_QKV = 3072
CONV_W = 4
CHUNK = 64
EPS = 1e-6
NEG = -1e30
LAMBDA_INIT = 0.8 - 0.6 * math.exp(-0.3 * 0)

O_QA, O_KA, O_VA, O_ZA, O_QKVB, O_ZB, O_A, O_GA, O_GB, D_IN = (
    0, 1024, 2048, 3072, 4096, 7168, 8192, 8208, 10256, 12304)
LANES = 128
N_MAIN = (D_IN // LANES) * LANES
GATE_SHIFT = O_GA % LANES
VMEM_LIMIT = 56 * 1024 * 1024


def _dot(a, b):
    return jnp.dot(a, b, preferred_element_type=F32)


def _dot_nt(a, b):
    return lax.dot_general(a, b, (((1,), (1,)), ((), ())), preferred_element_type=F32)


def _dot_f32(a, b):
    return jnp.dot(a, b, preferred_element_type=F32, precision=lax.Precision.HIGHEST)


def _sigmoid(x):
    return 1.0 / (1.0 + jnp.exp(-x))


def _silu(x):
    return x * _sigmoid(x)


def _softplus(x):
    return jnp.maximum(x, 0.0) + jnp.log1p(jnp.exp(-jnp.abs(x)))


def _rmsnorm(x, w):
    return x * lax.rsqrt(jnp.mean(x * x, axis=-1, keepdims=True) + EPS) * w


def _rope_tables(pos):
    r = pos.shape[0]
    inv_freq = ROPE_THETA ** (-jnp.arange(0, ROPE_DIM, 2, dtype=F32) / ROPE_DIM)
    ang = pos.astype(F32)[:, None] * inv_freq[None, :]
    cos, sin = jnp.cos(ang), jnp.sin(ang)
    half = ROPE_DIM // 2
    rest = A_DH - ROPE_DIM
    c = jnp.concatenate([cos, cos, jnp.ones((r, rest), F32)], axis=1)
    sa = jnp.concatenate([jnp.zeros((r, half), F32), sin, jnp.zeros((r, rest), F32)], axis=1)
    sb = jnp.concatenate([-sin, jnp.zeros((r, half + rest), F32)], axis=1)
    return tuple(jnp.tile(t, (1, LANES // A_DH)) for t in (c, sa, sb))


def _rope_tile(t, c, sa, sb):
    out = []
    for i in range(t.shape[1] // LANES):
        x = t[:, i * LANES:(i + 1) * LANES]
        out.append(x * c + pltpu.roll(x, ROPE_DIM // 2, 1) * sa
                   + pltpu.roll(x, LANES - ROPE_DIM // 2, 1) * sb)
    return jnp.concatenate(out, axis=1) if len(out) > 1 else out[0]


IP_TM = 1024
IP_TN = 1024
IP_ROWS = 256


def _inproj_kernel(x_ref, nw_ref, w_ref, wt_ref, c_ref, sa_ref, sb_ref, h_ref, ht_ref, xn_ref):
    j = pl.program_id(1)

    @pl.when(j == 0)
    def _():
        def body(r, carry):
            rows = pl.ds(pl.multiple_of(r * IP_ROWS, IP_ROWS), IP_ROWS)
            xn_ref[rows, :] = _rmsnorm(x_ref[rows, :], nw_ref[...]).astype(BF16)
            return carry
        lax.fori_loop(0, IP_TM // IP_ROWS, body, 0)
        ht_ref[...] = _dot_nt(xn_ref[...], wt_ref[...])

    h_ref[...] = _dot_nt(xn_ref[...], w_ref[...])

    @pl.when(j < O_VA // IP_TN)
    def _():
        def body(r, carry):
            rows = pl.ds(pl.multiple_of(r * IP_ROWS, IP_ROWS), IP_ROWS)
            h_ref[rows, :] = _rope_tile(h_ref[rows, :], c_ref[rows, :], sa_ref[rows, :], sb_ref[rows, :])
            return carry
        lax.fori_loop(0, IP_TM // IP_ROWS, body, 0)


def _inproj(x, nw, w_bf, w_tail, tabs):
    m = x.shape[0]
    per_seq = SEQ // IP_TM
    tab_spec = pl.BlockSpec((IP_TM, LANES), lambda i, j: (i % per_seq, 0))
    return pl.pallas_call(
        _inproj_kernel,
        grid=(m // IP_TM, N_MAIN // IP_TN),
        in_specs=[
            pl.BlockSpec((IP_TM, D_MODEL), lambda i, j: (i, 0)),
            pl.BlockSpec((1, D_MODEL), lambda i, j: (0, 0)),
            pl.BlockSpec((IP_TN, D_MODEL), lambda i, j: (j, 0)),
            pl.BlockSpec((LANES, D_MODEL), lambda i, j: (0, 0)),
            tab_spec, tab_spec, tab_spec,
        ],
        out_specs=[
            pl.BlockSpec((IP_TM, IP_TN), lambda i, j: (i, j)),
            pl.BlockSpec((IP_TM, LANES), lambda i, j: (i, 0)),
        ],
        out_shape=[jax.ShapeDtypeStruct((m, N_MAIN), F32),
                   jax.ShapeDtypeStruct((m, LANES), F32)],
        scratch_shapes=[pltpu.VMEM((IP_TM, D_MODEL), BF16)],
        compiler_params=pltpu.CompilerParams(
            dimension_semantics=("arbitrary", "arbitrary"), vmem_limit_bytes=VMEM_LIMIT),
        name="inproj",
    )(x, nw, w_bf, w_tail, *tabs)


AUX_ROWS = 64
AUX_TN = 512


def _aux_inproj_kernel(x_ref, nw_ref, w_ref, c_ref, sa_ref, sb_ref, o_ref, wbf_ref, wtail_ref, xs_ref):
    j = pl.program_id(0)

    @pl.when(j == 0)
    def _():
        xn = _rmsnorm(x_ref[...], nw_ref[...])
        hi = xn.astype(BF16)
        xs_ref[0:AUX_ROWS, :] = hi
        xs_ref[AUX_ROWS:2 * AUX_ROWS, :] = (xn - hi.astype(F32)).astype(BF16)

    w = w_ref[...]
    w_hi = w.astype(BF16)
    wbf_ref[...] = w_hi

    @pl.when(j == pl.num_programs(0) - 1)
    def _():
        row = lax.broadcasted_iota(jnp.int32, (LANES, D_MODEL), 0)
        wtail_ref[...] = jnp.where(row < D_IN - N_MAIN, w_hi[0:LANES, :], jnp.zeros((), BF16))

    w_lo = (w - w_hi.astype(F32)).astype(BF16)
    r1 = _dot_nt(xs_ref[...], w_hi)
    r2 = _dot_nt(xs_ref[0:AUX_ROWS, :], w_lo)
    acc = r1[0:AUX_ROWS] + (r1[AUX_ROWS:] + r2)
    is_rope = j < O_VA // AUX_TN

    @pl.when(is_rope)
    def _():
        o_ref[...] = _rope_tile(acc, c_ref[...], sa_ref[...], sb_ref[...])

    @pl.when(jnp.logical_not(is_rope))
    def _():
        o_ref[...] = acc


def _aux_inproj(xa, nw, w, tabs):
    tab_spec = pl.BlockSpec((AUX_ROWS, LANES), lambda j: (0, 0))
    return pl.pallas_call(
        _aux_inproj_kernel,
        grid=(pl.cdiv(D_IN, AUX_TN),),
        in_specs=[
            pl.BlockSpec((AUX_ROWS, D_MODEL), lambda j: (0, 0)),
            pl.BlockSpec((1, D_MODEL), lambda j: (0, 0)),
            pl.BlockSpec((AUX_TN, D_MODEL), lambda j: (j, 0)),
            tab_spec, tab_spec, tab_spec,
        ],
        out_specs=[pl.BlockSpec((AUX_ROWS, AUX_TN), lambda j: (0, j)),
                   pl.BlockSpec((AUX_TN, D_MODEL), lambda j: (j, 0)),
                   pl.BlockSpec((LANES, D_MODEL), lambda j: (0, 0))],
        out_shape=[jax.ShapeDtypeStruct((AUX_ROWS, D_IN), F32),
                   jax.ShapeDtypeStruct((D_IN, D_MODEL), BF16),
                   jax.ShapeDtypeStruct((LANES, D_MODEL), BF16)],
        scratch_shapes=[pltpu.VMEM((2 * AUX_ROWS, D_MODEL), BF16)],
        compiler_params=pltpu.CompilerParams(
            dimension_semantics=("arbitrary",), vmem_limit_bytes=VMEM_LIMIT),
        name="aux_inproj",
    )(xa, nw, w, *tabs)


AT_TQ = 512
AT_TK = 512
AT_HP = 2
LOG2E = math.log2(math.e)


def _diff_lambda(lq1, lk1, lq2, lk2):
    a = jnp.exp(jnp.sum(lq1 * lk1, axis=-1, keepdims=True))
    b = jnp.exp(jnp.sum(lq2 * lk2, axis=-1, keepdims=True))
    return a - b + LAMBDA_INIT


def _attn_kernel(lq1_ref, lk1_ref, lq2_ref, lk2_ref, q_ref, k_ref, v_ref, km_ref, vm_ref,
                 za_ref, sw_ref, o_ref, kb_ref, vb_ref, kmb_ref, vmb_ref, q2_ref, s_ref, sm_ref,
                 m_ref, l_ref, acc_ref):
    qi = pl.program_id(2)
    tq, tk = AT_TQ, AT_TK
    n_tiles = tk // LANES
    heads = range(AT_HP)
    hl = lambda h: slice(h * LANES, (h + 1) * LANES)

    @pl.when(qi == 0)
    def _():
        pad = jnp.zeros((LANES - N_META, LANES), BF16)
        for h in heads:
            kb_ref[h] = k_ref[:, hl(h)].astype(BF16)
            vb_ref[h] = v_ref[:, hl(h)].astype(BF16)
            kmb_ref[h] = jnp.concatenate([km_ref[:, hl(h)].astype(BF16), pad], axis=0)
            vmb_ref[h] = jnp.concatenate([vm_ref[:, hl(h)].astype(BF16), pad], axis=0)

    lane = lax.broadcasted_iota(jnp.int32, (tq, LANES), 1)
    for h in heads:
        q = q_ref[:, hl(h)] * (A_DH ** -0.5 * LOG2E)
        q2_ref[h, 0:tq, :] = jnp.where(lane < A_DH, q, 0.0).astype(BF16)
        q2_ref[h, tq:2 * tq, :] = jnp.where(lane >= A_DH, q, 0.0).astype(BF16)

    def tile_max(m, s):
        for c in range(s.shape[1] // LANES):
            m = jnp.maximum(m, s[:, c * LANES:(c + 1) * LANES])
        return m

    lane2 = lax.broadcasted_iota(jnp.int32, (2 * tq, LANES), 1)
    sm = [jnp.where(lane2 < N_META, _dot_nt(q2_ref[h], kmb_ref[h]), NEG) for h in heads]
    for h in heads:
        sm_ref[h] = sm[h]
        m_ref[h] = sm[h]

    def scores(j):
        rows = pl.ds(pl.multiple_of(j * tk, tk), tk)
        return [_dot_nt(q2_ref[h], kb_ref[h, rows, :]) for h in heads]

    def pass1(j, carry):
        s = scores(j)
        for h in heads:
            s_ref[h, j] = s[h]
            m_ref[h] = tile_max(m_ref[h], s[h])
        return carry
    lax.fori_loop(0, qi, pass1, 0)

    r = lax.broadcasted_iota(jnp.int32, (2 * tq, tk), 0)
    r = jnp.where(r >= tq, r - tq, r)
    c = lax.broadcasted_iota(jnp.int32, (2 * tq, tk), 1)
    causal = c <= r
    s = [jnp.where(causal, t, NEG) for t in scores(qi)]
    for h in heads:
        s_ref[h, qi] = s[h]
    m = [jnp.max(tile_max(m_ref[h], s[h]), axis=-1, keepdims=True) for h in heads]
    for h in heads:
        m_ref[h] = jnp.broadcast_to(m[h], (2 * tq, LANES))

    p = [jnp.exp2(sm_ref[h] - m_ref[h]) for h in heads]
    for h in heads:
        l_ref[h] = p[h]
        acc_ref[h] = _dot(p[h].astype(BF16), vmb_ref[h])

    def pass2(j, carry):
        rows = pl.ds(pl.multiple_of(j * tk, tk), tk)
        pb = []
        for h in heads:
            mb = m_ref[h]
            lsum = l_ref[h]
            ps = []
            for t in range(n_tiles):
                p = jnp.exp2(s_ref[h, j, :, t * LANES:(t + 1) * LANES] - mb)
                lsum = lsum + p
                ps.append(p.astype(BF16))
            l_ref[h] = lsum
            pb.append(jnp.concatenate(ps, axis=1))
        pv = [_dot(pb[h], vb_ref[h, rows, :]) for h in heads]
        for h in heads:
            acc_ref[h] += pv[h]
        return carry
    lax.fori_loop(0, qi + 1, pass2, 0)

    lam = _diff_lambda(lq1_ref[...], lk1_ref[...], lq2_ref[...], lk2_ref[...])
    for h in heads:
        l = jnp.sum(l_ref[h], axis=-1, keepdims=True)
        o = acc_ref[h, 0:tq, :] / l[0:tq] - lam * (acc_ref[h, tq:2 * tq, :] / l[tq:2 * tq])
        y = _rmsnorm(o, sw_ref[...]) * (1.0 - LAMBDA_INIT)
        o_ref[:, hl(h)] = (y * _silu(za_ref[:, hl(h)])).astype(BF16)


def _attention(lams, h_main, k_meta, v_meta, subln_w):
    nq = SEQ // AT_TQ
    w = AT_HP * LANES
    lam_spec = pl.BlockSpec((1, A_DH), lambda b, g, i: (0, 0))
    return pl.pallas_call(
        _attn_kernel,
        grid=(BATCH, A_HEADS // AT_HP, nq),
        in_specs=[
            lam_spec, lam_spec, lam_spec, lam_spec,
            pl.BlockSpec((AT_TQ, w), lambda b, g, i: (b * nq + i, O_QA // w + g)),
            pl.BlockSpec((SEQ, w), lambda b, g, i: (b, O_KA // w + g)),
            pl.BlockSpec((SEQ, w), lambda b, g, i: (b, O_VA // w + g)),
            pl.BlockSpec((N_META, w), lambda b, g, i: (0, g)),
            pl.BlockSpec((N_META, w), lambda b, g, i: (0, g)),
            pl.BlockSpec((AT_TQ, w), lambda b, g, i: (b * nq + i, O_ZA // w + g)),
            pl.BlockSpec((1, A_DV), lambda b, g, i: (0, 0)),
        ],
        out_specs=pl.BlockSpec((AT_TQ, w), lambda b, g, i: (b * nq + i, g)),
        out_shape=jax.ShapeDtypeStruct((BATCH * SEQ, A_HEADS * A_DV), BF16),
        scratch_shapes=[
            pltpu.VMEM((AT_HP, SEQ, LANES), BF16), pltpu.VMEM((AT_HP, SEQ, LANES), BF16),
            pltpu.VMEM((AT_HP, LANES, LANES), BF16), pltpu.VMEM((AT_HP, LANES, LANES), BF16),
            pltpu.VMEM((AT_HP, 2 * AT_TQ, LANES), BF16),
            pltpu.VMEM((AT_HP, SEQ // AT_TK, 2 * AT_TQ, AT_TK), F32),
            pltpu.VMEM((AT_HP, 2 * AT_TQ, LANES), F32),
            pltpu.VMEM((AT_HP, 2 * AT_TQ, LANES), F32), pltpu.VMEM((AT_HP, 2 * AT_TQ, LANES), F32),
            pltpu.VMEM((AT_HP, 2 * AT_TQ, A_DV), F32),
        ],
        compiler_params=pltpu.CompilerParams(
            dimension_semantics=("arbitrary", "arbitrary", "arbitrary"),
            vmem_limit_bytes=VMEM_LIMIT),
        name="diff_attn",
    )(*lams, h_main, h_main, h_main, k_meta, v_meta, h_main, subln_w)


GDN_FILL_POINTS = 12


def _unit_lower_inverse_minus_eye(a_list, i_idx, j_idx, fill):
    base = 8
    diag = (i_idx // base) == (j_idx // base)
    b = [jnp.where(diag, a, 0.0) for a in a_list]
    n = [-x for x in b]
    for _ in range(2):
        bb = [x.astype(BF16) for x in b]
        b = [_dot(x, x) for x in bb]
        nb = [_dot(x.astype(BF16), y.astype(BF16)) for x, y in zip(n, b)]
        n = [x + y + z for x, y, z in zip(n, b, nb)]
        fill()
    s = base
    while s < CHUNK:
        join = ((i_idx // (2 * s)) == (j_idx // (2 * s))) & ((i_idx // s) % 2 == 1) & ((j_idx // s) % 2 == 0)
        a_s = [jnp.where(join, a, 0.0) for a in a_list]
        x = [p + _dot(q.astype(BF16), p.astype(BF16)) for p, q in zip(a_s, n)]
        xn = [_dot(p.astype(BF16), q.astype(BF16)) for p, q in zip(x, n)]
        n = [q - (p + r) for q, p, r in zip(n, x, xn)]
        fill()
        s *= 2
    return n


def _gdn_kernel(xq_ref, xk_ref, xv_ref, ab_ref, abt_ref, zb_ref, cw_ref, alog_ref, dtb_ref,
                alogc_ref, dtbc_ref, gw_ref, halo_ref, s0_ref, yb_ref, sfin_ref, ext_ref, s_ref,
                *, masked_rows, fill=lambda: None):
    c = pl.program_id(1)
    hist = 8

    @pl.when(c == 0)
    def _():
        ext_ref[0:hist, :] = halo_ref[...]
        s_ref[...] = s0_ref[...]

    x = jnp.concatenate([xq_ref[...], xk_ref[...], xv_ref[...]], axis=1)
    ext_ref[hist:hist + CHUNK, :] = x
    y = cw_ref[CONV_W - 1:CONV_W, :] * x
    for t in range(CONV_W - 1):
        lo = hist - (CONV_W - 1) + t
        y = y + cw_ref[t:t + 1, :] * ext_ref[lo:lo + CHUNK, :]
    y = _silu(y)
    ext_ref[0:hist, :] = x[CHUNK - hist:CHUNK, :]
    fill()

    i_idx = lax.broadcasted_iota(jnp.int32, (CHUNK, CHUNK), 0)
    j_idx = lax.broadcasted_iota(jnp.int32, (CHUNK, CHUNK), 1)
    tril = i_idx >= j_idx
    strict = i_idx > j_idx
    tril_f = tril.astype(F32)
    triu_f = (i_idx <= j_idx).astype(F32)

    ab = ab_ref[...]
    g_c = -jnp.exp(alog_ref[...]) * _softplus(ab + dtb_ref[...])
    beta_c = _sigmoid(ab)
    abt = abt_ref[...]
    g_r = -jnp.exp(alogc_ref[...]) * _softplus(abt + dtbc_ref[...])
    if masked_rows:
        row_ok = lax.broadcasted_iota(jnp.int32, (CHUNK, LANES), 0) >= masked_rows
        col_ok = lax.broadcasted_iota(jnp.int32, (2 * B_HEADS, CHUNK), 1) >= masked_rows
        g_c = jnp.where(row_ok, g_c, 0.0)
        beta_c = jnp.where(row_ok, beta_c, 0.0)
        g_r = jnp.where(col_ok, g_r, 0.0)
    gc_c = _dot_f32(tril_f, g_c)
    gc_r = _dot_f32(g_r, triu_f)
    fill()

    nk = B_HEADS * B_DK
    heads = range(B_HEADS)

    def l2n(t):
        return t * lax.rsqrt(jnp.sum(t * t, axis=-1, keepdims=True) + EPS)

    qn = [l2n(y[:, h * B_DK:(h + 1) * B_DK]) * (B_DK ** -0.5) for h in heads]
    kn = [l2n(y[:, nk + h * B_DK:nk + (h + 1) * B_DK]) for h in heads]
    vh = [y[:, 2 * nk + h * B_DV:2 * nk + (h + 1) * B_DV] for h in heads]
    bcol = [beta_c[:, B_HEADS + h:B_HEADS + h + 1] for h in heads]
    gcc = [gc_c[:, h:h + 1] for h in heads]
    decay = [jnp.where(tril, jnp.exp(jnp.where(tril, gcc[h] - gc_r[h:h + 1, :], 0.0)), 0.0)
             for h in heads]
    kbeta = [kn[h] * bcol[h] for h in heads]
    fill()
    kn_b = [t.astype(BF16) for t in kn]
    kk = [_dot_nt(kbeta[h].astype(BF16), kn_b[h]) for h in heads]
    qk = [_dot_nt(qn[h].astype(BF16), kn_b[h]) for h in heads]
    a = [jnp.where(strict, kk[h] * decay[h], 0.0) for h in heads]
    fill()
    n = _unit_lower_inverse_minus_eye(a, i_idx, j_idx, fill)
    egc = [jnp.exp(t) for t in gcc]
    rhs = [jnp.concatenate([vh[h] * bcol[h], kbeta[h] * egc[h]], axis=1) for h in heads]
    nr = [_dot(n[h].astype(BF16), rhs[h].astype(BF16)) for h in heads]
    sol = [rhs[h] + nr[h] for h in heads]
    fill()
    st = [s_ref[h] for h in heads]
    st_b = [t.astype(BF16) for t in st]
    ws = [_dot(sol[h][:, B_DV:B_DV + B_DK].astype(BF16), st_b[h]) for h in heads]
    qs = [_dot((qn[h] * egc[h]).astype(BF16), st_b[h]) for h in heads]
    v_new_b = [(sol[h][:, 0:B_DV] - ws[h]).astype(BF16) for h in heads]
    fill()
    av = [_dot((qk[h] * decay[h]).astype(BF16), v_new_b[h]) for h in heads]
    g_last = [t[CHUNK - 1:CHUNK, :] for t in gcc]
    ke_t = [(kn[h] * jnp.exp(g_last[h] - gcc[h])).T.astype(BF16) for h in heads]
    kv = [_dot(ke_t[h], v_new_b[h]) for h in heads]
    fill()
    for h in heads:
        s_ref[h] = st[h] * jnp.exp(g_last[h]) + kv[h]
    for h in heads:
        zb = zb_ref[:, h * B_DV:(h + 1) * B_DV]
        yb_ref[:, h * B_DV:(h + 1) * B_DV] = (_rmsnorm(qs[h] + av[h], gw_ref[...]) * _silu(zb)).astype(BF16)

    @pl.when(c == pl.num_programs(1) - 1)
    def _():
        sfin_ref[...] = s_ref[...]


def _gdn_chunks(x, x_col, ab, ab_col, abt, zsrc, zb_col, cw, gvecs, gw, halo, s0, n_seq, n_chunk,
                masked_rows):
    alog_l, dtb_l, alog_c, dtb_c = gvecs
    const2 = lambda b, c: (0, 0)
    kern = functools.partial(_gdn_kernel, masked_rows=masked_rows)
    part = B_QKV // 3

    def x_spec(k):
        return pl.BlockSpec((CHUNK, part), lambda b, c: (b * n_chunk + c, x_col + k))

    return pl.pallas_call(
        kern,
        grid=(n_seq, n_chunk),
        in_specs=[
            x_spec(0), x_spec(1), x_spec(2),
            pl.BlockSpec((CHUNK, LANES), lambda b, c: (b * n_chunk + c, ab_col)),
            pl.BlockSpec((None, None, 2 * B_HEADS, CHUNK), lambda b, c: (b, c, 0, 0)),
            pl.BlockSpec((CHUNK, B_HEADS * B_DV), lambda b, c: (b * n_chunk + c, zb_col)),
            pl.BlockSpec((CONV_W, B_QKV), const2),
            pl.BlockSpec((1, LANES), const2), pl.BlockSpec((1, LANES), const2),
            pl.BlockSpec((2 * B_HEADS, 1), const2), pl.BlockSpec((2 * B_HEADS, 1), const2),
            pl.BlockSpec((1, B_DV), const2),
            pl.BlockSpec((8, B_QKV), const2),
            pl.BlockSpec((B_HEADS, B_DK, B_DV), lambda b, c: (0, 0, 0)),
        ],
        out_specs=[
            pl.BlockSpec((CHUNK, B_HEADS * B_DV), lambda b, c: (b * n_chunk + c, 0)),
            pl.BlockSpec((None, B_HEADS, B_DK, B_DV), lambda b, c: (b, 0, 0, 0)),
        ],
        out_shape=[jax.ShapeDtypeStruct((n_seq * n_chunk * CHUNK, B_HEADS * B_DV), BF16),
                   jax.ShapeDtypeStruct((n_seq, B_HEADS, B_DK, B_DV), F32)],
        scratch_shapes=[pltpu.VMEM((8 + CHUNK, B_QKV), F32),
                        pltpu.VMEM((B_HEADS, B_DK, B_DV), F32)],
        compiler_params=pltpu.CompilerParams(
            dimension_semantics=("arbitrary", "arbitrary"), vmem_limit_bytes=VMEM_LIMIT),
        name="gdn_chunks",
    )(x, x, x, ab, abt, zsrc, cw, alog_l, dtb_l, alog_c, dtb_c, gw, halo, s0)


def _shift_lanes(x, k):
    nblk = x.shape[1] // LANES
    r = [pltpu.roll(x[:, c * LANES:(c + 1) * LANES], LANES - k, 1) for c in range(nblk)]
    lane = lax.broadcasted_iota(jnp.int32, (x.shape[0], LANES), 1)
    return jnp.concatenate([jnp.where(lane < LANES - k, r[c], r[c + 1]) for c in range(nblk - 1)],
                           axis=1)


def _merge_kernel(x_ref, ya_ref, yb_ref, g4_ref, g5_ref, gt_ref, wpa_ref, wpb_ref, wo_ref, fw_ref,
                  y_ref):
    g5 = g5_ref[...]
    ga = _shift_lanes(jnp.concatenate([g4_ref[...], g5[:, 0:LANES]], axis=1), GATE_SHIFT)
    gb = _shift_lanes(jnp.concatenate([g5, gt_ref[...]], axis=1), GATE_SHIFT)
    pa = _dot(ya_ref[...], wpa_ref[...])
    pb = _dot(yb_ref[...], wpb_ref[...])
    mixed = _sigmoid(ga) * pa + _sigmoid(gb) * pb
    hp = x_ref[...] + _dot(mixed.astype(BF16), wo_ref[...])
    y_ref[...] = _rmsnorm(hp, fw_ref[...])


def _merge(x, ya, yb, hsrc, htail, wpa, wpb, wo, fw, tm):
    m = x.shape[0]
    once = pl.Buffered(1)
    return pl.pallas_call(
        _merge_kernel,
        grid=(m // tm,),
        in_specs=[
            pl.BlockSpec((tm, D_MODEL), lambda i: (i, 0)),
            pl.BlockSpec((tm, A_HEADS * A_DV), lambda i: (i, 0)),
            pl.BlockSpec((tm, B_HEADS * B_DV), lambda i: (i, 0)),
            pl.BlockSpec((tm, D_MODEL), lambda i: (i, O_A // D_MODEL)),
            pl.BlockSpec((tm, D_MODEL), lambda i: (i, O_A // D_MODEL + 1)),
            pl.BlockSpec((tm, LANES), lambda i: (i, 0)),
            pl.BlockSpec((A_HEADS * A_DV, D_MODEL), lambda i: (0, 0), pipeline_mode=once),
            pl.BlockSpec((B_HEADS * B_DV, D_MODEL), lambda i: (0, 0), pipeline_mode=once),
            pl.BlockSpec((D_MODEL, D_MODEL), lambda i: (0, 0), pipeline_mode=once),
            pl.BlockSpec((1, D_MODEL), lambda i: (0, 0)),
        ],
        out_specs=pl.BlockSpec((tm, D_MODEL), lambda i: (i, 0)),
        out_shape=jax.ShapeDtypeStruct((m, D_MODEL), F32),
        compiler_params=pltpu.CompilerParams(
            dimension_semantics=("arbitrary",), vmem_limit_bytes=VMEM_LIMIT),
        name="merge",
    )(x, ya, yb, hsrc, hsrc, htail, wpa, wpb, wo, fw)


HALF_TOK = PAGE_SIZE // 2
HALF_ROWS = HALF_TOK * A_HEADS
TOK_TILES = HALF_ROWS // LANES


def _lane_group_reduce(x, op):
    s = A_HEADS
    while s < LANES:
        x = op(x, pltpu.roll(x, s, 1))
        s *= 2
    return x


FD_PAGES = N_PAGES // 2
FD_STEPS = 4
FD_NBUF = 2 * FD_PAGES


def _gdn_decode_kernel(pt_ref, xq_ref, xk_ref, xv_ref, ab_ref, abt_ref, zb_ref, cw_ref, alog_ref,
                       dtb_ref, alogc_ref, dtbc_ref, gw_ref, halo_ref, s0_ref,
                       lq1_ref, lk1_ref, lq2_ref, lk2_ref, q_ref, kn_ref, vn_ref, za_ref, sw_ref,
                       ck_ref, cv_ref, yb_ref, sfin_ref, o_ref,
                       ext_ref, s_ref, buf_ref, sem_ref, sc_ref, acc_ref, wn_ref):
    n_chunk = pl.num_programs(1)
    t = pl.program_id(0) * n_chunk + pl.program_id(1)
    n_steps = pl.num_programs(0) * n_chunk
    role = t % FD_STEPS

    def start_step(step):
        seq = step // FD_STEPS
        r = step % FD_STEPS
        half = (step % 2) * FD_PAGES
        first = (r % 2) * FD_PAGES

        def copies(src_ref):
            for s in range(FD_PAGES):
                pltpu.make_async_copy(src_ref.at[0, pt_ref[seq, first + s]], buf_ref.at[half + s],
                                      sem_ref.at[half + s]).start(priority=s % 2)

        @pl.when(r < 2)
        def _():
            copies(ck_ref)

        @pl.when(r >= 2)
        def _():
            copies(cv_ref)

    def wait_slot(slot):
        pltpu.make_async_copy(ck_ref.at[0, 0], buf_ref.at[slot], sem_ref.at[slot]).wait()

    @pl.when(t == 0)
    def _():
        start_step(0)

    @pl.when(t + 1 < n_steps)
    def _():
        start_step(t + 1)

    def gdn_chunk(fill):
        _gdn_kernel(xq_ref, xk_ref, xv_ref, ab_ref, abt_ref, zb_ref, cw_ref, alog_ref, dtb_ref,
                    alogc_ref, dtbc_ref, gw_ref, halo_ref, s0_ref, yb_ref, sfin_ref, ext_ref, s_ref,
                    masked_rows=0, fill=fill)

    def page_batches():
        lo, hi = divmod(FD_PAGES, GDN_FILL_POINTS)
        sizes = [lo + 1] * hi + [lo] * (GDN_FILL_POINTS - hi)
        starts = [sum(sizes[:i]) for i in range(GDN_FILL_POINTS)]
        return iter([range(a, a + n) for a, n in zip(starts, sizes)])

    half = (t % 2) * FD_PAGES
    sub = lax.broadcasted_iota(jnp.int32, (A_HEADS, LANES), 0)
    lane = lax.broadcasted_iota(jnp.int32, (A_HEADS, LANES), 1)
    hmask = (lane % A_HEADS) == sub
    first_half = sub < 2
    second_half = (sub >= 2) & (sub < 4)
    map0 = (sub % 2) == 0
    sub2 = lax.broadcasted_iota(jnp.int32, (A_HEADS, 2 * LANES), 0)
    lane2 = lax.broadcasted_iota(jnp.int32, (A_HEADS, 2 * LANES), 1)
    half_sel = ((sub2 < 4) & (lane2 // A_DH == sub2)).astype(BF16)

    def halves_on_lanes(page):
        return jnp.concatenate([page[0:HALF_TOK].reshape(HALF_ROWS, LANES),
                                page[HALF_TOK:].reshape(HALF_ROWS, LANES)], axis=1)

    def pair_halves(x, op):
        up = pltpu.roll(x, 2, 0)
        dn = pltpu.roll(x, A_HEADS - 2, 0)
        return jnp.where(first_half, op(x, dn), jnp.where(second_half, op(x, up), 0.0))

    @pl.when(role < 2)
    def _():
        q = q_ref[...]
        for s in range(FD_PAGES):
            wait_slot(half + s)
        batches = page_batches()

        def k_fill():
            pages = next(batches)
            prods = [halves_on_lanes(buf_ref[half + p] * q[None]).astype(BF16) for p in pages]
            scs = [_dot_nt(half_sel, x) for x in prods]
            for p, sc in zip(pages, scs):
                base = (role * FD_PAGES + p) * TOK_TILES
                for c in range(TOK_TILES):
                    sc_ref[base + c] = sc[:, c * LANES:(c + 1) * LANES]
        gdn_chunk(k_fill)

    @pl.when(role == 1)
    def _():
        lam = _diff_lambda(lq1_ref[...], lk1_ref[...], lq2_ref[...], lk2_ref[...])
        prod = q_ref[...] * kn_ref[...]
        hs1 = jnp.sum(jnp.where(lane < A_DH, prod, 0.0), axis=1, keepdims=True)
        hs2 = jnp.sum(jnp.where(lane >= A_DH, prod, 0.0), axis=1, keepdims=True)
        row1 = jnp.sum(jnp.where(hmask, hs1, 0.0), axis=0, keepdims=True)
        row2 = jnp.sum(jnp.where(hmask, hs2, 0.0), axis=0, keepdims=True)
        s_new = jnp.where(sub < 4, jnp.where(map0, row1, row2), 0.0)
        sc = sc_ref[...]
        mx = pair_halves(_lane_group_reduce(jnp.max(sc, axis=0), jnp.maximum), jnp.maximum)
        mx = jnp.maximum(mx, s_new)
        p = jnp.exp(sc - mx[None])
        p_new = jnp.exp(s_new - mx)
        den = pair_halves(_lane_group_reduce(jnp.sum(p, axis=0), jnp.add), jnp.add) + p_new
        coef = jnp.where(sub < 4, jnp.where(map0, 1.0 / den, -lam / den), 0.0)
        sc_ref[...] = p * coef[None]
        wn_ref[...] = jnp.sum(jnp.where(first_half, p_new * coef, 0.0), axis=0, keepdims=True)
        acc_ref[...] = jnp.zeros_like(acc_ref)

    @pl.when(role >= 2)
    def _():
        def page_weights(page):
            wa, wb = [], []
            for c in range(TOK_TILES):
                tile = sc_ref[page * TOK_TILES + c]
                ra = jnp.sum(jnp.where(first_half, tile, 0.0), axis=0, keepdims=True)
                rb = jnp.sum(jnp.where(second_half, tile, 0.0), axis=0, keepdims=True)
                wa.append(jnp.where(hmask, ra, 0.0))
                wb.append(jnp.where(hmask, rb, 0.0))
            w = jnp.concatenate([jnp.concatenate(wa, axis=1), jnp.concatenate(wb, axis=1)], axis=0)
            w_hi = w.astype(BF16)
            w_lo = (w - w_hi.astype(F32)).astype(BF16)
            return jnp.concatenate([w_hi, w_lo], axis=0)

        for s in range(FD_PAGES):
            wait_slot(half + s)
        batches = page_batches()
        acc = [acc_ref[...]]

        def v_fill():
            pages = next(batches)
            ws = [page_weights((role - 2) * FD_PAGES + p) for p in pages]
            vs = [halves_on_lanes(buf_ref[half + p]).astype(BF16) for p in pages]
            rs = [_dot(w, v) for w, v in zip(ws, vs)]
            for r in rs:
                acc[0] = acc[0] + ((r[0:8, 0:LANES] + r[8:16, LANES:])
                                   + (r[16:24, 0:LANES] + r[24:32, LANES:]))
        gdn_chunk(v_fill)
        acc_ref[...] = acc[0]

    @pl.when(role == FD_STEPS - 1)
    def _():
        w_new = jnp.sum(jnp.where(hmask, wn_ref[...], 0.0), axis=1, keepdims=True) * (A_HEADS / LANES)
        o = acc_ref[...] + w_new * vn_ref[...]
        y = _rmsnorm(o, sw_ref[...]) * (1.0 - LAMBDA_INIT)
        o_ref[...] = y * _silu(za_ref[...])


def _gdn_decode(page_table, x, ab_src, abt, cw, gvecs, gw, halo, s0,
                lams, q_s, k_new, v_new, za_s, subln_w, cache_k, cache_v):
    alog_l, dtb_l, alog_c, dtb_c = gvecs
    n_chunk = SEQ // CHUNK
    assert BATCH * n_chunk == DEC_BATCH * FD_STEPS
    part = B_QKV // 3
    row = lambda b, c, pt: (b * n_chunk + c, 0)
    const2 = lambda b, c, pt: (0, 0)
    seq3 = lambda b, c, pt: ((b * n_chunk + c) // FD_STEPS, 0, 0)

    def col(width, k):
        return pl.BlockSpec((CHUNK, width), lambda b, c, pt: (b * n_chunk + c, k))

    lam_spec = pl.BlockSpec((1, A_DH), const2)
    seq_spec = pl.BlockSpec((None, A_HEADS, LANES), seq3)
    hbm_spec = pl.BlockSpec(memory_space=pl.ANY)
    grid_spec = pltpu.PrefetchScalarGridSpec(
        num_scalar_prefetch=1,
        grid=(BATCH, n_chunk),
        in_specs=[
            col(part, O_QKVB // part), col(part, O_QKVB // part + 1), col(part, O_QKVB // part + 2),
            col(LANES, O_A // LANES),
            pl.BlockSpec((None, None, 2 * B_HEADS, CHUNK), lambda b, c, pt: (b, c, 0, 0)),
            col(part, O_ZB // part),
            pl.BlockSpec((CONV_W, B_QKV), const2),
            pl.BlockSpec((1, LANES), const2), pl.BlockSpec((1, LANES), const2),
            pl.BlockSpec((2 * B_HEADS, 1), const2), pl.BlockSpec((2 * B_HEADS, 1), const2),
            pl.BlockSpec((1, B_DV), const2),
            pl.BlockSpec((8, B_QKV), const2),
            pl.BlockSpec((B_HEADS, B_DK, B_DV), lambda b, c, pt: (0, 0, 0)),
            lam_spec, lam_spec, lam_spec, lam_spec, seq_spec, seq_spec, seq_spec, seq_spec,
            pl.BlockSpec((1, A_DV), const2), hbm_spec, hbm_spec,
        ],
        out_specs=[
            pl.BlockSpec((CHUNK, B_HEADS * B_DV), row),
            pl.BlockSpec((None, B_HEADS, B_DK, B_DV), lambda b, c, pt: (b, 0, 0, 0)),
            seq_spec,
        ],
        scratch_shapes=[
            pltpu.VMEM((8 + CHUNK, B_QKV), F32),
            pltpu.VMEM((B_HEADS, B_DK, B_DV), F32),
            pltpu.VMEM((FD_NBUF, PAGE_SIZE, A_HEADS, LANES), F32),
            pltpu.SemaphoreType.DMA((FD_NBUF,)),
            pltpu.VMEM((N_PAGES * TOK_TILES, A_HEADS, LANES), F32),
            pltpu.VMEM((A_HEADS, A_DV), F32),
            pltpu.VMEM((1, LANES), F32),
        ],
    )
    return pl.pallas_call(
        _gdn_decode_kernel,
        grid_spec=grid_spec,
        out_shape=[jax.ShapeDtypeStruct((BATCH * SEQ, B_HEADS * B_DV), BF16),
                   jax.ShapeDtypeStruct((BATCH, B_HEADS, B_DK, B_DV), F32),
                   jax.ShapeDtypeStruct((DEC_BATCH, A_HEADS, A_DV), F32)],
        compiler_params=pltpu.CompilerParams(
            dimension_semantics=("arbitrary", "arbitrary"), vmem_limit_bytes=VMEM_LIMIT),
        name="gdn_decode",
    )(page_table, x, x, x, ab_src, abt, x, cw, alog_l, dtb_l, alog_c, dtb_c, gw, halo, s0,
      *lams, q_s, k_new, v_new, za_s, subln_w, cache_k, cache_v)


GS_SEQS = 4


def _gdn_step_kernel(sc_ref, x_ref, cw_ref, ab_ref, alog_ref, dtb_ref, zb_ref, gw_ref, s_ref,
                     yb_ref, sout_ref, cout_ref):
    for i in range(GS_SEQS):
        x = x_ref[i]
        y = cw_ref[CONV_W - 1] * x
        for t in range(CONV_W - 1):
            y = y + cw_ref[t] * sc_ref[i, t]
        y = _silu(y)
        for t in range(CONV_W - 2):
            cout_ref[i, t] = sc_ref[i, t + 1]
        cout_ref[i, CONV_W - 2] = x

        q = y[0:B_HEADS]
        k = y[B_HEADS:2 * B_HEADS]
        v = y[2 * B_HEADS:3 * B_HEADS]
        qn = q * lax.rsqrt(jnp.sum(q * q, axis=-1, keepdims=True) + EPS) * (B_DK ** -0.5)
        kn = k * lax.rsqrt(jnp.sum(k * k, axis=-1, keepdims=True) + EPS)
        qt = qn.T
        kt = kn.T
        ab = ab_ref[i]
        g = -jnp.exp(alog_ref[...]) * _softplus(ab + dtb_ref[...])
        beta = _sigmoid(ab)
        for h in range(B_HEADS):
            kcol = kt[:, h:h + 1]
            qcol = qt[:, h:h + 1]
            st = s_ref[i, h] * jnp.exp(g[:, h:h + 1])
            kv = jnp.sum(st * kcol, axis=0, keepdims=True)
            d = (v[h:h + 1] - kv) * beta[:, B_HEADS + h:B_HEADS + h + 1]
            st = st + kcol * d
            sout_ref[i, h] = st
            o = jnp.sum(st * qcol, axis=0, keepdims=True)
            yb_ref[i, h:h + 1, :] = _rmsnorm(o, gw_ref[...]) * _silu(zb_ref[i, h:h + 1, :])


def _gdn_step(state_conv, x, cw, ab, alog_l, dtb_l, zb, gw, state_ssm):
    rows = B_QKV // LANES
    seq3 = lambda b: (b, 0, 0)
    seq4 = lambda b: (b, 0, 0, 0)
    const2 = lambda b: (0, 0)
    return pl.pallas_call(
        _gdn_step_kernel,
        grid=(DEC_BATCH // GS_SEQS,),
        in_specs=[
            pl.BlockSpec((GS_SEQS, CONV_W - 1, rows, LANES), seq4),
            pl.BlockSpec((GS_SEQS, rows, LANES), seq3),
            pl.BlockSpec((CONV_W, rows, LANES), lambda b: (0, 0, 0)),
            pl.BlockSpec((GS_SEQS, 1, LANES), seq3),
            pl.BlockSpec((1, LANES), const2), pl.BlockSpec((1, LANES), const2),
            pl.BlockSpec((GS_SEQS, B_HEADS, B_DV), seq3),
            pl.BlockSpec((1, B_DV), const2),
            pl.BlockSpec((GS_SEQS, B_HEADS, B_DK, B_DV), seq4),
        ],
        out_specs=[
            pl.BlockSpec((GS_SEQS, B_HEADS, B_DV), seq3),
            pl.BlockSpec((GS_SEQS, B_HEADS, B_DK, B_DV), seq4),
            pl.BlockSpec((GS_SEQS, CONV_W - 1, rows, LANES), seq4),
        ],
        out_shape=[jax.ShapeDtypeStruct((DEC_BATCH, B_HEADS, B_DV), F32),
                   jax.ShapeDtypeStruct((DEC_BATCH, B_HEADS, B_DK, B_DV), F32),
                   jax.ShapeDtypeStruct((DEC_BATCH, CONV_W - 1, rows, LANES), F32)],
        compiler_params=pltpu.CompilerParams(
            dimension_semantics=("arbitrary",), vmem_limit_bytes=VMEM_LIMIT),
        name="gdn_step",
    )(state_conv, x, cw, ab, alog_l, dtb_l, zb, gw, state_ssm)


def kernel(x_prompt, x_sample, cache_k, cache_v, state_conv, state_ssm, page_table, meta_tokens,
           norm_w, w_in, lambda_q1, lambda_k1, lambda_q2, lambda_k2, subln_w, conv_w, a_log,
           dt_bias, gdn_norm_w, w_pa, w_pb, w_o, final_norm_w):
    assert x_prompt.shape == (BATCH, SEQ, D_MODEL) and x_sample.shape == (DEC_BATCH, 1, D_MODEL)
    assert w_in.shape == (1, D_MODEL, D_IN) and page_table.shape == (DEC_BATCH, N_PAGES)
    w = w_in[0].T
    nw = norm_w
    lams = (lambda_q1, lambda_k1, lambda_q2, lambda_k2)
    fw = final_norm_w.reshape(1, D_MODEL)
    wpa, wpb, wo = w_pa[0].astype(BF16), w_pb[0].astype(BF16), w_o[0].astype(BF16)
    cw = conv_w[0]

    def lanes8(v, off):
        return jnp.zeros((1, LANES), F32).at[0, off:off + B_HEADS].set(v)

    alog_l, dtb_l = lanes8(a_log[0], 0), lanes8(dt_bias[0], 0)
    alog_c, dtb_c = alog_l[0, 0:2 * B_HEADS].reshape(-1, 1), dtb_l[0, 0:2 * B_HEADS].reshape(-1, 1)
    gvecs = (alog_l, dtb_l, alog_c, dtb_c)

    xa = jnp.concatenate([x_sample[:, 0, :], meta_tokens,
                          jnp.zeros((AUX_ROWS - DEC_BATCH - N_META, D_MODEL), F32)], axis=0)
    pos_a = jnp.concatenate([jnp.full((DEC_BATCH,), PAST_LEN), jnp.arange(N_META),
                             jnp.zeros((AUX_ROWS - DEC_BATCH - N_META,), jnp.int32)])
    h_aux, w_bf, w_tail = _aux_inproj(xa, nw, w, _rope_tables(pos_a))
    hs, hm = h_aux[0:DEC_BATCH], h_aux[DEC_BATCH:DEC_BATCH + N_META]

    tabs_p = _rope_tables(N_META + jnp.arange(SEQ))
    h_main, h_tail = _inproj(x_prompt.reshape(BATCH * SEQ, D_MODEL), nw, w_bf, w_tail, tabs_p)

    pad_rows = CHUNK - N_META
    x_meta = jnp.pad(hm[:, O_QKVB:O_ZB], ((pad_rows, 0), (0, 0)))
    ab_meta = jnp.pad(hm[:, O_A:O_GA], ((pad_rows, 0), (0, LANES - 2 * B_HEADS)))
    abt_meta = ab_meta[:, 0:2 * B_HEADS].T.reshape(1, 1, 2 * B_HEADS, CHUNK)
    _, s_meta = _gdn_chunks(
        x_meta, 0, ab_meta, 0, abt_meta, jnp.zeros((CHUNK, B_HEADS * B_DV), F32), 0, cw, gvecs,
        gdn_norm_w, jnp.zeros((8, B_QKV), F32), jnp.zeros((B_HEADS, B_DK, B_DV), F32), 1, 1, pad_rows)
    k_meta, v_meta = hm[:, O_KA:O_VA], hm[:, O_VA:O_ZA]

    n_chunk = SEQ // CHUNK
    part = B_QKV // 3
    abt = h_main[:, O_A:O_GA].reshape(BATCH, n_chunk, CHUNK, 2 * B_HEADS).transpose(0, 1, 3, 2)
    heads = lambda t: t.reshape(DEC_BATCH, A_HEADS, LANES)
    q_s = heads(hs[:, O_QA:O_KA]) * (A_DH ** -0.5)
    k_s, v_s = heads(hs[:, O_KA:O_VA]), heads(hs[:, O_VA:O_ZA])
    yb, ssm_p, ya_s = _gdn_decode(
        page_table, h_main, h_main, abt, cw, gvecs, gdn_norm_w, x_meta[CHUNK - 8:CHUNK], s_meta[0],
        lams, q_s, k_s, v_s, heads(hs[:, O_ZA:O_QKVB]), subln_w, cache_k, cache_v)
    ya = _attention(lams, h_main, k_meta, v_meta, subln_w)
    y_prompt = _merge(x_prompt.reshape(BATCH * SEQ, D_MODEL), ya, yb, h_main, h_tail,
                      wpa, wpb, wo, fw, 256)

    def with_meta(rows_meta, rows_tok):
        meta = jnp.broadcast_to(rows_meta.reshape(1, N_META, A_HEADS, LANES),
                                (BATCH, N_META, A_HEADS, LANES))
        return jnp.concatenate([meta, rows_tok.reshape(BATCH, SEQ, A_HEADS, LANES)], axis=1)[None]

    k_rows_p = with_meta(k_meta, h_main[:, O_KA:O_VA])
    v_rows_p = with_meta(v_meta, h_main[:, O_VA:O_ZA])
    conv_p = h_main.reshape(BATCH, SEQ, N_MAIN)[:, SEQ - (CONV_W - 1):, O_QKVB:O_ZB][None]

    rows = B_QKV // LANES
    ab_s = jnp.pad(hs[:, O_A:O_GA], ((0, 0), (0, LANES - 2 * B_HEADS))).reshape(DEC_BATCH, 1, LANES)
    yb_s, ssm_s, conv_s = _gdn_step(
        state_conv[0].reshape(DEC_BATCH, CONV_W - 1, rows, LANES),
        hs[:, O_QKVB:O_ZB].reshape(DEC_BATCH, rows, LANES), cw.reshape(CONV_W, rows, LANES),
        ab_s, alog_l, dtb_l, heads(hs[:, O_ZB:O_A]), gdn_norm_w, state_ssm[0])
    hs_tail = jnp.pad(hs[:, N_MAIN:], ((0, 0), (0, LANES - (D_IN - N_MAIN))))
    y_sample = _merge(x_sample[:, 0, :], ya_s.reshape(DEC_BATCH, -1).astype(BF16),
                      yb_s.reshape(DEC_BATCH, -1).astype(BF16), hs, hs_tail,
                      wpa, wpb, wo, fw, DEC_BATCH)

    return (y_prompt.reshape(BATCH, SEQ, D_MODEL), y_sample.reshape(DEC_BATCH, 1, D_MODEL),
            k_rows_p, v_rows_p, conv_p, ssm_p[None],
            k_s.reshape(1, DEC_BATCH, 1, A_HEADS, LANES), v_s.reshape(1, DEC_BATCH, 1, A_HEADS, LANES),
            conv_s.reshape(1, DEC_BATCH, CONV_W - 1, B_QKV), ssm_s[None])
```

```python
import functools
import math

import jax
import jax.numpy as jnp
import numpy as np
from jax import lax
from jax.experimental import pallas as pl
from jax.experimental.pallas import tpu as pltpu

F32 = jnp.float32
BF16 = jnp.bfloat16

D_MODEL = 2048
BATCH = 4
SEQ = 2048
DEC_BATCH = 32
PAST_LEN = 8192
PAGE_SIZE = 128
N_PAGES = PAST_LEN // PAGE_SIZE
N_META = 16
A_HEADS = 8
A_DH = 64
A_DV = 128
ROPE_DIM = 16
ROPE_THETA = 500000.0
B_HEADS = 8
B_DK = 128
B_DV = 128
B_QKV = 3072
CONV_W = 4
CHUNK = 64
EPS = 1e-6
NEG = -1e30
LAMBDA_INIT = 0.8 - 0.6 * math.exp(-0.3 * 0)

O_QA, O_KA, O_VA, O_ZA, O_QKVB, O_ZB, O_A, O_GA, O_GB, D_IN = (
    0, 1024, 2048, 3072, 4096, 7168, 8192, 8208, 10256, 12304)
LANES = 128
N_MAIN = (D_IN // LANES) * LANES
GATE_SHIFT = O_GA % LANES
VMEM_LIMIT = 56 * 1024 * 1024


def _dot(a, b):
    return jnp.dot(a, b, preferred_element_type=F32)


def _dot_nt(a, b):
    return lax.dot_general(a, b, (((1,), (1,)), ((), ())), preferred_element_type=F32)


def _dot_f32(a, b):
    return jnp.dot(a, b, preferred_element_type=F32, precision=lax.Precision.HIGHEST)


def _sigmoid(x):
    return 1.0 / (1.0 + jnp.exp(-x))


def _silu(x):
    return x * _sigmoid(x)


def _softplus(x):
    return jnp.maximum(x, 0.0) + jnp.log1p(jnp.exp(-jnp.abs(x)))


def _rmsnorm(x, w):
    return x * lax.rsqrt(jnp.mean(x * x, axis=-1, keepdims=True) + EPS) * w


def _rope_tables(pos):
    pos = np.asarray(pos, np.float32)
    r = pos.shape[0]
    inv_freq = np.float32(ROPE_THETA) ** (-np.arange(0, ROPE_DIM, 2, dtype=np.float32) / ROPE_DIM)
    ang = pos[:, None] * inv_freq[None, :]
    cos, sin = np.cos(ang), np.sin(ang)
    half = ROPE_DIM // 2
    rest = A_DH - ROPE_DIM
    c = np.concatenate([cos, cos, np.ones((r, rest), np.float32)], axis=1)
    sa = np.concatenate([np.zeros((r, half), np.float32), sin, np.zeros((r, rest), np.float32)], axis=1)
    sb = np.concatenate([-sin, np.zeros((r, half + rest), np.float32)], axis=1)
    return tuple(jnp.asarray(np.tile(t, (1, LANES // A_DH)).astype(np.float32)) for t in (c, sa, sb))


def _rope_tile(t, c, sa, sb):
    out = []
    for i in range(t.shape[1] // LANES):
        x = t[:, i * LANES:(i + 1) * LANES]
        out.append(x * c + pltpu.roll(x, ROPE_DIM // 2, 1) * sa
                   + pltpu.roll(x, LANES - ROPE_DIM // 2, 1) * sb)
    return jnp.concatenate(out, axis=1) if len(out) > 1 else out[0]


IP_TM = 1024
IP_TN = 1024
IP_ROWS = 256
K_TILE, V_TILE = O_KA // IP_TN, O_VA // IP_TN
assert O_VA - O_KA == IP_TN and O_ZA - O_VA == IP_TN


def _inproj_kernel(x_ref, nw_ref, w_ref, wt_ref, c_ref, sa_ref, sb_ref, km_ref, vm_ref,
                   h_ref, ht_ref, kr_ref, vr_ref, xn_ref, rows_ref, sem_ref):
    i = pl.program_id(0)
    j = pl.program_id(1)
    per_seq = SEQ // IP_TM
    seq = i // per_seq
    row0 = (i % per_seq) * IP_TM
    first_tile = (i % per_seq) == 0

    @pl.when(j == 0)
    def _():
        def body(r, carry):
            rows = pl.ds(pl.multiple_of(r * IP_ROWS, IP_ROWS), IP_ROWS)
            xn_ref[rows, :] = _rmsnorm(x_ref[rows, :], nw_ref[...]).astype(BF16)
            return carry
        lax.fori_loop(0, IP_TM // IP_ROWS, body, 0)
        ht_ref[...] = _dot_nt(xn_ref[...], wt_ref[...])

    h_ref[...] = _dot_nt(xn_ref[...], w_ref[...])

    @pl.when(j < O_VA // IP_TN)
    def _():
        def body(r, carry):
            rows = pl.ds(pl.multiple_of(r * IP_ROWS, IP_ROWS), IP_ROWS)
            h_ref[rows, :] = _rope_tile(h_ref[rows, :], c_ref[rows, :], sa_ref[rows, :], sb_ref[rows, :])
            return carry
        lax.fori_loop(0, IP_TM // IP_ROWS, body, 0)

    def row_copies(dst_ref, k):
        return [pltpu.make_async_copy(rows_ref.at[k, :, pl.ds(h * LANES, LANES)],
                                      dst_ref.at[seq, pl.ds(N_META + row0, IP_TM), h, :], sem_ref.at[k])
                for h in range(A_HEADS)]

    def meta_copy(src_ref, dst_ref, sem):
        return pltpu.make_async_copy(src_ref, dst_ref.at[seq, pl.ds(0, N_META)], sem)

    def start_rows(meta_ref, dst_ref, k):
        def body(r, carry):
            rows = pl.ds(pl.multiple_of(r * IP_ROWS, IP_ROWS), IP_ROWS)
            rows_ref[k, rows, :] = h_ref[rows, :]
            return carry
        lax.fori_loop(0, IP_TM // IP_ROWS, body, 0)
        for cp in row_copies(dst_ref, k):
            cp.start()

        @pl.when(first_tile)
        def _():
            meta_copy(meta_ref, dst_ref, sem_ref.at[2 + k]).start()

    def wait_rows(meta_ref, dst_ref, k):
        for cp in row_copies(dst_ref, k):
            cp.wait()

        @pl.when(first_tile)
        def _():
            meta_copy(meta_ref, dst_ref, sem_ref.at[2 + k]).wait()

    @pl.when(j == K_TILE)
    def _():
        start_rows(km_ref, kr_ref, 0)

    @pl.when(j == V_TILE)
    def _():
        start_rows(vm_ref, vr_ref, 1)
        wait_rows(km_ref, kr_ref, 0)

    @pl.when(j == V_TILE + 1)
    def _():
        wait_rows(vm_ref, vr_ref, 1)


def _inproj(x, nw, w_bf, w_tail, tabs, k_meta, v_meta):
    m = x.shape[0]
    per_seq = SEQ // IP_TM
    tab_spec = pl.BlockSpec((IP_TM, LANES), lambda i, j: (i % per_seq, 0))
    meta_spec = pl.BlockSpec((N_META, A_HEADS, LANES), lambda i, j: (0, 0, 0))
    rows_shape = jax.ShapeDtypeStruct((m // SEQ, N_META + SEQ, A_HEADS, LANES), F32)
    return pl.pallas_call(
        _inproj_kernel,
        grid=(m // IP_TM, N_MAIN // IP_TN),
        in_specs=[
            pl.BlockSpec((IP_TM, D_MODEL), lambda i, j: (i, 0)),
            pl.BlockSpec((1, D_MODEL), lambda i, j: (0, 0)),
            pl.BlockSpec((IP_TN, D_MODEL), lambda i, j: (j, 0)),
            pl.BlockSpec((LANES, D_MODEL), lambda i, j: (0, 0)),
            tab_spec, tab_spec, tab_spec, meta_spec, meta_spec,
        ],
        out_specs=[
            pl.BlockSpec((IP_TM, IP_TN), lambda i, j: (i, j)),
            pl.BlockSpec((IP_TM, LANES), lambda i, j: (i, 0)),
            pl.BlockSpec(memory_space=pl.ANY), pl.BlockSpec(memory_space=pl.ANY),
        ],
        out_shape=[jax.ShapeDtypeStruct((m, N_MAIN), F32),
                   jax.ShapeDtypeStruct((m, LANES), F32), rows_shape, rows_shape],
        scratch_shapes=[pltpu.VMEM((IP_TM, D_MODEL), BF16), pltpu.VMEM((2, IP_TM, IP_TN), F32),
                        pltpu.SemaphoreType.DMA((4,))],
        compiler_params=pltpu.CompilerParams(
            dimension_semantics=("arbitrary", "arbitrary"), vmem_limit_bytes=VMEM_LIMIT),
        name="inproj",
    )(x, nw, w_bf, w_tail, *tabs, k_meta, v_meta)


AUX_ROWS = 64
AUX_TN = 512


def _aux_inproj_kernel(x_ref, nw_ref, w_ref, c_ref, sa_ref, sb_ref, o_ref, wbf_ref, wtail_ref, xs_ref):
    j = pl.program_id(0)

    @pl.when(j == 0)
    def _():
        xn = _rmsnorm(x_ref[...], nw_ref[...])
        hi = xn.astype(BF16)
        xs_ref[0:AUX_ROWS, :] = hi
        xs_ref[AUX_ROWS:2 * AUX_ROWS, :] = (xn - hi.astype(F32)).astype(BF16)

    w = w_ref[...]
    w_hi = w.astype(BF16)
    wbf_ref[...] = w_hi

    @pl.when(j == pl.num_programs(0) - 1)
    def _():
        row = lax.broadcasted_iota(jnp.int32, (LANES, D_MODEL), 0)
        wtail_ref[...] = jnp.where(row < D_IN - N_MAIN, w_hi[0:LANES, :], jnp.zeros((), BF16))

    w_lo = (w - w_hi.astype(F32)).astype(BF16)
    r1 = _dot_nt(xs_ref[...], w_hi)
    r2 = _dot_nt(xs_ref[0:AUX_ROWS, :], w_lo)
    acc = r1[0:AUX_ROWS] + (r1[AUX_ROWS:] + r2)
    is_rope = j < O_VA // AUX_TN

    @pl.when(is_rope)
    def _():
        o_ref[...] = _rope_tile(acc, c_ref[...], sa_ref[...], sb_ref[...])

    @pl.when(jnp.logical_not(is_rope))
    def _():
        o_ref[...] = acc


def _aux_inproj(xa, nw, w, tabs):
    tab_spec = pl.BlockSpec((AUX_ROWS, LANES), lambda j: (0, 0))
    return pl.pallas_call(
        _aux_inproj_kernel,
        grid=(pl.cdiv(D_IN, AUX_TN),),
        in_specs=[
            pl.BlockSpec((AUX_ROWS, D_MODEL), lambda j: (0, 0)),
            pl.BlockSpec((1, D_MODEL), lambda j: (0, 0)),
            pl.BlockSpec((AUX_TN, D_MODEL), lambda j: (j, 0)),
            tab_spec, tab_spec, tab_spec,
        ],
        out_specs=[pl.BlockSpec((AUX_ROWS, AUX_TN), lambda j: (0, j)),
                   pl.BlockSpec((AUX_TN, D_MODEL), lambda j: (j, 0)),
                   pl.BlockSpec((LANES, D_MODEL), lambda j: (0, 0))],
        out_shape=[jax.ShapeDtypeStruct((AUX_ROWS, D_IN), F32),
                   jax.ShapeDtypeStruct((D_IN, D_MODEL), BF16),
                   jax.ShapeDtypeStruct((LANES, D_MODEL), BF16)],
        scratch_shapes=[pltpu.VMEM((2 * AUX_ROWS, D_MODEL), BF16)],
        compiler_params=pltpu.CompilerParams(
            dimension_semantics=("arbitrary",), vmem_limit_bytes=VMEM_LIMIT),
        name="aux_inproj",
    )(xa, nw, w, *tabs)


AT_TQ = 512
AT_TK = 512
AT_HP = 2
LOG2E = math.log2(math.e)


def _diff_lambda(lq1, lk1, lq2, lk2):
    a = jnp.exp(jnp.sum(lq1 * lk1, axis=-1, keepdims=True))
    b = jnp.exp(jnp.sum(lq2 * lk2, axis=-1, keepdims=True))
    return a - b + LAMBDA_INIT


def _attn_kernel(lq1_ref, lk1_ref, lq2_ref, lk2_ref, q_ref, k_ref, v_ref, km_ref, vm_ref,
                 za_ref, sw_ref, o_ref, kb_ref, vb_ref, kmb_ref, vmb_ref, q2_ref, s_ref, sm_ref,
                 m_ref, l_ref, acc_ref):
    qi = pl.program_id(2)
    tq, tk = AT_TQ, AT_TK
    n_tiles = tk // LANES
    heads = range(AT_HP)
    hl = lambda h: slice(h * LANES, (h + 1) * LANES)

    @pl.when(qi == 0)
    def _():
        pad = jnp.zeros((LANES - N_META, LANES), BF16)
        for h in heads:
            kb_ref[h] = k_ref[:, hl(h)].astype(BF16)
            vb_ref[h] = v_ref[:, hl(h)].astype(BF16)
            kmb_ref[h] = jnp.concatenate([km_ref[:, hl(h)].astype(BF16), pad], axis=0)
            vmb_ref[h] = jnp.concatenate([vm_ref[:, hl(h)].astype(BF16), pad], axis=0)

    lane = lax.broadcasted_iota(jnp.int32, (tq, LANES), 1)
    for h in heads:
        q = q_ref[:, hl(h)] * (A_DH ** -0.5 * LOG2E)
        q2_ref[h, 0:tq, :] = jnp.where(lane < A_DH, q, 0.0).astype(BF16)
        q2_ref[h, tq:2 * tq, :] = jnp.where(lane >= A_DH, q, 0.0).astype(BF16)

    def tile_max(m, s):
        for c in range(s.shape[1] // LANES):
            m = jnp.maximum(m, s[:, c * LANES:(c + 1) * LANES])
        return m

    lane2 = lax.broadcasted_iota(jnp.int32, (2 * tq, LANES), 1)
    sm = [jnp.where(lane2 < N_META, _dot_nt(q2_ref[h], kmb_ref[h]), NEG) for h in heads]
    for h in heads:
        sm_ref[h] = sm[h]
        m_ref[h] = sm[h]

    def scores(j):
        rows = pl.ds(pl.multiple_of(j * tk, tk), tk)
        return [_dot_nt(q2_ref[h], kb_ref[h, rows, :]) for h in heads]

    def pass1(j, carry):
        s = scores(j)
        for h in heads:
            s_ref[h, j] = s[h]
            m_ref[h] = tile_max(m_ref[h], s[h])
        return carry
    lax.fori_loop(0, qi, pass1, 0)

    r = lax.broadcasted_iota(jnp.int32, (2 * tq, tk), 0)
    r = jnp.where(r >= tq, r - tq, r)
    c = lax.broadcasted_iota(jnp.int32, (2 * tq, tk), 1)
    causal = c <= r
    s = [jnp.where(causal, t, NEG) for t in scores(qi)]
    for h in heads:
        s_ref[h, qi] = s[h]
    m = [jnp.max(tile_max(m_ref[h], s[h]), axis=-1, keepdims=True) for h in heads]
    for h in heads:
        m_ref[h] = jnp.broadcast_to(m[h], (2 * tq, LANES))

    p = [jnp.exp2(sm_ref[h] - m_ref[h]) for h in heads]
    for h in heads:
        l_ref[h] = p[h]
        acc_ref[h] = _dot(p[h].astype(BF16), vmb_ref[h])

    def pass2(j, carry):
        rows = pl.ds(pl.multiple_of(j * tk, tk), tk)
        pb = []
        for h in heads:
            mb = m_ref[h]
            lsum = l_ref[h]
            ps = []
            for t in range(n_tiles):
                p = jnp.exp2(s_ref[h, j, :, t * LANES:(t + 1) * LANES] - mb)
                lsum = lsum + p
                ps.append(p.astype(BF16))
            l_ref[h] = lsum
            pb.append(jnp.concatenate(ps, axis=1))
        pv = [_dot(pb[h], vb_ref[h, rows, :]) for h in heads]
        for h in heads:
            acc_ref[h] += pv[h]
        return carry
    lax.fori_loop(0, qi + 1, pass2, 0)

    lam = _diff_lambda(lq1_ref[...], lk1_ref[...], lq2_ref[...], lk2_ref[...])
    for h in heads:
        l = jnp.sum(l_ref[h], axis=-1, keepdims=True)
        o = acc_ref[h, 0:tq, :] / l[0:tq] - lam * (acc_ref[h, tq:2 * tq, :] / l[tq:2 * tq])
        y = _rmsnorm(o, sw_ref[...]) * (1.0 - LAMBDA_INIT)
        o_ref[:, hl(h)] = (y * _silu(za_ref[:, hl(h)])).astype(BF16)


def _attention(lams, h_main, k_meta, v_meta, subln_w):
    nq = SEQ // AT_TQ
    w = AT_HP * LANES
    lam_spec = pl.BlockSpec((1, A_DH), lambda b, g, i: (0, 0))
    return pl.pallas_call(
        _attn_kernel,
        grid=(BATCH, A_HEADS // AT_HP, nq),
        in_specs=[
            lam_spec, lam_spec, lam_spec, lam_spec,
            pl.BlockSpec((AT_TQ, w), lambda b, g, i: (b * nq + i, O_QA // w + g)),
            pl.BlockSpec((SEQ, w), lambda b, g, i: (b, O_KA // w + g)),
            pl.BlockSpec((SEQ, w), lambda b, g, i: (b, O_VA // w + g)),
            pl.BlockSpec((N_META, w), lambda b, g, i: (0, g)),
            pl.BlockSpec((N_META, w), lambda b, g, i: (0, g)),
            pl.BlockSpec((AT_TQ, w), lambda b, g, i: (b * nq + i, O_ZA // w + g)),
            pl.BlockSpec((1, A_DV), lambda b, g, i: (0, 0)),
        ],
        out_specs=pl.BlockSpec((AT_TQ, w), lambda b, g, i: (b * nq + i, g)),
        out_shape=jax.ShapeDtypeStruct((BATCH * SEQ, A_HEADS * A_DV), BF16),
        scratch_shapes=[
            pltpu.VMEM((AT_HP, SEQ, LANES), BF16), pltpu.VMEM((AT_HP, SEQ, LANES), BF16),
            pltpu.VMEM((AT_HP, LANES, LANES), BF16), pltpu.VMEM((AT_HP, LANES, LANES), BF16),
            pltpu.VMEM((AT_HP, 2 * AT_TQ, LANES), BF16),
            pltpu.VMEM((AT_HP, SEQ // AT_TK, 2 * AT_TQ, AT_TK), F32),
            pltpu.VMEM((AT_HP, 2 * AT_TQ, LANES), F32),
            pltpu.VMEM((AT_HP, 2 * AT_TQ, LANES), F32), pltpu.VMEM((AT_HP, 2 * AT_TQ, LANES), F32),
            pltpu.VMEM((AT_HP, 2 * AT_TQ, A_DV), F32),
        ],
        compiler_params=pltpu.CompilerParams(
            dimension_semantics=("arbitrary", "arbitrary", "arbitrary"),
            vmem_limit_bytes=VMEM_LIMIT),
        name="diff_attn",
    )(*lams, h_main, h_main, h_main, k_meta, v_meta, h_main, subln_w)


def _unit_lower_inverse_minus_eye(a_list, i_idx, j_idx):
    base = 8
    diag = (i_idx // base) == (j_idx // base)
    b = [jnp.where(diag, a, 0.0) for a in a_list]
    n = [-x for x in b]
    for _ in range(2):
        bb = [x.astype(BF16) for x in b]
        b = [_dot(x, x) for x in bb]
        nb = [_dot(x.astype(BF16), y.astype(BF16)) for x, y in zip(n, b)]
        n = [x + y + z for x, y, z in zip(n, b, nb)]
    s = base
    while s < CHUNK:
        join = ((i_idx // (2 * s)) == (j_idx // (2 * s))) & ((i_idx // s) % 2 == 1) & ((j_idx // s) % 2 == 0)
        a_s = [jnp.where(join, a, 0.0) for a in a_list]
        x = [p + _dot(q.astype(BF16), p.astype(BF16)) for p, q in zip(a_s, n)]
        xn = [_dot(p.astype(BF16), q.astype(BF16)) for p, q in zip(x, n)]
        n = [q - (p + r) for q, p, r in zip(n, x, xn)]
        s *= 2
    return n


def _gdn_kernel(xq_ref, xk_ref, xv_ref, ab_ref, abt_ref, zb_ref, cw_ref, alog_ref, dtb_ref,
                alogc_ref, dtbc_ref, gw_ref, halo_ref, s0_ref, yb_ref, sfin_ref, ext_ref, s_ref,
                *, masked_rows):
    c = pl.program_id(1)
    hist = 8

    @pl.when(c == 0)
    def _():
        ext_ref[0:hist, :] = halo_ref[...]
        s_ref[...] = s0_ref[...]

    x = jnp.concatenate([xq_ref[...], xk_ref[...], xv_ref[...]], axis=1)
    ext_ref[hist:hist + CHUNK, :] = x
    y = cw_ref[CONV_W - 1:CONV_W, :] * x
    for t in range(CONV_W - 1):
        lo = hist - (CONV_W - 1) + t
        y = y + cw_ref[t:t + 1, :] * ext_ref[lo:lo + CHUNK, :]
    y = _silu(y)
    ext_ref[0:hist, :] = x[CHUNK - hist:CHUNK, :]

    i_idx = lax.broadcasted_iota(jnp.int32, (CHUNK, CHUNK), 0)
    j_idx = lax.broadcasted_iota(jnp.int32, (CHUNK, CHUNK), 1)
    tril = i_idx >= j_idx
    strict = i_idx > j_idx
    tril_f = tril.astype(F32)
    triu_f = (i_idx <= j_idx).astype(F32)

    ab = ab_ref[...]
    g_c = -jnp.exp(alog_ref[...]) * _softplus(ab + dtb_ref[...])
    beta_c = _sigmoid(ab)
    abt = abt_ref[...]
    g_r = -jnp.exp(alogc_ref[...]) * _softplus(abt + dtbc_ref[...])
    if masked_rows:
        row_ok = lax.broadcasted_iota(jnp.int32, (CHUNK, LANES), 0) >= masked_rows
        col_ok = lax.broadcasted_iota(jnp.int32, (2 * B_HEADS, CHUNK), 1) >= masked_rows
        g_c = jnp.where(row_ok, g_c, 0.0)
        beta_c = jnp.where(row_ok, beta_c, 0.0)
        g_r = jnp.where(col_ok, g_r, 0.0)
    gc_c = _dot_f32(tril_f, g_c)
    gc_r = _dot_f32(g_r, triu_f)

    nk = B_HEADS * B_DK
    heads = range(B_HEADS)

    def l2n(t):
        return t * lax.rsqrt(jnp.sum(t * t, axis=-1, keepdims=True) + EPS)

    qn = [l2n(y[:, h * B_DK:(h + 1) * B_DK]) * (B_DK ** -0.5) for h in heads]
    kn = [l2n(y[:, nk + h * B_DK:nk + (h + 1) * B_DK]) for h in heads]
    vh = [y[:, 2 * nk + h * B_DV:2 * nk + (h + 1) * B_DV] for h in heads]
    bcol = [beta_c[:, B_HEADS + h:B_HEADS + h + 1] for h in heads]
    gcc = [gc_c[:, h:h + 1] for h in heads]
    decay = [jnp.where(tril, jnp.exp(jnp.where(tril, gcc[h] - gc_r[h:h + 1, :], 0.0)), 0.0)
             for h in heads]
    kbeta = [kn[h] * bcol[h] for h in heads]
    kn_b = [t.astype(BF16) for t in kn]
    kk = [_dot_nt(kbeta[h].astype(BF16), kn_b[h]) for h in heads]
    qk = [_dot_nt(qn[h].astype(BF16), kn_b[h]) for h in heads]
    a = [jnp.where(strict, kk[h] * decay[h], 0.0) for h in heads]
    n = _unit_lower_inverse_minus_eye(a, i_idx, j_idx)
    egc = [jnp.exp(t) for t in gcc]
    rhs = [jnp.concatenate([vh[h] * bcol[h], kbeta[h] * egc[h]], axis=1) for h in heads]
    nr = [_dot(n[h].astype(BF16), rhs[h].astype(BF16)) for h in heads]
    sol = [rhs[h] + nr[h] for h in heads]
    st = [s_ref[h] for h in heads]
    st_b = [t.astype(BF16) for t in st]
    ws = [_dot(sol[h][:, B_DV:B_DV + B_DK].astype(BF16), st_b[h]) for h in heads]
    qs = [_dot((qn[h] * egc[h]).astype(BF16), st_b[h]) for h in heads]
    v_new_b = [(sol[h][:, 0:B_DV] - ws[h]).astype(BF16) for h in heads]
    av = [_dot((qk[h] * decay[h]).astype(BF16), v_new_b[h]) for h in heads]
    g_last = [t[CHUNK - 1:CHUNK, :] for t in gcc]
    ke_t = [(kn[h] * jnp.exp(g_last[h] - gcc[h])).T.astype(BF16) for h in heads]
    kv = [_dot(ke_t[h], v_new_b[h]) for h in heads]
    for h in heads:
        s_ref[h] = st[h] * jnp.exp(g_last[h]) + kv[h]
    for h in heads:
        zb = zb_ref[:, h * B_DV:(h + 1) * B_DV]
        yb_ref[:, h * B_DV:(h + 1) * B_DV] = (_rmsnorm(qs[h] + av[h], gw_ref[...]) * _silu(zb)).astype(BF16)

    @pl.when(c == pl.num_programs(1) - 1)
    def _():
        sfin_ref[...] = s_ref[...]


def _gdn_chunks(x, x_col, ab, ab_col, abt, zsrc, zb_col, cw, gvecs, gw, halo, s0, n_seq, n_chunk,
                masked_rows):
    alog_l, dtb_l, alog_c, dtb_c = gvecs
    const2 = lambda b, c: (0, 0)
    kern = functools.partial(_gdn_kernel, masked_rows=masked_rows)
    part = B_QKV // 3

    def x_spec(k):
        return pl.BlockSpec((CHUNK, part), lambda b, c: (b * n_chunk + c, x_col + k))

    return pl.pallas_call(
        kern,
        grid=(n_seq, n_chunk),
        in_specs=[
            x_spec(0), x_spec(1), x_spec(2),
            pl.BlockSpec((CHUNK, LANES), lambda b, c: (b * n_chunk + c, ab_col)),
            pl.BlockSpec((None, None, 2 * B_HEADS, CHUNK), lambda b, c: (b, c, 0, 0)),
            pl.BlockSpec((CHUNK, B_HEADS * B_DV), lambda b, c: (b * n_chunk + c, zb_col)),
            pl.BlockSpec((CONV_W, B_QKV), const2),
            pl.BlockSpec((1, LANES), const2), pl.BlockSpec((1, LANES), const2),
            pl.BlockSpec((2 * B_HEADS, 1), const2), pl.BlockSpec((2 * B_HEADS, 1), const2),
            pl.BlockSpec((1, B_DV), const2),
            pl.BlockSpec((8, B_QKV), const2),
            pl.BlockSpec((B_HEADS, B_DK, B_DV), lambda b, c: (0, 0, 0)),
        ],
        out_specs=[
            pl.BlockSpec((CHUNK, B_HEADS * B_DV), lambda b, c: (b * n_chunk + c, 0)),
            pl.BlockSpec((None, B_HEADS, B_DK, B_DV), lambda b, c: (b, 0, 0, 0)),
        ],
        out_shape=[jax.ShapeDtypeStruct((n_seq * n_chunk * CHUNK, B_HEADS * B_DV), BF16),
                   jax.ShapeDtypeStruct((n_seq, B_HEADS, B_DK, B_DV), F32)],
        scratch_shapes=[pltpu.VMEM((8 + CHUNK, B_QKV), F32),
                        pltpu.VMEM((B_HEADS, B_DK, B_DV), F32)],
        compiler_params=pltpu.CompilerParams(
            dimension_semantics=("arbitrary", "arbitrary"), vmem_limit_bytes=VMEM_LIMIT),
        name="gdn_chunks",
    )(x, x, x, ab, abt, zsrc, cw, alog_l, dtb_l, alog_c, dtb_c, gw, halo, s0)


def _shift_lanes(x, k):
    nblk = x.shape[1] // LANES
    r = [pltpu.roll(x[:, c * LANES:(c + 1) * LANES], LANES - k, 1) for c in range(nblk)]
    lane = lax.broadcasted_iota(jnp.int32, (x.shape[0], LANES), 1)
    return jnp.concatenate([jnp.where(lane < LANES - k, r[c], r[c + 1]) for c in range(nblk - 1)],
                           axis=1)


def _merge_kernel(x_ref, ya_ref, yb_ref, g4_ref, g5_ref, gt_ref, wpa_ref, wpb_ref, wo_ref, fw_ref,
                  y_ref):
    g5 = g5_ref[...]
    ga = _shift_lanes(jnp.concatenate([g4_ref[...], g5[:, 0:LANES]], axis=1), GATE_SHIFT)
    gb = _shift_lanes(jnp.concatenate([g5, gt_ref[...]], axis=1), GATE_SHIFT)
    pa = _dot(ya_ref[...], wpa_ref[...])
    pb = _dot(yb_ref[...], wpb_ref[...])
    mixed = _sigmoid(ga) * pa + _sigmoid(gb) * pb
    hp = x_ref[...] + _dot(mixed.astype(BF16), wo_ref[...])
    y_ref[...] = _rmsnorm(hp, fw_ref[...])


def _merge(x, ya, yb, hsrc, htail, wpa, wpb, wo, fw, tm):
    m = x.shape[0]
    once = pl.Buffered(1)
    return pl.pallas_call(
        _merge_kernel,
        grid=(m // tm,),
        in_specs=[
            pl.BlockSpec((tm, D_MODEL), lambda i: (i, 0)),
            pl.BlockSpec((tm, A_HEADS * A_DV), lambda i: (i, 0)),
            pl.BlockSpec((tm, B_HEADS * B_DV), lambda i: (i, 0)),
            pl.BlockSpec((tm, D_MODEL), lambda i: (i, O_A // D_MODEL)),
            pl.BlockSpec((tm, D_MODEL), lambda i: (i, O_A // D_MODEL + 1)),
            pl.BlockSpec((tm, LANES), lambda i: (i, 0)),
            pl.BlockSpec((A_HEADS * A_DV, D_MODEL), lambda i: (0, 0), pipeline_mode=once),
            pl.BlockSpec((B_HEADS * B_DV, D_MODEL), lambda i: (0, 0), pipeline_mode=once),
            pl.BlockSpec((D_MODEL, D_MODEL), lambda i: (0, 0), pipeline_mode=once),
            pl.BlockSpec((1, D_MODEL), lambda i: (0, 0)),
        ],
        out_specs=pl.BlockSpec((tm, D_MODEL), lambda i: (i, 0)),
        out_shape=jax.ShapeDtypeStruct((m, D_MODEL), F32),
        compiler_params=pltpu.CompilerParams(
            dimension_semantics=("arbitrary",), vmem_limit_bytes=VMEM_LIMIT),
        name="merge",
    )(x, ya, yb, hsrc, hsrc, htail, wpa, wpb, wo, fw)


DA_GRP = 8
HALF_TOK = PAGE_SIZE // 2
HALF_ROWS = HALF_TOK * A_HEADS
TOK_TILES = HALF_ROWS // LANES


def _lane_group_reduce(x, op):
    s = A_HEADS
    while s < LANES:
        x = op(x, pltpu.roll(x, s, 1))
        s *= 2
    return x


FD_PAGES = N_PAGES // 2
FD_STEPS = 4
FD_NBUF = 2 * FD_PAGES


def _gdn_decode_kernel(pt_ref, xq_ref, xk_ref, xv_ref, ab_ref, abt_ref, zb_ref, cw_ref, alog_ref,
                       dtb_ref, alogc_ref, dtbc_ref, gw_ref, halo_ref, s0_ref,
                       lq1_ref, lk1_ref, lq2_ref, lk2_ref, q_ref, kn_ref, vn_ref, za_ref, sw_ref,
                       ck_ref, cv_ref, yb_ref, sfin_ref, o_ref,
                       ext_ref, s_ref, buf_ref, sem_ref, sc_ref, acc_ref, wn_ref):
    n_chunk = pl.num_programs(1)
    t = pl.program_id(0) * n_chunk + pl.program_id(1)
    n_steps = pl.num_programs(0) * n_chunk
    role = t % FD_STEPS

    def start_step(step):
        seq = step // FD_STEPS
        r = step % FD_STEPS
        half = (step % 2) * FD_PAGES
        first = (r % 2) * FD_PAGES

        def copies(src_ref):
            for s in range(FD_PAGES):
                pltpu.make_async_copy(src_ref.at[0, pt_ref[seq, first + s]], buf_ref.at[half + s],
                                      sem_ref.at[half + s]).start(priority=s % 2)

        @pl.when(r < 2)
        def _():
            copies(ck_ref)

        @pl.when(r >= 2)
        def _():
            copies(cv_ref)

    def wait_slot(slot):
        pltpu.make_async_copy(ck_ref.at[0, 0], buf_ref.at[slot], sem_ref.at[slot]).wait()

    @pl.when(t == 0)
    def _():
        start_step(0)

    @pl.when(t + 1 < n_steps)
    def _():
        start_step(t + 1)

    _gdn_kernel(xq_ref, xk_ref, xv_ref, ab_ref, abt_ref, zb_ref, cw_ref, alog_ref, dtb_ref,
                alogc_ref, dtbc_ref, gw_ref, halo_ref, s0_ref, yb_ref, sfin_ref, ext_ref, s_ref,
                masked_rows=0)

    half = (t % 2) * FD_PAGES
    sub = lax.broadcasted_iota(jnp.int32, (A_HEADS, LANES), 0)
    lane = lax.broadcasted_iota(jnp.int32, (A_HEADS, LANES), 1)
    hmask = (lane % A_HEADS) == sub
    first_half = sub < 2
    second_half = (sub >= 2) & (sub < 4)
    map0 = (sub % 2) == 0
    sub2 = lax.broadcasted_iota(jnp.int32, (A_HEADS, 2 * LANES), 0)
    lane2 = lax.broadcasted_iota(jnp.int32, (A_HEADS, 2 * LANES), 1)
    half_sel = ((sub2 < 4) & (lane2 // A_DH == sub2)).astype(BF16)
    n_grp = FD_PAGES // DA_GRP

    def halves_on_lanes(page):
        return jnp.concatenate([page[0:HALF_TOK].reshape(HALF_ROWS, LANES),
                                page[HALF_TOK:].reshape(HALF_ROWS, LANES)], axis=1)

    def pair_halves(x, op):
        up = pltpu.roll(x, 2, 0)
        dn = pltpu.roll(x, A_HEADS - 2, 0)
        return jnp.where(first_half, op(x, dn), jnp.where(second_half, op(x, up), 0.0))

    @pl.when(role < 2)
    def _():
        q = q_ref[...]

        def k_group(g, carry):
            slots = [half + g * DA_GRP + i for i in range(DA_GRP)]
            for s in slots:
                wait_slot(s)
            prods = [halves_on_lanes(buf_ref[s] * q[None]).astype(BF16) for s in slots]
            scs = [_dot_nt(half_sel, p) for p in prods]
            for i, sc in enumerate(scs):
                base = (role * FD_PAGES + g * DA_GRP + i) * TOK_TILES
                for c in range(TOK_TILES):
                    sc_ref[base + c] = sc[:, c * LANES:(c + 1) * LANES]
            return carry
        lax.fori_loop(0, n_grp, k_group, 0)

    @pl.when(role == 1)
    def _():
        lam = _diff_lambda(lq1_ref[...], lk1_ref[...], lq2_ref[...], lk2_ref[...])
        prod = q_ref[...] * kn_ref[...]
        hs1 = jnp.sum(jnp.where(lane < A_DH, prod, 0.0), axis=1, keepdims=True)
        hs2 = jnp.sum(jnp.where(lane >= A_DH, prod, 0.0), axis=1, keepdims=True)
        row1 = jnp.sum(jnp.where(hmask, hs1, 0.0), axis=0, keepdims=True)
        row2 = jnp.sum(jnp.where(hmask, hs2, 0.0), axis=0, keepdims=True)
        s_new = jnp.where(sub < 4, jnp.where(map0, row1, row2), 0.0)
        sc = sc_ref[...]
        mx = pair_halves(_lane_group_reduce(jnp.max(sc, axis=0), jnp.maximum), jnp.maximum)
        mx = jnp.maximum(mx, s_new)
        p = jnp.exp(sc - mx[None])
        p_new = jnp.exp(s_new - mx)
        den = pair_halves(_lane_group_reduce(jnp.sum(p, axis=0), jnp.add), jnp.add) + p_new
        coef = jnp.where(sub < 4, jnp.where(map0, 1.0 / den, -lam / den), 0.0)
        sc_ref[...] = p * coef[None]
        wn_ref[...] = jnp.sum(jnp.where(first_half, p_new * coef, 0.0), axis=0, keepdims=True)
        acc_ref[...] = jnp.zeros_like(acc_ref)

    @pl.when(role >= 2)
    def _():
        def page_weights(page):
            wa, wb = [], []
            for c in range(TOK_TILES):
                tile = sc_ref[page * TOK_TILES + c]
                ra = jnp.sum(jnp.where(first_half, tile, 0.0), axis=0, keepdims=True)
                rb = jnp.sum(jnp.where(second_half, tile, 0.0), axis=0, keepdims=True)
                wa.append(jnp.where(hmask, ra, 0.0))
                wb.append(jnp.where(hmask, rb, 0.0))
            w = jnp.concatenate([jnp.concatenate(wa, axis=1), jnp.concatenate(wb, axis=1)], axis=0)
            w_hi = w.astype(BF16)
            w_lo = (w - w_hi.astype(F32)).astype(BF16)
            return jnp.concatenate([w_hi, w_lo], axis=0)

        def v_group(g, acc):
            first = (role - 2) * FD_PAGES + g * DA_GRP
            ws = [page_weights(first + i) for i in range(DA_GRP)]
            slots = [half + g * DA_GRP + i for i in range(DA_GRP)]
            for s in slots:
                wait_slot(s)
            vs = [halves_on_lanes(buf_ref[s]).astype(BF16) for s in slots]
            rs = [_dot(w, v) for w, v in zip(ws, vs)]
            for r in rs:
                acc = acc + ((r[0:8, 0:LANES] + r[8:16, LANES:]) + (r[16:24, 0:LANES] + r[24:32, LANES:]))
            return acc
        acc_ref[...] = lax.fori_loop(0, n_grp, v_group, acc_ref[...])

    @pl.when(role == FD_STEPS - 1)
    def _():
        w_new = jnp.sum(jnp.where(hmask, wn_ref[...], 0.0), axis=1, keepdims=True) * (A_HEADS / LANES)
        o = acc_ref[...] + w_new * vn_ref[...]
        y = _rmsnorm(o, sw_ref[...]) * (1.0 - LAMBDA_INIT)
        o_ref[...] = y * _silu(za_ref[...])


def _gdn_decode(page_table, x, ab_src, abt, cw, gvecs, gw, halo, s0,
                lams, q_s, k_new, v_new, za_s, subln_w, cache_k, cache_v):
    alog_l, dtb_l, alog_c, dtb_c = gvecs
    n_chunk = SEQ // CHUNK
    assert BATCH * n_chunk == DEC_BATCH * FD_STEPS
    part = B_QKV // 3
    row = lambda b, c, pt: (b * n_chunk + c, 0)
    const2 = lambda b, c, pt: (0, 0)
    seq3 = lambda b, c, pt: ((b * n_chunk + c) // FD_STEPS, 0, 0)

    def col(width, k):
        return pl.BlockSpec((CHUNK, width), lambda b, c, pt: (b * n_chunk + c, k))

    lam_spec = pl.BlockSpec((1, A_DH), const2)
    seq_spec = pl.BlockSpec((None, A_HEADS, LANES), seq3)
    hbm_spec = pl.BlockSpec(memory_space=pl.ANY)
    grid_spec = pltpu.PrefetchScalarGridSpec(
        num_scalar_prefetch=1,
        grid=(BATCH, n_chunk),
        in_specs=[
            col(part, O_QKVB // part), col(part, O_QKVB // part + 1), col(part, O_QKVB // part + 2),
            col(LANES, O_A // LANES),
            pl.BlockSpec((None, None, 2 * B_HEADS, CHUNK), lambda b, c, pt: (b, c, 0, 0)),
            col(part, O_ZB // part),
            pl.BlockSpec((CONV_W, B_QKV), const2),
            pl.BlockSpec((1, LANES), const2), pl.BlockSpec((1, LANES), const2),
            pl.BlockSpec((2 * B_HEADS, 1), const2), pl.BlockSpec((2 * B_HEADS, 1), const2),
            pl.BlockSpec((1, B_DV), const2),
            pl.BlockSpec((8, B_QKV), const2),
            pl.BlockSpec((B_HEADS, B_DK, B_DV), lambda b, c, pt: (0, 0, 0)),
            lam_spec, lam_spec, lam_spec, lam_spec, seq_spec, seq_spec, seq_spec, seq_spec,
            pl.BlockSpec((1, A_DV), const2), hbm_spec, hbm_spec,
        ],
        out_specs=[
            pl.BlockSpec((CHUNK, B_HEADS * B_DV), row),
            pl.BlockSpec((None, B_HEADS, B_DK, B_DV), lambda b, c, pt: (b, 0, 0, 0)),
            seq_spec,
        ],
        scratch_shapes=[
            pltpu.VMEM((8 + CHUNK, B_QKV), F32),
            pltpu.VMEM((B_HEADS, B_DK, B_DV), F32),
            pltpu.VMEM((FD_NBUF, PAGE_SIZE, A_HEADS, LANES), F32),
            pltpu.SemaphoreType.DMA((FD_NBUF,)),
            pltpu.VMEM((N_PAGES * TOK_TILES, A_HEADS, LANES), F32),
            pltpu.VMEM((A_HEADS, A_DV), F32),
            pltpu.VMEM((1, LANES), F32),
        ],
    )
    return pl.pallas_call(
        _gdn_decode_kernel,
        grid_spec=grid_spec,
        out_shape=[jax.ShapeDtypeStruct((BATCH * SEQ, B_HEADS * B_DV), BF16),
                   jax.ShapeDtypeStruct((BATCH, B_HEADS, B_DK, B_DV), F32),
                   jax.ShapeDtypeStruct((DEC_BATCH, A_HEADS, A_DV), F32)],
        compiler_params=pltpu.CompilerParams(
            dimension_semantics=("arbitrary", "arbitrary"), vmem_limit_bytes=VMEM_LIMIT),
        name="gdn_decode",
    )(page_table, x, x, x, ab_src, abt, x, cw, alog_l, dtb_l, alog_c, dtb_c, gw, halo, s0,
      *lams, q_s, k_new, v_new, za_s, subln_w, cache_k, cache_v)


GS_SEQS = 4


def _gdn_step_kernel(sc_ref, x_ref, cw_ref, ab_ref, alog_ref, dtb_ref, zb_ref, gw_ref, s_ref,
                     yb_ref, sout_ref, cout_ref):
    for i in range(GS_SEQS):
        x = x_ref[i]
        y = cw_ref[CONV_W - 1] * x
        for t in range(CONV_W - 1):
            y = y + cw_ref[t] * sc_ref[i, t]
        y = _silu(y)
        for t in range(CONV_W - 2):
            cout_ref[i, t] = sc_ref[i, t + 1]
        cout_ref[i, CONV_W - 2] = x

        q = y[0:B_HEADS]
        k = y[B_HEADS:2 * B_HEADS]
        v = y[2 * B_HEADS:3 * B_HEADS]
        qn = q * lax.rsqrt(jnp.sum(q * q, axis=-1, keepdims=True) + EPS) * (B_DK ** -0.5)
        kn = k * lax.rsqrt(jnp.sum(k * k, axis=-1, keepdims=True) + EPS)
        qt = qn.T
        kt = kn.T
        ab = ab_ref[i]
        g = -jnp.exp(alog_ref[...]) * _softplus(ab + dtb_ref[...])
        beta = _sigmoid(ab)
        for h in range(B_HEADS):
            kcol = kt[:, h:h + 1]
            qcol = qt[:, h:h + 1]
            st = s_ref[i, h] * jnp.exp(g[:, h:h + 1])
            kv = jnp.sum(st * kcol, axis=0, keepdims=True)
            d = (v[h:h + 1] - kv) * beta[:, B_HEADS + h:B_HEADS + h + 1]
            st = st + kcol * d
            sout_ref[i, h] = st
            o = jnp.sum(st * qcol, axis=0, keepdims=True)
            yb_ref[i, h:h + 1, :] = _rmsnorm(o, gw_ref[...]) * _silu(zb_ref[i, h:h + 1, :])


def _gdn_step(state_conv, x, cw, ab, alog_l, dtb_l, zb, gw, state_ssm):
    rows = B_QKV // LANES
    seq3 = lambda b: (b, 0, 0)
    seq4 = lambda b: (b, 0, 0, 0)
    const2 = lambda b: (0, 0)
    return pl.pallas_call(
        _gdn_step_kernel,
        grid=(DEC_BATCH // GS_SEQS,),
        in_specs=[
            pl.BlockSpec((GS_SEQS, CONV_W - 1, rows, LANES), seq4),
            pl.BlockSpec((GS_SEQS, rows, LANES), seq3),
            pl.BlockSpec((CONV_W, rows, LANES), lambda b: (0, 0, 0)),
            pl.BlockSpec((GS_SEQS, 1, LANES), seq3),
            pl.BlockSpec((1, LANES), const2), pl.BlockSpec((1, LANES), const2),
            pl.BlockSpec((GS_SEQS, B_HEADS, B_DV), seq3),
            pl.BlockSpec((1, B_DV), const2),
            pl.BlockSpec((GS_SEQS, B_HEADS, B_DK, B_DV), seq4),
        ],
        out_specs=[
            pl.BlockSpec((GS_SEQS, B_HEADS, B_DV), seq3),
            pl.BlockSpec((GS_SEQS, B_HEADS, B_DK, B_DV), seq4),
            pl.BlockSpec((GS_SEQS, CONV_W - 1, rows, LANES), seq4),
        ],
        out_shape=[jax.ShapeDtypeStruct((DEC_BATCH, B_HEADS, B_DV), F32),
                   jax.ShapeDtypeStruct((DEC_BATCH, B_HEADS, B_DK, B_DV), F32),
                   jax.ShapeDtypeStruct((DEC_BATCH, CONV_W - 1, rows, LANES), F32)],
        compiler_params=pltpu.CompilerParams(
            dimension_semantics=("arbitrary",), vmem_limit_bytes=VMEM_LIMIT),
        name="gdn_step",
    )(state_conv, x, cw, ab, alog_l, dtb_l, zb, gw, state_ssm)


def kernel(x_prompt, x_sample, cache_k, cache_v, state_conv, state_ssm, page_table, meta_tokens,
           norm_w, w_in, lambda_q1, lambda_k1, lambda_q2, lambda_k2, subln_w, conv_w, a_log,
           dt_bias, gdn_norm_w, w_pa, w_pb, w_o, final_norm_w):
    assert x_prompt.shape == (BATCH, SEQ, D_MODEL) and x_sample.shape == (DEC_BATCH, 1, D_MODEL)
    assert w_in.shape == (1, D_MODEL, D_IN) and page_table.shape == (DEC_BATCH, N_PAGES)
    w = w_in[0].T
    nw = norm_w
    lams = (lambda_q1, lambda_k1, lambda_q2, lambda_k2)
    fw = final_norm_w.reshape(1, D_MODEL)
    wpa, wpb, wo = w_pa[0].astype(BF16), w_pb[0].astype(BF16), w_o[0].astype(BF16)
    cw = conv_w[0]

    def lanes8(v, off):
        return jnp.zeros((1, LANES), F32).at[0, off:off + B_HEADS].set(v)

    alog_l, dtb_l = lanes8(a_log[0], 0), lanes8(dt_bias[0], 0)
    alog_c, dtb_c = alog_l[0, 0:2 * B_HEADS].reshape(-1, 1), dtb_l[0, 0:2 * B_HEADS].reshape(-1, 1)
    gvecs = (alog_l, dtb_l, alog_c, dtb_c)

    xa = jnp.concatenate([x_sample[:, 0, :], meta_tokens,
                          jnp.zeros((AUX_ROWS - DEC_BATCH - N_META, D_MODEL), F32)], axis=0)
    pos_a = np.concatenate([np.full((DEC_BATCH,), PAST_LEN), np.arange(N_META),
                            np.zeros((AUX_ROWS - DEC_BATCH - N_META,), np.int64)])
    h_aux, w_bf, w_tail = _aux_inproj(xa, nw, w, _rope_tables(pos_a))
    hs, hm = h_aux[0:DEC_BATCH], h_aux[DEC_BATCH:DEC_BATCH + N_META]

    k_meta, v_meta = hm[:, O_KA:O_VA], hm[:, O_VA:O_ZA]
    tabs_p = _rope_tables(N_META + np.arange(SEQ))
    h_main, h_tail, k_rows_p, v_rows_p = _inproj(
        x_prompt.reshape(BATCH * SEQ, D_MODEL), nw, w_bf, w_tail, tabs_p,
        k_meta.reshape(N_META, A_HEADS, LANES), v_meta.reshape(N_META, A_HEADS, LANES))

    pad_rows = CHUNK - N_META
    x_meta = jnp.pad(hm[:, O_QKVB:O_ZB], ((pad_rows, 0), (0, 0)))
    ab_meta = jnp.pad(hm[:, O_A:O_GA], ((pad_rows, 0), (0, LANES - 2 * B_HEADS)))
    abt_meta = ab_meta[:, 0:2 * B_HEADS].T.reshape(1, 1, 2 * B_HEADS, CHUNK)
    _, s_meta = _gdn_chunks(
        x_meta, 0, ab_meta, 0, abt_meta, jnp.zeros((CHUNK, B_HEADS * B_DV), F32), 0, cw, gvecs,
        gdn_norm_w, jnp.zeros((8, B_QKV), F32), jnp.zeros((B_HEADS, B_DK, B_DV), F32), 1, 1, pad_rows)

    n_chunk = SEQ // CHUNK
    abt = h_main[:, O_A:O_GA].reshape(BATCH, n_chunk, CHUNK, 2 * B_HEADS).transpose(0, 1, 3, 2)
    heads = lambda t: t.reshape(DEC_BATCH, A_HEADS, LANES)
    q_s = heads(hs[:, O_QA:O_KA]) * (A_DH ** -0.5)
    k_s, v_s = heads(hs[:, O_KA:O_VA]), heads(hs[:, O_VA:O_ZA])
    yb, ssm_p, ya_s = _gdn_decode(
        page_table, h_main, h_main, abt, cw, gvecs, gdn_norm_w, x_meta[CHUNK - 8:CHUNK], s_meta[0],
        lams, q_s, k_s, v_s, heads(hs[:, O_ZA:O_QKVB]), subln_w, cache_k, cache_v)
    ya = _attention(lams, h_main, k_meta, v_meta, subln_w)
    y_prompt = _merge(x_prompt.reshape(BATCH * SEQ, D_MODEL), ya, yb, h_main, h_tail,
                      wpa, wpb, wo, fw, 256)
    conv_p = h_main.reshape(BATCH, SEQ, N_MAIN)[:, SEQ - (CONV_W - 1):, O_QKVB:O_ZB][None]

    rows = B_QKV // LANES
    ab_s = jnp.pad(hs[:, O_A:O_GA], ((0, 0), (0, LANES - 2 * B_HEADS))).reshape(DEC_BATCH, 1, LANES)
    yb_s, ssm_s, conv_s = _gdn_step(
        state_conv[0].reshape(DEC_BATCH, CONV_W - 1, rows, LANES),
        hs[:, O_QKVB:O_ZB].reshape(DEC_BATCH, rows, LANES), cw.reshape(CONV_W, rows, LANES),
        ab_s, alog_l, dtb_l, heads(hs[:, O_ZB:O_A]), gdn_norm_w, state_ssm[0])
    hs_tail = jnp.pad(hs[:, N_MAIN:], ((0, 0), (0, LANES - (D_IN - N_MAIN))))
    y_sample = _merge(x_sample[:, 0, :], ya_s.reshape(DEC_BATCH, -1).astype(BF16),
                      yb_s.reshape(DEC_BATCH, -1).astype(BF16), hs, hs_tail,
                      wpa, wpb, wo, fw, DEC_BATCH)

    return (y_prompt.reshape(BATCH, SEQ, D_MODEL), y_sample.reshape(DEC_BATCH, 1, D_MODEL),
            k_rows_p[None], v_rows_p[None], conv_p, ssm_p[None],
            k_s.reshape(1, DEC_BATCH, 1, A_HEADS, LANES), v_s.reshape(1, DEC_BATCH, 1, A_HEADS, LANES),
            conv_s.reshape(1, DEC_BATCH, CONV_W - 1, B_QKV), ssm_s[None])
```

```python
import functools
import math

import jax
import jax.numpy as jnp
import numpy as np
from jax import lax
from jax.experimental import pallas as pl
from jax.experimental.pallas import tpu as pltpu

F32 = jnp.float32
BF16 = jnp.bfloat16

D_MODEL = 2048
BATCH = 4
SEQ = 2048
DEC_BATCH = 32
PAST_LEN = 8192
PAGE_SIZE = 128
N_PAGES = PAST_LEN // PAGE_SIZE
N_META = 16
A_HEADS = 8
A_DH = 64
A_DV = 128
ROPE_DIM = 16
ROPE_THETA = 500000.0
B_HEADS = 8
B_DK = 128
B_DV = 128
B_QKV = 3072
CONV_W = 4
CHUNK = 64
EPS = 1e-6
NEG = -1e30
LAMBDA_INIT = 0.8 - 0.6 * math.exp(-0.3 * 0)

O_QA, O_KA, O_VA, O_ZA, O_QKVB, O_ZB, O_A, O_GA, O_GB, D_IN = (
    0, 1024, 2048, 3072, 4096, 7168, 8192, 8208, 10256, 12304)
LANES = 128
N_MAIN = (D_IN // LANES) * LANES
GATE_SHIFT = O_GA % LANES
VMEM_LIMIT = 56 * 1024 * 1024


def _dot(a, b):
    return jnp.dot(a, b, preferred_element_type=F32)


def _dot_nt(a, b):
    return lax.dot_general(a, b, (((1,), (1,)), ((), ())), preferred_element_type=F32)


def _dot_f32(a, b):
    return jnp.dot(a, b, preferred_element_type=F32, precision=lax.Precision.HIGHEST)


def _sigmoid(x):
    return 1.0 / (1.0 + jnp.exp(-x))


def _silu(x):
    return x * _sigmoid(x)


def _softplus(x):
    return jnp.maximum(x, 0.0) + jnp.log1p(jnp.exp(-jnp.abs(x)))


def _rmsnorm(x, w):
    return x * lax.rsqrt(jnp.mean(x * x, axis=-1, keepdims=True) + EPS) * w


def _rope_tables(pos):
    pos = np.asarray(pos, np.float32)
    r = pos.shape[0]
    inv_freq = np.float32(ROPE_THETA) ** (-np.arange(0, ROPE_DIM, 2, dtype=np.float32) / ROPE_DIM)
    ang = pos[:, None] * inv_freq[None, :]
    cos, sin = np.cos(ang), np.sin(ang)
    half = ROPE_DIM // 2
    rest = A_DH - ROPE_DIM
    c = np.concatenate([cos, cos, np.ones((r, rest), np.float32)], axis=1)
    sa = np.concatenate([np.zeros((r, half), np.float32), sin, np.zeros((r, rest), np.float32)], axis=1)
    sb = np.concatenate([-sin, np.zeros((r, half + rest), np.float32)], axis=1)
    return tuple(jnp.asarray(np.tile(t, (1, LANES // A_DH)).astype(np.float32)) for t in (c, sa, sb))


def _rope_tile(t, c, sa, sb):
    out = []
    for i in range(t.shape[1] // LANES):
        x = t[:, i * LANES:(i + 1) * LANES]
        out.append(x * c + pltpu.roll(x, ROPE_DIM // 2, 1) * sa
                   + pltpu.roll(x, LANES - ROPE_DIM // 2, 1) * sb)
    return jnp.concatenate(out, axis=1) if len(out) > 1 else out[0]


IP_TM = 1024
IP_TN = 1024
IP_ROWS = 256
K_TILE, V_TILE = O_KA // IP_TN, O_VA // IP_TN
assert O_VA - O_KA == IP_TN and O_ZA - O_VA == IP_TN


def _inproj_kernel(x_ref, nw_ref, w_ref, wt_ref, c_ref, sa_ref, sb_ref, km_ref, vm_ref,
                   h_ref, ht_ref, kr_ref, vr_ref, xn_ref, rows_ref, sem_ref):
    i = pl.program_id(0)
    j = pl.program_id(1)
    per_seq = SEQ // IP_TM
    seq = i // per_seq
    row0 = (i % per_seq) * IP_TM
    first_tile = (i % per_seq) == 0

    @pl.when(j == 0)
    def _():
        def body(r, carry):
            rows = pl.ds(pl.multiple_of(r * IP_ROWS, IP_ROWS), IP_ROWS)
            xn_ref[rows, :] = _rmsnorm(x_ref[rows, :], nw_ref[...]).astype(BF16)
            return carry
        lax.fori_loop(0, IP_TM // IP_ROWS, body, 0)
        ht_ref[...] = _dot_nt(xn_ref[...], wt_ref[...])

    h_ref[...] = _dot_nt(xn_ref[...], w_ref[...])

    @pl.when(j < O_VA // IP_TN)
    def _():
        def body(r, carry):
            rows = pl.ds(pl.multiple_of(r * IP_ROWS, IP_ROWS), IP_ROWS)
            h_ref[rows, :] = _rope_tile(h_ref[rows, :], c_ref[rows, :], sa_ref[rows, :], sb_ref[rows, :])
            return carry
        lax.fori_loop(0, IP_TM // IP_ROWS, body, 0)

    def row_copies(dst_ref, k):
        return [pltpu.make_async_copy(rows_ref.at[k, :, pl.ds(h * LANES, LANES)],
                                      dst_ref.at[seq, pl.ds(N_META + row0, IP_TM), h, :], sem_ref.at[k])
                for h in range(A_HEADS)]

    def meta_copy(src_ref, dst_ref, sem):
        return pltpu.make_async_copy(src_ref, dst_ref.at[seq, pl.ds(0, N_META)], sem)

    def start_rows(meta_ref, dst_ref, k):
        def body(r, carry):
            rows = pl.ds(pl.multiple_of(r * IP_ROWS, IP_ROWS), IP_ROWS)
            rows_ref[k, rows, :] = h_ref[rows, :]
            return carry
        lax.fori_loop(0, IP_TM // IP_ROWS, body, 0)
        for cp in row_copies(dst_ref, k):
            cp.start()

        @pl.when(first_tile)
        def _():
            meta_copy(meta_ref, dst_ref, sem_ref.at[2 + k]).start()

    def wait_rows(meta_ref, dst_ref, k):
        for cp in row_copies(dst_ref, k):
            cp.wait()

        @pl.when(first_tile)
        def _():
            meta_copy(meta_ref, dst_ref, sem_ref.at[2 + k]).wait()

    @pl.when(j == K_TILE)
    def _():
        start_rows(km_ref, kr_ref, 0)

    @pl.when(j == V_TILE)
    def _():
        start_rows(vm_ref, vr_ref, 1)
        wait_rows(km_ref, kr_ref, 0)

    @pl.when(j == V_TILE + 1)
    def _():
        wait_rows(vm_ref, vr_ref, 1)


def _inproj(x, nw, w_bf, w_tail, tabs, k_meta, v_meta):
    m = x.shape[0]
    per_seq = SEQ // IP_TM
    tab_spec = pl.BlockSpec((IP_TM, LANES), lambda i, j: (i % per_seq, 0))
    meta_spec = pl.BlockSpec((N_META, A_HEADS, LANES), lambda i, j: (0, 0, 0))
    rows_shape = jax.ShapeDtypeStruct((m // SEQ, N_META + SEQ, A_HEADS, LANES), F32)
    return pl.pallas_call(
        _inproj_kernel,
        grid=(m // IP_TM, N_MAIN // IP_TN),
        in_specs=[
            pl.BlockSpec((IP_TM, D_MODEL), lambda i, j: (i, 0)),
            pl.BlockSpec((1, D_MODEL), lambda i, j: (0, 0)),
            pl.BlockSpec((IP_TN, D_MODEL), lambda i, j: (j, 0)),
            pl.BlockSpec((LANES, D_MODEL), lambda i, j: (0, 0)),
            tab_spec, tab_spec, tab_spec, meta_spec, meta_spec,
        ],
        out_specs=[
            pl.BlockSpec((IP_TM, IP_TN), lambda i, j: (i, j)),
            pl.BlockSpec((IP_TM, LANES), lambda i, j: (i, 0)),
            pl.BlockSpec(memory_space=pl.ANY), pl.BlockSpec(memory_space=pl.ANY),
        ],
        out_shape=[jax.ShapeDtypeStruct((m, N_MAIN), F32),
                   jax.ShapeDtypeStruct((m, LANES), F32), rows_shape, rows_shape],
        scratch_shapes=[pltpu.VMEM((IP_TM, D_MODEL), BF16), pltpu.VMEM((2, IP_TM, IP_TN), F32),
                        pltpu.SemaphoreType.DMA((4,))],
        compiler_params=pltpu.CompilerParams(
            dimension_semantics=("arbitrary", "arbitrary"), vmem_limit_bytes=VMEM_LIMIT),
        name="inproj",
    )(x, nw, w_bf, w_tail, *tabs, k_meta, v_meta)


AUX_ROWS = 64
AUX_TN = 512


def _aux_inproj_kernel(x_ref, nw_ref, w_ref, c_ref, sa_ref, sb_ref, o_ref, wbf_ref, wtail_ref, xs_ref):
    j = pl.program_id(0)

    @pl.when(j == 0)
    def _():
        xn = _rmsnorm(x_ref[...], nw_ref[...])
        hi = xn.astype(BF16)
        xs_ref[0:AUX_ROWS, :] = hi
        xs_ref[AUX_ROWS:2 * AUX_ROWS, :] = (xn - hi.astype(F32)).astype(BF16)

    w = w_ref[...]
    w_hi = w.astype(BF16)
    wbf_ref[...] = w_hi

    @pl.when(j == pl.num_programs(0) - 1)
    def _():
        row = lax.broadcasted_iota(jnp.int32, (LANES, D_MODEL), 0)
        wtail_ref[...] = jnp.where(row < D_IN - N_MAIN, w_hi[0:LANES, :], jnp.zeros((), BF16))

    w_lo = (w - w_hi.astype(F32)).astype(BF16)
    r1 = _dot_nt(xs_ref[...], w_hi)
    r2 = _dot_nt(xs_ref[0:AUX_ROWS, :], w_lo)
    acc = r1[0:AUX_ROWS] + (r1[AUX_ROWS:] + r2)
    is_rope = j < O_VA // AUX_TN

    @pl.when(is_rope)
    def _():
        o_ref[...] = _rope_tile(acc, c_ref[...], sa_ref[...], sb_ref[...])

    @pl.when(jnp.logical_not(is_rope))
    def _():
        o_ref[...] = acc


def _aux_inproj(xa, nw, w, tabs):
    tab_spec = pl.BlockSpec((AUX_ROWS, LANES), lambda j: (0, 0))
    return pl.pallas_call(
        _aux_inproj_kernel,
        grid=(pl.cdiv(D_IN, AUX_TN),),
        in_specs=[
            pl.BlockSpec((AUX_ROWS, D_MODEL), lambda j: (0, 0)),
            pl.BlockSpec((1, D_MODEL), lambda j: (0, 0)),
            pl.BlockSpec((AUX_TN, D_MODEL), lambda j: (j, 0)),
            tab_spec, tab_spec, tab_spec,
        ],
        out_specs=[pl.BlockSpec((AUX_ROWS, AUX_TN), lambda j: (0, j)),
                   pl.BlockSpec((AUX_TN, D_MODEL), lambda j: (j, 0)),
                   pl.BlockSpec((LANES, D_MODEL), lambda j: (0, 0))],
        out_shape=[jax.ShapeDtypeStruct((AUX_ROWS, D_IN), F32),
                   jax.ShapeDtypeStruct((D_IN, D_MODEL), BF16),
                   jax.ShapeDtypeStruct((LANES, D_MODEL), BF16)],
        scratch_shapes=[pltpu.VMEM((2 * AUX_ROWS, D_MODEL), BF16)],
        compiler_params=pltpu.CompilerParams(
            dimension_semantics=("arbitrary",), vmem_limit_bytes=VMEM_LIMIT),
        name="aux_inproj",
    )(xa, nw, w, *tabs)


AT_TQ = 512
AT_TK = 512
AT_HP = 2
LOG2E = math.log2(math.e)


def _diff_lambda(lq1, lk1, lq2, lk2):
    a = jnp.exp(jnp.sum(lq1 * lk1, axis=-1, keepdims=True))
    b = jnp.exp(jnp.sum(lq2 * lk2, axis=-1, keepdims=True))
    return a - b + LAMBDA_INIT


def _attn_kernel(lq1_ref, lk1_ref, lq2_ref, lk2_ref, q_ref, k_ref, v_ref, km_ref, vm_ref,
                 za_ref, sw_ref, o_ref, kb_ref, vb_ref, kmb_ref, vmb_ref, q2_ref, s_ref, sm_ref,
                 m_ref, l_ref, acc_ref):
    qi = pl.program_id(2)
    tq, tk = AT_TQ, AT_TK
    n_tiles = tk // LANES
    heads = range(AT_HP)
    hl = lambda h: slice(h * LANES, (h + 1) * LANES)

    @pl.when(qi == 0)
    def _():
        lane_m = lax.broadcasted_iota(jnp.int32, (N_META, LANES), 1)
        zeros_kv = jnp.zeros((N_META, LANES), BF16)
        for h in heads:
            kb_ref[h] = k_ref[:, hl(h)].astype(BF16)
            vb_ref[h] = v_ref[:, hl(h)].astype(BF16)
            km = km_ref[:, hl(h)]
            vm = vm_ref[:, hl(h)].astype(BF16)
            kmb_ref[h] = jnp.concatenate(
                [jnp.where(lane_m < A_DH, km, 0.0).astype(BF16),
                 jnp.where(lane_m >= A_DH, km, 0.0).astype(BF16),
                 jnp.zeros((LANES - 2 * N_META, LANES), BF16)], axis=0)
            vmb_ref[h] = jnp.concatenate(
                [jnp.concatenate([vm, zeros_kv], axis=1), jnp.concatenate([zeros_kv, vm], axis=1),
                 jnp.zeros((LANES - 2 * N_META, 2 * LANES), BF16)], axis=0)

    lane = lax.broadcasted_iota(jnp.int32, (tq, LANES), 1)
    qb = []
    for h in heads:
        q = q_ref[:, hl(h)] * (A_DH ** -0.5 * LOG2E)
        qb.append(q.astype(BF16))
        q2_ref[h, 0:tq, :] = jnp.where(lane < A_DH, q, 0.0).astype(BF16)
        q2_ref[h, tq:2 * tq, :] = jnp.where(lane >= A_DH, q, 0.0).astype(BF16)

    def tile_max(m, s):
        for c in range(s.shape[1] // LANES):
            m = jnp.maximum(m, s[:, c * LANES:(c + 1) * LANES])
        return m

    rm = [_dot_nt(qb[h], kmb_ref[h]) for h in heads]
    for h in heads:
        top = jnp.where(lane < N_META, rm[h], NEG)
        bot = jnp.where((lane >= N_META) & (lane < 2 * N_META), rm[h], NEG)
        sm_ref[h, 0:tq, :] = top
        sm_ref[h, tq:2 * tq, :] = bot
        m_ref[h, 0:tq, :] = top
        m_ref[h, tq:2 * tq, :] = bot

    def scores(j):
        rows = pl.ds(pl.multiple_of(j * tk, tk), tk)
        return [_dot_nt(q2_ref[h], kb_ref[h, rows, :]) for h in heads]

    def pass1(j, carry):
        s = scores(j)
        for h in heads:
            s_ref[h, j] = s[h]
            m_ref[h] = tile_max(m_ref[h], s[h])
        return carry
    lax.fori_loop(0, qi, pass1, 0)

    r = lax.broadcasted_iota(jnp.int32, (2 * tq, tk), 0)
    r = jnp.where(r >= tq, r - tq, r)
    c = lax.broadcasted_iota(jnp.int32, (2 * tq, tk), 1)
    causal = c <= r
    s = [jnp.where(causal, t, NEG) for t in scores(qi)]
    for h in heads:
        s_ref[h, qi] = s[h]
    m = [jnp.max(tile_max(m_ref[h], s[h]), axis=-1, keepdims=True) for h in heads]
    for h in heads:
        m_ref[h] = jnp.broadcast_to(m[h], (2 * tq, LANES))

    p = [jnp.exp2(sm_ref[h] - m_ref[h]) for h in heads]
    om = [_dot((p[h][0:tq] + p[h][tq:2 * tq]).astype(BF16), vmb_ref[h]) for h in heads]
    for h in heads:
        l_ref[h] = p[h]
        acc_ref[h, 0:tq, :] = om[h][:, 0:LANES]
        acc_ref[h, tq:2 * tq, :] = om[h][:, LANES:]

    def pass2(j, carry):
        rows = pl.ds(pl.multiple_of(j * tk, tk), tk)
        pb = []
        for h in heads:
            mb = m_ref[h]
            lsum = l_ref[h]
            ps = []
            for t in range(n_tiles):
                p = jnp.exp2(s_ref[h, j, :, t * LANES:(t + 1) * LANES] - mb)
                lsum = lsum + p
                ps.append(p.astype(BF16))
            l_ref[h] = lsum
            pb.append(jnp.concatenate(ps, axis=1))
        pv = [_dot(pb[h], vb_ref[h, rows, :]) for h in heads]
        for h in heads:
            acc_ref[h] += pv[h]
        return carry
    lax.fori_loop(0, qi + 1, pass2, 0)

    lam = _diff_lambda(lq1_ref[...], lk1_ref[...], lq2_ref[...], lk2_ref[...])
    for h in heads:
        l = jnp.sum(l_ref[h], axis=-1, keepdims=True)
        o = acc_ref[h, 0:tq, :] / l[0:tq] - lam * (acc_ref[h, tq:2 * tq, :] / l[tq:2 * tq])
        y = _rmsnorm(o, sw_ref[...]) * (1.0 - LAMBDA_INIT)
        o_ref[:, hl(h)] = (y * _silu(za_ref[:, hl(h)])).astype(BF16)


def _attention(lams, h_main, k_meta, v_meta, subln_w):
    nq = SEQ // AT_TQ
    w = AT_HP * LANES
    lam_spec = pl.BlockSpec((1, A_DH), lambda b, g, i: (0, 0))
    return pl.pallas_call(
        _attn_kernel,
        grid=(BATCH, A_HEADS // AT_HP, nq),
        in_specs=[
            lam_spec, lam_spec, lam_spec, lam_spec,
            pl.BlockSpec((AT_TQ, w), lambda b, g, i: (b * nq + i, O_QA // w + g)),
            pl.BlockSpec((SEQ, w), lambda b, g, i: (b, O_KA // w + g)),
            pl.BlockSpec((SEQ, w), lambda b, g, i: (b, O_VA // w + g)),
            pl.BlockSpec((N_META, w), lambda b, g, i: (0, g)),
            pl.BlockSpec((N_META, w), lambda b, g, i: (0, g)),
            pl.BlockSpec((AT_TQ, w), lambda b, g, i: (b * nq + i, O_ZA // w + g)),
            pl.BlockSpec((1, A_DV), lambda b, g, i: (0, 0)),
        ],
        out_specs=pl.BlockSpec((AT_TQ, w), lambda b, g, i: (b * nq + i, g)),
        out_shape=jax.ShapeDtypeStruct((BATCH * SEQ, A_HEADS * A_DV), BF16),
        scratch_shapes=[
            pltpu.VMEM((AT_HP, SEQ, LANES), BF16), pltpu.VMEM((AT_HP, SEQ, LANES), BF16),
            pltpu.VMEM((AT_HP, LANES, LANES), BF16), pltpu.VMEM((AT_HP, LANES, 2 * LANES), BF16),
            pltpu.VMEM((AT_HP, 2 * AT_TQ, LANES), BF16),
            pltpu.VMEM((AT_HP, SEQ // AT_TK, 2 * AT_TQ, AT_TK), F32),
            pltpu.VMEM((AT_HP, 2 * AT_TQ, LANES), F32),
            pltpu.VMEM((AT_HP, 2 * AT_TQ, LANES), F32), pltpu.VMEM((AT_HP, 2 * AT_TQ, LANES), F32),
            pltpu.VMEM((AT_HP, 2 * AT_TQ, A_DV), F32),
        ],
        compiler_params=pltpu.CompilerParams(
            dimension_semantics=("arbitrary", "arbitrary", "arbitrary"),
            vmem_limit_bytes=VMEM_LIMIT),
        name="diff_attn",
    )(*lams, h_main, h_main, h_main, k_meta, v_meta, h_main, subln_w)


def _unit_lower_inverse_minus_eye(a_list, i_idx, j_idx):
    base = 8
    diag = (i_idx // base) == (j_idx // base)
    b = [jnp.where(diag, a, 0.0) for a in a_list]
    n = [-x for x in b]
    for _ in range(2):
        bb = [x.astype(BF16) for x in b]
        b = [_dot(x, x) for x in bb]
        nb = [_dot(x.astype(BF16), y.astype(BF16)) for x, y in zip(n, b)]
        n = [x + y + z for x, y, z in zip(n, b, nb)]
    s = base
    while s < CHUNK:
        join = ((i_idx // (2 * s)) == (j_idx // (2 * s))) & ((i_idx // s) % 2 == 1) & ((j_idx // s) % 2 == 0)
        a_s = [jnp.where(join, a, 0.0) for a in a_list]
        x = [p + _dot(q.astype(BF16), p.astype(BF16)) for p, q in zip(a_s, n)]
        xn = [_dot(p.astype(BF16), q.astype(BF16)) for p, q in zip(x, n)]
        n = [q - (p + r) for q, p, r in zip(n, x, xn)]
        s *= 2
    return n


def _gdn_kernel(xq_ref, xk_ref, xv_ref, ab_ref, abt_ref, zb_ref, cw_ref, alog_ref, dtb_ref,
                alogc_ref, dtbc_ref, gw_ref, halo_ref, s0_ref, yb_ref, sfin_ref, ext_ref, s_ref,
                *, masked_rows):
    c = pl.program_id(1)
    hist = 8

    @pl.when(c == 0)
    def _():
        ext_ref[0:hist, :] = halo_ref[...]
        s_ref[...] = s0_ref[...]

    x = jnp.concatenate([xq_ref[...], xk_ref[...], xv_ref[...]], axis=1)
    ext_ref[hist:hist + CHUNK, :] = x
    y = cw_ref[CONV_W - 1:CONV_W, :] * x
    for t in range(CONV_W - 1):
        lo = hist - (CONV_W - 1) + t
        y = y + cw_ref[t:t + 1, :] * ext_ref[lo:lo + CHUNK, :]
    y = _silu(y)
    ext_ref[0:hist, :] = x[CHUNK - hist:CHUNK, :]

    i_idx = lax.broadcasted_iota(jnp.int32, (CHUNK, CHUNK), 0)
    j_idx = lax.broadcasted_iota(jnp.int32, (CHUNK, CHUNK), 1)
    tril = i_idx >= j_idx
    strict = i_idx > j_idx
    tril_f = tril.astype(F32)
    triu_f = (i_idx <= j_idx).astype(F32)

    ab = ab_ref[...]
    g_c = -jnp.exp(alog_ref[...]) * _softplus(ab + dtb_ref[...])
    beta_c = _sigmoid(ab)
    abt = abt_ref[...]
    g_r = -jnp.exp(alogc_ref[...]) * _softplus(abt + dtbc_ref[...])
    if masked_rows:
        row_ok = lax.broadcasted_iota(jnp.int32, (CHUNK, LANES), 0) >= masked_rows
        col_ok = lax.broadcasted_iota(jnp.int32, (2 * B_HEADS, CHUNK), 1) >= masked_rows
        g_c = jnp.where(row_ok, g_c, 0.0)
        beta_c = jnp.where(row_ok, beta_c, 0.0)
        g_r = jnp.where(col_ok, g_r, 0.0)
    gc_c = _dot_f32(tril_f, g_c)
    gc_r = _dot_f32(g_r, triu_f)

    nk = B_HEADS * B_DK
    heads = range(B_HEADS)

    def l2n(t):
        return t * lax.rsqrt(jnp.sum(t * t, axis=-1, keepdims=True) + EPS)

    qn = [l2n(y[:, h * B_DK:(h + 1) * B_DK]) * (B_DK ** -0.5) for h in heads]
    kn = [l2n(y[:, nk + h * B_DK:nk + (h + 1) * B_DK]) for h in heads]
    vh = [y[:, 2 * nk + h * B_DV:2 * nk + (h + 1) * B_DV] for h in heads]
    bcol = [beta_c[:, B_HEADS + h:B_HEADS + h + 1] for h in heads]
    gcc = [gc_c[:, h:h + 1] for h in heads]
    decay = [jnp.where(tril, jnp.exp(jnp.where(tril, gcc[h] - gc_r[h:h + 1, :], 0.0)), 0.0)
             for h in heads]
    kbeta = [kn[h] * bcol[h] for h in heads]
    kn_b = [t.astype(BF16) for t in kn]
    kk = [_dot_nt(kbeta[h].astype(BF16), kn_b[h]) for h in heads]
    qk = [_dot_nt(qn[h].astype(BF16), kn_b[h]) for h in heads]
    a = [jnp.where(strict, kk[h] * decay[h], 0.0) for h in heads]
    n = _unit_lower_inverse_minus_eye(a, i_idx, j_idx)
    egc = [jnp.exp(t) for t in gcc]
    rhs = [jnp.concatenate([vh[h] * bcol[h], kbeta[h] * egc[h]], axis=1) for h in heads]
    nr = [_dot(n[h].astype(BF16), rhs[h].astype(BF16)) for h in heads]
    sol = [rhs[h] + nr[h] for h in heads]
    st = [s_ref[h] for h in heads]
    st_b = [t.astype(BF16) for t in st]
    ws = [_dot(sol[h][:, B_DV:B_DV + B_DK].astype(BF16), st_b[h]) for h in heads]
    qs = [_dot((qn[h] * egc[h]).astype(BF16), st_b[h]) for h in heads]
    v_new_b = [(sol[h][:, 0:B_DV] - ws[h]).astype(BF16) for h in heads]
    av = [_dot((qk[h] * decay[h]).astype(BF16), v_new_b[h]) for h in heads]
    g_last = [t[CHUNK - 1:CHUNK, :] for t in gcc]
    ke_t = [(kn[h] * jnp.exp(g_last[h] - gcc[h])).T.astype(BF16) for h in heads]
    kv = [_dot(ke_t[h], v_new_b[h]) for h in heads]
    for h in heads:
        s_ref[h] = st[h] * jnp.exp(g_last[h]) + kv[h]
    for h in heads:
        zb = zb_ref[:, h * B_DV:(h + 1) * B_DV]
        yb_ref[:, h * B_DV:(h + 1) * B_DV] = (_rmsnorm(qs[h] + av[h], gw_ref[...]) * _silu(zb)).astype(BF16)

    @pl.when(c == pl.num_programs(1) - 1)
    def _():
        sfin_ref[...] = s_ref[...]


def _gdn_chunks(x, x_col, ab, ab_col, abt, zsrc, zb_col, cw, gvecs, gw, halo, s0, n_seq, n_chunk,
                masked_rows):
    alog_l, dtb_l, alog_c, dtb_c = gvecs
    const2 = lambda b, c: (0, 0)
    kern = functools.partial(_gdn_kernel, masked_rows=masked_rows)
    part = B_QKV // 3

    def x_spec(k):
        return pl.BlockSpec((CHUNK, part), lambda b, c: (b * n_chunk + c, x_col + k))

    return pl.pallas_call(
        kern,
        grid=(n_seq, n_chunk),
        in_specs=[
            x_spec(0), x_spec(1), x_spec(2),
            pl.BlockSpec((CHUNK, LANES), lambda b, c: (b * n_chunk + c, ab_col)),
            pl.BlockSpec((None, None, 2 * B_HEADS, CHUNK), lambda b, c: (b, c, 0, 0)),
            pl.BlockSpec((CHUNK, B_HEADS * B_DV), lambda b, c: (b * n_chunk + c, zb_col)),
            pl.BlockSpec((CONV_W, B_QKV), const2),
            pl.BlockSpec((1, LANES), const2), pl.BlockSpec((1, LANES), const2),
            pl.BlockSpec((2 * B_HEADS, 1), const2), pl.BlockSpec((2 * B_HEADS, 1), const2),
            pl.BlockSpec((1, B_DV), const2),
            pl.BlockSpec((8, B_QKV), const2),
            pl.BlockSpec((B_HEADS, B_DK, B_DV), lambda b, c: (0, 0, 0)),
        ],
        out_specs=[
            pl.BlockSpec((CHUNK, B_HEADS * B_DV), lambda b, c: (b * n_chunk + c, 0)),
            pl.BlockSpec((None, B_HEADS, B_DK, B_DV), lambda b, c: (b, 0, 0, 0)),
        ],
        out_shape=[jax.ShapeDtypeStruct((n_seq * n_chunk * CHUNK, B_HEADS * B_DV), BF16),
                   jax.ShapeDtypeStruct((n_seq, B_HEADS, B_DK, B_DV), F32)],
        scratch_shapes=[pltpu.VMEM((8 + CHUNK, B_QKV), F32),
                        pltpu.VMEM((B_HEADS, B_DK, B_DV), F32)],
        compiler_params=pltpu.CompilerParams(
            dimension_semantics=("arbitrary", "arbitrary"), vmem_limit_bytes=VMEM_LIMIT),
        name="gdn_chunks",
    )(x, x, x, ab, abt, zsrc, cw, alog_l, dtb_l, alog_c, dtb_c, gw, halo, s0)


def _shift_lanes(x, k):
    nblk = x.shape[1] // LANES
    r = [pltpu.roll(x[:, c * LANES:(c + 1) * LANES], LANES - k, 1) for c in range(nblk)]
    lane = lax.broadcasted_iota(jnp.int32, (x.shape[0], LANES), 1)
    return jnp.concatenate([jnp.where(lane < LANES - k, r[c], r[c + 1]) for c in range(nblk - 1)],
                           axis=1)


def _merge_kernel(x_ref, ya_ref, yb_ref, g4_ref, g5_ref, gt_ref, wpa_ref, wpb_ref, wo_ref, fw_ref,
                  y_ref):
    g5 = g5_ref[...]
    ga = _shift_lanes(jnp.concatenate([g4_ref[...], g5[:, 0:LANES]], axis=1), GATE_SHIFT)
    gb = _shift_lanes(jnp.concatenate([g5, gt_ref[...]], axis=1), GATE_SHIFT)
    pa = _dot(ya_ref[...], wpa_ref[...])
    pb = _dot(yb_ref[...], wpb_ref[...])
    mixed = _sigmoid(ga) * pa + _sigmoid(gb) * pb
    hp = x_ref[...] + _dot(mixed.astype(BF16), wo_ref[...])
    y_ref[...] = _rmsnorm(hp, fw_ref[...])


def _merge(x, ya, yb, hsrc, htail, wpa, wpb, wo, fw, tm):
    m = x.shape[0]
    once = pl.Buffered(1)
    return pl.pallas_call(
        _merge_kernel,
        grid=(m // tm,),
        in_specs=[
            pl.BlockSpec((tm, D_MODEL), lambda i: (i, 0)),
            pl.BlockSpec((tm, A_HEADS * A_DV), lambda i: (i, 0)),
            pl.BlockSpec((tm, B_HEADS * B_DV), lambda i: (i, 0)),
            pl.BlockSpec((tm, D_MODEL), lambda i: (i, O_A // D_MODEL)),
            pl.BlockSpec((tm, D_MODEL), lambda i: (i, O_A // D_MODEL + 1)),
            pl.BlockSpec((tm, LANES), lambda i: (i, 0)),
            pl.BlockSpec((A_HEADS * A_DV, D_MODEL), lambda i: (0, 0), pipeline_mode=once),
            pl.BlockSpec((B_HEADS * B_DV, D_MODEL), lambda i: (0, 0), pipeline_mode=once),
            pl.BlockSpec((D_MODEL, D_MODEL), lambda i: (0, 0), pipeline_mode=once),
            pl.BlockSpec((1, D_MODEL), lambda i: (0, 0)),
        ],
        out_specs=pl.BlockSpec((tm, D_MODEL), lambda i: (i, 0)),
        out_shape=jax.ShapeDtypeStruct((m, D_MODEL), F32),
        compiler_params=pltpu.CompilerParams(
            dimension_semantics=("arbitrary",), vmem_limit_bytes=VMEM_LIMIT),
        name="merge",
    )(x, ya, yb, hsrc, hsrc, htail, wpa, wpb, wo, fw)


DA_GRP = 8
HALF_TOK = PAGE_SIZE // 2
HALF_ROWS = HALF_TOK * A_HEADS
TOK_TILES = HALF_ROWS // LANES


def _lane_group_reduce(x, op):
    s = A_HEADS
    while s < LANES:
        x = op(x, pltpu.roll(x, s, 1))
        s *= 2
    return x


FD_PAGES = N_PAGES // 2
FD_STEPS = 4
FD_NBUF = 2 * FD_PAGES


def _gdn_decode_kernel(pt_ref, xq_ref, xk_ref, xv_ref, ab_ref, abt_ref, zb_ref, cw_ref, alog_ref,
                       dtb_ref, alogc_ref, dtbc_ref, gw_ref, halo_ref, s0_ref,
                       lq1_ref, lk1_ref, lq2_ref, lk2_ref, q_ref, kn_ref, vn_ref, za_ref, sw_ref,
                       ck_ref, cv_ref, yb_ref, sfin_ref, o_ref,
                       ext_ref, s_ref, buf_ref, sem_ref, sc_ref, acc_ref, wn_ref):
    n_chunk = pl.num_programs(1)
    t = pl.program_id(0) * n_chunk + pl.program_id(1)
    n_steps = pl.num_programs(0) * n_chunk
    role = t % FD_STEPS

    def start_step(step):
        seq = step // FD_STEPS
        r = step % FD_STEPS
        half = (step % 2) * FD_PAGES
        first = (r % 2) * FD_PAGES

        def copies(src_ref):
            for s in range(FD_PAGES):
                pltpu.make_async_copy(src_ref.at[0, pt_ref[seq, first + s]], buf_ref.at[half + s],
                                      sem_ref.at[half + s]).start(priority=s % 2)

        @pl.when(r < 2)
        def _():
            copies(ck_ref)

        @pl.when(r >= 2)
        def _():
            copies(cv_ref)

    def wait_slot(slot):
        pltpu.make_async_copy(ck_ref.at[0, 0], buf_ref.at[slot], sem_ref.at[slot]).wait()

    @pl.when(t == 0)
    def _():
        start_step(0)

    @pl.when(t + 1 < n_steps)
    def _():
        start_step(t + 1)

    _gdn_kernel(xq_ref, xk_ref, xv_ref, ab_ref, abt_ref, zb_ref, cw_ref, alog_ref, dtb_ref,
                alogc_ref, dtbc_ref, gw_ref, halo_ref, s0_ref, yb_ref, sfin_ref, ext_ref, s_ref,
                masked_rows=0)

    half = (t % 2) * FD_PAGES
    sub = lax.broadcasted_iota(jnp.int32, (A_HEADS, LANES), 0)
    lane = lax.broadcasted_iota(jnp.int32, (A_HEADS, LANES), 1)
    hmask = (lane % A_HEADS) == sub
    first_half = sub < 2
    second_half = (sub >= 2) & (sub < 4)
    map0 = (sub % 2) == 0
    sub2 = lax.broadcasted_iota(jnp.int32, (A_HEADS, 2 * LANES), 0)
    lane2 = lax.broadcasted_iota(jnp.int32, (A_HEADS, 2 * LANES), 1)
    half_sel = ((sub2 < 4) & (lane2 // A_DH == sub2)).astype(BF16)
    n_grp = FD_PAGES // DA_GRP

    def halves_on_lanes(page):
        return jnp.concatenate([page[0:HALF_TOK].reshape(HALF_ROWS, LANES),
                                page[HALF_TOK:].reshape(HALF_ROWS, LANES)], axis=1)

    def pair_halves(x, op):
        up = pltpu.roll(x, 2, 0)
        dn = pltpu.roll(x, A_HEADS - 2, 0)
        return jnp.where(first_half, op(x, dn), jnp.where(second_half, op(x, up), 0.0))

    @pl.when(role < 2)
    def _():
        q = q_ref[...]

        def k_group(g, carry):
            slots = [half + g * DA_GRP + i for i in range(DA_GRP)]
            for s in slots:
                wait_slot(s)
            prods = [halves_on_lanes(buf_ref[s] * q[None]).astype(BF16) for s in slots]
            scs = [_dot_nt(half_sel, p) for p in prods]
            for i, sc in enumerate(scs):
                base = (role * FD_PAGES + g * DA_GRP + i) * TOK_TILES
                for c in range(TOK_TILES):
                    sc_ref[base + c] = sc[:, c * LANES:(c + 1) * LANES]
            return carry
        lax.fori_loop(0, n_grp, k_group, 0)

    @pl.when(role == 1)
    def _():
        lam = _diff_lambda(lq1_ref[...], lk1_ref[...], lq2_ref[...], lk2_ref[...])
        prod = q_ref[...] * kn_ref[...]
        hs1 = jnp.sum(jnp.where(lane < A_DH, prod, 0.0), axis=1, keepdims=True)
        hs2 = jnp.sum(jnp.where(lane >= A_DH, prod, 0.0), axis=1, keepdims=True)
        row1 = jnp.sum(jnp.where(hmask, hs1, 0.0), axis=0, keepdims=True)
        row2 = jnp.sum(jnp.where(hmask, hs2, 0.0), axis=0, keepdims=True)
        s_new = jnp.where(sub < 4, jnp.where(map0, row1, row2), 0.0)
        sc = sc_ref[...]
        mx = pair_halves(_lane_group_reduce(jnp.max(sc, axis=0), jnp.maximum), jnp.maximum)
        mx = jnp.maximum(mx, s_new)
        p = jnp.exp(sc - mx[None])
        p_new = jnp.exp(s_new - mx)
        den = pair_halves(_lane_group_reduce(jnp.sum(p, axis=0), jnp.add), jnp.add) + p_new
        coef = jnp.where(sub < 4, jnp.where(map0, 1.0 / den, -lam / den), 0.0)
        sc_ref[...] = p * coef[None]
        wn_ref[...] = jnp.sum(jnp.where(first_half, p_new * coef, 0.0), axis=0, keepdims=True)
        acc_ref[...] = jnp.zeros_like(acc_ref)

    @pl.when(role >= 2)
    def _():
        def page_weights(page):
            wa, wb = [], []
            for c in range(TOK_TILES):
                tile = sc_ref[page * TOK_TILES + c]
                ra = jnp.sum(jnp.where(first_half, tile, 0.0), axis=0, keepdims=True)
                rb = jnp.sum(jnp.where(second_half, tile, 0.0), axis=0, keepdims=True)
                wa.append(jnp.where(hmask, ra, 0.0))
                wb.append(jnp.where(hmask, rb, 0.0))
            w = jnp.concatenate([jnp.concatenate(wa, axis=1), jnp.concatenate(wb, axis=1)], axis=0)
            w_hi = w.astype(BF16)
            w_lo = (w - w_hi.astype(F32)).astype(BF16)
            return jnp.concatenate([w_hi, w_lo], axis=0)

        def v_group(g, acc):
            first = (role - 2) * FD_PAGES + g * DA_GRP
            ws = [page_weights(first + i) for i in range(DA_GRP)]
            slots = [half + g * DA_GRP + i for i in range(DA_GRP)]
            for s in slots:
                wait_slot(s)
            vs = [halves_on_lanes(buf_ref[s]).astype(BF16) for s in slots]
            rs = [_dot(w, v) for w, v in zip(ws, vs)]
            for r in rs:
                acc = acc + ((r[0:8, 0:LANES] + r[8:16, LANES:]) + (r[16:24, 0:LANES] + r[24:32, LANES:]))
            return acc
        acc_ref[...] = lax.fori_loop(0, n_grp, v_group, acc_ref[...])

    @pl.when(role == FD_STEPS - 1)
    def _():
        w_new = jnp.sum(jnp.where(hmask, wn_ref[...], 0.0), axis=1, keepdims=True) * (A_HEADS / LANES)
        o = acc_ref[...] + w_new * vn_ref[...]
        y = _rmsnorm(o, sw_ref[...]) * (1.0 - LAMBDA_INIT)
        o_ref[...] = y * _silu(za_ref[...])


def _gdn_decode(page_table, x, ab_src, abt, cw, gvecs, gw, halo, s0,
                lams, q_s, k_new, v_new, za_s, subln_w, cache_k, cache_v):
    alog_l, dtb_l, alog_c, dtb_c = gvecs
    n_chunk = SEQ // CHUNK
    assert BATCH * n_chunk == DEC_BATCH * FD_STEPS
    part = B_QKV // 3
    row = lambda b, c, pt: (b * n_chunk + c, 0)
    const2 = lambda b, c, pt: (0, 0)
    seq3 = lambda b, c, pt: ((b * n_chunk + c) // FD_STEPS, 0, 0)

    def col(width, k):
        return pl.BlockSpec((CHUNK, width), lambda b, c, pt: (b * n_chunk + c, k))

    lam_spec = pl.BlockSpec((1, A_DH), const2)
    seq_spec = pl.BlockSpec((None, A_HEADS, LANES), seq3)
    hbm_spec = pl.BlockSpec(memory_space=pl.ANY)
    grid_spec = pltpu.PrefetchScalarGridSpec(
        num_scalar_prefetch=1,
        grid=(BATCH, n_chunk),
        in_specs=[
            col(part, O_QKVB // part), col(part, O_QKVB // part + 1), col(part, O_QKVB // part + 2),
            col(LANES, O_A // LANES),
            pl.BlockSpec((None, None, 2 * B_HEADS, CHUNK), lambda b, c, pt: (b, c, 0, 0)),
            col(part, O_ZB // part),
            pl.BlockSpec((CONV_W, B_QKV), const2),
            pl.BlockSpec((1, LANES), const2), pl.BlockSpec((1, LANES), const2),
            pl.BlockSpec((2 * B_HEADS, 1), const2), pl.BlockSpec((2 * B_HEADS, 1), const2),
            pl.BlockSpec((1, B_DV), const2),
            pl.BlockSpec((8, B_QKV), const2),
            pl.BlockSpec((B_HEADS, B_DK, B_DV), lambda b, c, pt: (0, 0, 0)),
            lam_spec, lam_spec, lam_spec, lam_spec, seq_spec, seq_spec, seq_spec, seq_spec,
            pl.BlockSpec((1, A_DV), const2), hbm_spec, hbm_spec,
        ],
        out_specs=[
            pl.BlockSpec((CHUNK, B_HEADS * B_DV), row),
            pl.BlockSpec((None, B_HEADS, B_DK, B_DV), lambda b, c, pt: (b, 0, 0, 0)),
            seq_spec,
        ],
        scratch_shapes=[
            pltpu.VMEM((8 + CHUNK, B_QKV), F32),
            pltpu.VMEM((B_HEADS, B_DK, B_DV), F32),
            pltpu.VMEM((FD_NBUF, PAGE_SIZE, A_HEADS, LANES), F32),
            pltpu.SemaphoreType.DMA((FD_NBUF,)),
            pltpu.VMEM((N_PAGES * TOK_TILES, A_HEADS, LANES), F32),
            pltpu.VMEM((A_HEADS, A_DV), F32),
            pltpu.VMEM((1, LANES), F32),
        ],
    )
    return pl.pallas_call(
        _gdn_decode_kernel,
        grid_spec=grid_spec,
        out_shape=[jax.ShapeDtypeStruct((BATCH * SEQ, B_HEADS * B_DV), BF16),
                   jax.ShapeDtypeStruct((BATCH, B_HEADS, B_DK, B_DV), F32),
                   jax.ShapeDtypeStruct((DEC_BATCH, A_HEADS, A_DV), F32)],
        compiler_params=pltpu.CompilerParams(
            dimension_semantics=("arbitrary", "arbitrary"), vmem_limit_bytes=VMEM_LIMIT),
        name="gdn_decode",
    )(page_table, x, x, x, ab_src, abt, x, cw, alog_l, dtb_l, alog_c, dtb_c, gw, halo, s0,
      *lams, q_s, k_new, v_new, za_s, subln_w, cache_k, cache_v)


GS_SEQS = 4


def _gdn_step_kernel(sc_ref, x_ref, cw_ref, ab_ref, alog_ref, dtb_ref, zb_ref, gw_ref, s_ref,
                     yb_ref, sout_ref, cout_ref):
    for i in range(GS_SEQS):
        x = x_ref[i]
        y = cw_ref[CONV_W - 1] * x
        for t in range(CONV_W - 1):
            y = y + cw_ref[t] * sc_ref[i, t]
        y = _silu(y)
        for t in range(CONV_W - 2):
            cout_ref[i, t] = sc_ref[i, t + 1]
        cout_ref[i, CONV_W - 2] = x

        q = y[0:B_HEADS]
        k = y[B_HEADS:2 * B_HEADS]
        v = y[2 * B_HEADS:3 * B_HEADS]
        qn = q * lax.rsqrt(jnp.sum(q * q, axis=-1, keepdims=True) + EPS) * (B_DK ** -0.5)
        kn = k * lax.rsqrt(jnp.sum(k * k, axis=-1, keepdims=True) + EPS)
        qt = qn.T
        kt = kn.T
        ab = ab_ref[i]
        g = -jnp.exp(alog_ref[...]) * _softplus(ab + dtb_ref[...])
        beta = _sigmoid(ab)
        heads = range(B_HEADS)
        kcol = [kt[:, h:h + 1] for h in heads]
        st = [s_ref[i, h] * jnp.exp(g[:, h:h + 1]) for h in heads]
        kv = [jnp.sum(st[h] * kcol[h], axis=0, keepdims=True) for h in heads]
        d = [(v[h:h + 1] - kv[h]) * beta[:, B_HEADS + h:B_HEADS + h + 1] for h in heads]
        st = [st[h] + kcol[h] * d[h] for h in heads]
        for h in heads:
            sout_ref[i, h] = st[h]
        o = [jnp.sum(st[h] * qt[:, h:h + 1], axis=0, keepdims=True) for h in heads]
        for h in heads:
            yb_ref[i, h:h + 1, :] = _rmsnorm(o[h], gw_ref[...]) * _silu(zb_ref[i, h:h + 1, :])


def _gdn_step(state_conv, x, cw, ab, alog_l, dtb_l, zb, gw, state_ssm):
    rows = B_QKV // LANES
    seq3 = lambda b: (b, 0, 0)
    seq4 = lambda b: (b, 0, 0, 0)
    const2 = lambda b: (0, 0)
    return pl.pallas_call(
        _gdn_step_kernel,
        grid=(DEC_BATCH // GS_SEQS,),
        in_specs=[
            pl.BlockSpec((GS_SEQS, CONV_W - 1, rows, LANES), seq4),
            pl.BlockSpec((GS_SEQS, rows, LANES), seq3),
            pl.BlockSpec((CONV_W, rows, LANES), lambda b: (0, 0, 0)),
            pl.BlockSpec((GS_SEQS, 1, LANES), seq3),
            pl.BlockSpec((1, LANES), const2), pl.BlockSpec((1, LANES), const2),
            pl.BlockSpec((GS_SEQS, B_HEADS, B_DV), seq3),
            pl.BlockSpec((1, B_DV), const2),
            pl.BlockSpec((GS_SEQS, B_HEADS, B_DK, B_DV), seq4),
        ],
        out_specs=[
            pl.BlockSpec((GS_SEQS, B_HEADS, B_DV), seq3),
            pl.BlockSpec((GS_SEQS, B_HEADS, B_DK, B_DV), seq4),
            pl.BlockSpec((GS_SEQS, CONV_W - 1, rows, LANES), seq4),
        ],
        out_shape=[jax.ShapeDtypeStruct((DEC_BATCH, B_HEADS, B_DV), F32),
                   jax.ShapeDtypeStruct((DEC_BATCH, B_HEADS, B_DK, B_DV), F32),
                   jax.ShapeDtypeStruct((DEC_BATCH, CONV_W - 1, rows, LANES), F32)],
        compiler_params=pltpu.CompilerParams(
            dimension_semantics=("arbitrary",), vmem_limit_bytes=VMEM_LIMIT),
        name="gdn_step",
    )(state_conv, x, cw, ab, alog_l, dtb_l, zb, gw, state_ssm)


def kernel(x_prompt, x_sample, cache_k, cache_v, state_conv, state_ssm, page_table, meta_tokens,
           norm_w, w_in, lambda_q1, lambda_k1, lambda_q2, lambda_k2, subln_w, conv_w, a_log,
           dt_bias, gdn_norm_w, w_pa, w_pb, w_o, final_norm_w):
    assert x_prompt.shape == (BATCH, SEQ, D_MODEL) and x_sample.shape == (DEC_BATCH, 1, D_MODEL)
    assert w_in.shape == (1, D_MODEL, D_IN) and page_table.shape == (DEC_BATCH, N_PAGES)
    w = w_in[0].T
    nw = norm_w
    lams = (lambda_q1, lambda_k1, lambda_q2, lambda_k2)
    fw = final_norm_w.reshape(1, D_MODEL)
    wpa, wpb, wo = w_pa[0].astype(BF16), w_pb[0].astype(BF16), w_o[0].astype(BF16)
    cw = conv_w[0]

    def lanes8(v, off):
        return jnp.zeros((1, LANES), F32).at[0, off:off + B_HEADS].set(v)

    alog_l, dtb_l = lanes8(a_log[0], 0), lanes8(dt_bias[0], 0)
    alog_c, dtb_c = alog_l[0, 0:2 * B_HEADS].reshape(-1, 1), dtb_l[0, 0:2 * B_HEADS].reshape(-1, 1)
    gvecs = (alog_l, dtb_l, alog_c, dtb_c)

    xa = jnp.concatenate([x_sample[:, 0, :], meta_tokens,
                          jnp.zeros((AUX_ROWS - DEC_BATCH - N_META, D_MODEL), F32)], axis=0)
    pos_a = np.concatenate([np.full((DEC_BATCH,), PAST_LEN), np.arange(N_META),
                            np.zeros((AUX_ROWS - DEC_BATCH - N_META,), np.int64)])
    h_aux, w_bf, w_tail = _aux_inproj(xa, nw, w, _rope_tables(pos_a))
    hs, hm = h_aux[0:DEC_BATCH], h_aux[DEC_BATCH:DEC_BATCH + N_META]

    k_meta, v_meta = hm[:, O_KA:O_VA], hm[:, O_VA:O_ZA]
    tabs_p = _rope_tables(N_META + np.arange(SEQ))
    h_main, h_tail, k_rows_p, v_rows_p = _inproj(
        x_prompt.reshape(BATCH * SEQ, D_MODEL), nw, w_bf, w_tail, tabs_p,
        k_meta.reshape(N_META, A_HEADS, LANES), v_meta.reshape(N_META, A_HEADS, LANES))

    pad_rows = CHUNK - N_META
    x_meta = jnp.pad(hm[:, O_QKVB:O_ZB], ((pad_rows, 0), (0, 0)))
    ab_meta = jnp.pad(hm[:, O_A:O_GA], ((pad_rows, 0), (0, LANES - 2 * B_HEADS)))
    abt_meta = ab_meta[:, 0:2 * B_HEADS].T.reshape(1, 1, 2 * B_HEADS, CHUNK)
    _, s_meta = _gdn_chunks(
        x_meta, 0, ab_meta, 0, abt_meta, jnp.zeros((CHUNK, B_HEADS * B_DV), F32), 0, cw, gvecs,
        gdn_norm_w, jnp.zeros((8, B_QKV), F32), jnp.zeros((B_HEADS, B_DK, B_DV), F32), 1, 1, pad_rows)

    n_chunk = SEQ // CHUNK
    abt = h_main[:, O_A:O_GA].reshape(BATCH, n_chunk, CHUNK, 2 * B_HEADS).transpose(0, 1, 3, 2)
    heads = lambda t: t.reshape(DEC_BATCH, A_HEADS, LANES)
    q_s = heads(hs[:, O_QA:O_KA]) * (A_DH ** -0.5)
    k_s, v_s = heads(hs[:, O_KA:O_VA]), heads(hs[:, O_VA:O_ZA])
    yb, ssm_p, ya_s = _gdn_decode(
        page_table, h_main, h_main, abt, cw, gvecs, gdn_norm_w, x_meta[CHUNK - 8:CHUNK], s_meta[0],
        lams, q_s, k_s, v_s, heads(hs[:, O_ZA:O_QKVB]), subln_w, cache_k, cache_v)
    ya = _attention(lams, h_main, k_meta, v_meta, subln_w)
    y_prompt = _merge(x_prompt.reshape(BATCH * SEQ, D_MODEL), ya, yb, h_main, h_tail,
                      wpa, wpb, wo, fw, 256)
    conv_p = h_main.reshape(BATCH, SEQ, N_MAIN)[:, SEQ - (CONV_W - 1):, O_QKVB:O_ZB][None]

    rows = B_QKV // LANES
    ab_s = jnp.pad(hs[:, O_A:O_GA], ((0, 0), (0, LANES - 2 * B_HEADS))).reshape(DEC_BATCH, 1, LANES)
    yb_s, ssm_s, conv_s = _gdn_step(
        state_conv[0].reshape(DEC_BATCH, CONV_W - 1, rows, LANES),
        hs[:, O_QKVB:O_ZB].reshape(DEC_BATCH, rows, LANES), cw.reshape(CONV_W, rows, LANES),
        ab_s, alog_l, dtb_l, heads(hs[:, O_ZB:O_A]), gdn_norm_w, state_ssm[0])
    hs_tail = jnp.pad(hs[:, N_MAIN:], ((0, 0), (0, LANES - (D_IN - N_MAIN))))
    y_sample = _merge(x_sample[:, 0, :], ya_s.reshape(DEC_BATCH, -1).astype(BF16),
                      yb_s.reshape(DEC_BATCH, -1).astype(BF16), hs, hs_tail,
                      wpa, wpb, wo, fw, DEC_BATCH)

    return (y_prompt.reshape(BATCH, SEQ, D_MODEL), y_sample.reshape(DEC_BATCH, 1, D_MODEL),
            k_rows_p[None], v_rows_p[None], conv_p, ssm_p[None],
            k_s.reshape(1, DEC_BATCH, 1, A_HEADS, LANES), v_s.reshape(1, DEC_BATCH, 1, A_HEADS, LANES),
            conv_s.reshape(1, DEC_BATCH, CONV_W - 1, B_QKV), ssm_s[None])
```

```python
import functools
import math

import jax
import jax.numpy as jnp
import numpy as np
from jax import lax
from jax.experimental import pallas as pl
from jax.experimental.pallas import tpu as pltpu

F32 = jnp.float32
BF16 = jnp.bfloat16

D_MODEL = 2048
BATCH = 4
SEQ = 2048
DEC_BATCH = 32
PAST_LEN = 8192
PAGE_SIZE = 128
N_PAGES = PAST_LEN // PAGE_SIZE
N_META = 16
A_HEADS = 8
A_DH = 64
A_DV = 128
ROPE_DIM = 16
ROPE_THETA = 500000.0
B_HEADS = 8
B_DK = 128
B_DV = 128
B_QKV = 3072
CONV_W = 4
CHUNK = 64
EPS = 1e-6
NEG = -1e30
LAMBDA_INIT = 0.8 - 0.6 * math.exp(-0.3 * 0)

O_QA, O_KA, O_VA, O_ZA, O_QKVB, O_ZB, O_A, O_GA, O_GB, D_IN = (
    0, 1024, 2048, 3072, 4096, 7168, 8192, 8208, 10256, 12304)
LANES = 128
N_MAIN = (D_IN // LANES) * LANES
GATE_SHIFT = O_GA % LANES
VMEM_LIMIT = 56 * 1024 * 1024


def _dot(a, b):
    return jnp.dot(a, b, preferred_element_type=F32)


def _dot_nt(a, b):
    return lax.dot_general(a, b, (((1,), (1,)), ((), ())), preferred_element_type=F32)


def _dot_f32(a, b):
    return jnp.dot(a, b, preferred_element_type=F32, precision=lax.Precision.HIGHEST)


def _sigmoid(x):
    return 1.0 / (1.0 + jnp.exp(-x))


def _silu(x):
    return x * _sigmoid(x)


def _softplus(x):
    return jnp.maximum(x, 0.0) + jnp.log1p(jnp.exp(-jnp.abs(x)))


def _rmsnorm(x, w):
    return x * lax.rsqrt(jnp.mean(x * x, axis=-1, keepdims=True) + EPS) * w


def _rope_tables(pos):
    pos = np.asarray(pos, np.float32)
    r = pos.shape[0]
    inv_freq = np.float32(ROPE_THETA) ** (-np.arange(0, ROPE_DIM, 2, dtype=np.float32) / ROPE_DIM)
    ang = pos[:, None] * inv_freq[None, :]
    cos, sin = np.cos(ang), np.sin(ang)
    half = ROPE_DIM // 2
    rest = A_DH - ROPE_DIM
    c = np.concatenate([cos, cos, np.ones((r, rest), np.float32)], axis=1)
    sa = np.concatenate([np.zeros((r, half), np.float32), sin, np.zeros((r, rest), np.float32)], axis=1)
    sb = np.concatenate([-sin, np.zeros((r, half + rest), np.float32)], axis=1)
    return tuple(jnp.asarray(np.tile(t, (1, LANES // A_DH)).astype(np.float32)) for t in (c, sa, sb))


def _rope_tile(t, c, sa, sb):
    out = []
    for i in range(t.shape[1] // LANES):
        x = t[:, i * LANES:(i + 1) * LANES]
        out.append(x * c + pltpu.roll(x, ROPE_DIM // 2, 1) * sa
                   + pltpu.roll(x, LANES - ROPE_DIM // 2, 1) * sb)
    return jnp.concatenate(out, axis=1) if len(out) > 1 else out[0]


IP_TM = 1024
IP_TN = 1024
IP_ROWS = 256
K_TILE, V_TILE = O_KA // IP_TN, O_VA // IP_TN
assert O_VA - O_KA == IP_TN and O_ZA - O_VA == IP_TN


def _inproj_kernel(x_ref, nw_ref, w_ref, wt_ref, c_ref, sa_ref, sb_ref, km_ref, vm_ref,
                   h_ref, ht_ref, kr_ref, vr_ref, xn_ref, rows_ref, sem_ref):
    i = pl.program_id(0)
    j = pl.program_id(1)
    per_seq = SEQ // IP_TM
    seq = i // per_seq
    row0 = (i % per_seq) * IP_TM
    first_tile = (i % per_seq) == 0

    @pl.when(j == 0)
    def _():
        def body(r, carry):
            rows = pl.ds(pl.multiple_of(r * IP_ROWS, IP_ROWS), IP_ROWS)
            xn_ref[rows, :] = _rmsnorm(x_ref[rows, :], nw_ref[...]).astype(BF16)
            return carry
        lax.fori_loop(0, IP_TM // IP_ROWS, body, 0)
        ht_ref[...] = _dot_nt(xn_ref[...], wt_ref[...])

    is_rope = j < O_VA // IP_TN

    @pl.when(jnp.logical_not(is_rope))
    def _():
        h_ref[...] = _dot_nt(xn_ref[...], w_ref[...])

    @pl.when(is_rope)
    def _():
        n = IP_TM // IP_ROWS
        rows = [slice(r * IP_ROWS, (r + 1) * IP_ROWS) for r in range(n)]
        acc = _dot_nt(xn_ref[rows[0], :], w_ref[...])
        for r in range(n):
            nxt = _dot_nt(xn_ref[rows[r + 1], :], w_ref[...]) if r + 1 < n else None
            h_ref[rows[r], :] = _rope_tile(acc, c_ref[rows[r], :], sa_ref[rows[r], :], sb_ref[rows[r], :])
            acc = nxt

    def row_copies(dst_ref, k):
        return [pltpu.make_async_copy(rows_ref.at[k, :, pl.ds(h * LANES, LANES)],
                                      dst_ref.at[seq, pl.ds(N_META + row0, IP_TM), h, :], sem_ref.at[k])
                for h in range(A_HEADS)]

    def meta_copy(src_ref, dst_ref, sem):
        return pltpu.make_async_copy(src_ref, dst_ref.at[seq, pl.ds(0, N_META)], sem)

    def start_rows(meta_ref, dst_ref, k):
        def body(r, carry):
            rows = pl.ds(pl.multiple_of(r * IP_ROWS, IP_ROWS), IP_ROWS)
            rows_ref[k, rows, :] = h_ref[rows, :]
            return carry
        lax.fori_loop(0, IP_TM // IP_ROWS, body, 0)
        for cp in row_copies(dst_ref, k):
            cp.start()

        @pl.when(first_tile)
        def _():
            meta_copy(meta_ref, dst_ref, sem_ref.at[2 + k]).start()

    def wait_rows(meta_ref, dst_ref, k):
        for cp in row_copies(dst_ref, k):
            cp.wait()

        @pl.when(first_tile)
        def _():
            meta_copy(meta_ref, dst_ref, sem_ref.at[2 + k]).wait()

    @pl.when(j == K_TILE)
    def _():
        start_rows(km_ref, kr_ref, 0)

    @pl.when(j == V_TILE)
    def _():
        start_rows(vm_ref, vr_ref, 1)
        wait_rows(km_ref, kr_ref, 0)

    @pl.when(j == V_TILE + 1)
    def _():
        wait_rows(vm_ref, vr_ref, 1)


def _inproj(x, nw, w_bf, w_tail, tabs, k_meta, v_meta):
    m = x.shape[0]
    per_seq = SEQ // IP_TM
    tab_spec = pl.BlockSpec((IP_TM, LANES), lambda i, j: (i % per_seq, 0))
    meta_spec = pl.BlockSpec((N_META, A_HEADS, LANES), lambda i, j: (0, 0, 0))
    rows_shape = jax.ShapeDtypeStruct((m // SEQ, N_META + SEQ, A_HEADS, LANES), F32)
    return pl.pallas_call(
        _inproj_kernel,
        grid=(m // IP_TM, N_MAIN // IP_TN),
        in_specs=[
            pl.BlockSpec((IP_TM, D_MODEL), lambda i, j: (i, 0)),
            pl.BlockSpec((1, D_MODEL), lambda i, j: (0, 0)),
            pl.BlockSpec((IP_TN, D_MODEL), lambda i, j: (j, 0)),
            pl.BlockSpec((LANES, D_MODEL), lambda i, j: (0, 0)),
            tab_spec, tab_spec, tab_spec, meta_spec, meta_spec,
        ],
        out_specs=[
            pl.BlockSpec((IP_TM, IP_TN), lambda i, j: (i, j)),
            pl.BlockSpec((IP_TM, LANES), lambda i, j: (i, 0)),
            pl.BlockSpec(memory_space=pl.ANY), pl.BlockSpec(memory_space=pl.ANY),
        ],
        out_shape=[jax.ShapeDtypeStruct((m, N_MAIN), F32),
                   jax.ShapeDtypeStruct((m, LANES), F32), rows_shape, rows_shape],
        scratch_shapes=[pltpu.VMEM((IP_TM, D_MODEL), BF16), pltpu.VMEM((2, IP_TM, IP_TN), F32),
                        pltpu.SemaphoreType.DMA((4,))],
        compiler_params=pltpu.CompilerParams(
            dimension_semantics=("arbitrary", "arbitrary"), vmem_limit_bytes=VMEM_LIMIT),
        name="inproj",
    )(x, nw, w_bf, w_tail, *tabs, k_meta, v_meta)


AUX_ROWS = 64
AUX_TN = 512


def _aux_inproj_kernel(x_ref, nw_ref, w_ref, c_ref, sa_ref, sb_ref, o_ref, wbf_ref, wtail_ref, xs_ref):
    j = pl.program_id(0)

    @pl.when(j == 0)
    def _():
        xn = _rmsnorm(x_ref[...], nw_ref[...])
        hi = xn.astype(BF16)
        xs_ref[0:AUX_ROWS, :] = hi
        xs_ref[AUX_ROWS:2 * AUX_ROWS, :] = (xn - hi.astype(F32)).astype(BF16)

    w = w_ref[...]
    w_hi = w.astype(BF16)
    wbf_ref[...] = w_hi

    @pl.when(j == pl.num_programs(0) - 1)
    def _():
        row = lax.broadcasted_iota(jnp.int32, (LANES, D_MODEL), 0)
        wtail_ref[...] = jnp.where(row < D_IN - N_MAIN, w_hi[0:LANES, :], jnp.zeros((), BF16))

    w_lo = (w - w_hi.astype(F32)).astype(BF16)
    r1 = _dot_nt(xs_ref[...], w_hi)
    r2 = _dot_nt(xs_ref[0:AUX_ROWS, :], w_lo)
    acc = r1[0:AUX_ROWS] + (r1[AUX_ROWS:] + r2)
    is_rope = j < O_VA // AUX_TN

    @pl.when(is_rope)
    def _():
        o_ref[...] = _rope_tile(acc, c_ref[...], sa_ref[...], sb_ref[...])

    @pl.when(jnp.logical_not(is_rope))
    def _():
        o_ref[...] = acc


def _aux_inproj(xa, nw, w, tabs):
    tab_spec = pl.BlockSpec((AUX_ROWS, LANES), lambda j: (0, 0))
    return pl.pallas_call(
        _aux_inproj_kernel,
        grid=(pl.cdiv(D_IN, AUX_TN),),
        in_specs=[
            pl.BlockSpec((AUX_ROWS, D_MODEL), lambda j: (0, 0)),
            pl.BlockSpec((1, D_MODEL), lambda j: (0, 0)),
            pl.BlockSpec((AUX_TN, D_MODEL), lambda j: (j, 0)),
            tab_spec, tab_spec, tab_spec,
        ],
        out_specs=[pl.BlockSpec((AUX_ROWS, AUX_TN), lambda j: (0, j)),
                   pl.BlockSpec((AUX_TN, D_MODEL), lambda j: (j, 0)),
                   pl.BlockSpec((LANES, D_MODEL), lambda j: (0, 0))],
        out_shape=[jax.ShapeDtypeStruct((AUX_ROWS, D_IN), F32),
                   jax.ShapeDtypeStruct((D_IN, D_MODEL), BF16),
                   jax.ShapeDtypeStruct((LANES, D_MODEL), BF16)],
        scratch_shapes=[pltpu.VMEM((2 * AUX_ROWS, D_MODEL), BF16)],
        compiler_params=pltpu.CompilerParams(
            dimension_semantics=("arbitrary",), vmem_limit_bytes=VMEM_LIMIT),
        name="aux_inproj",
    )(xa, nw, w, *tabs)


AT_TQ = 512
AT_TK = 512
AT_HP = 2
LOG2E = math.log2(math.e)


def _diff_lambda(lq1, lk1, lq2, lk2):
    a = jnp.exp(jnp.sum(lq1 * lk1, axis=-1, keepdims=True))
    b = jnp.exp(jnp.sum(lq2 * lk2, axis=-1, keepdims=True))
    return a - b + LAMBDA_INIT


def _attn_kernel(lq1_ref, lk1_ref, lq2_ref, lk2_ref, q_ref, k_ref, v_ref, km_ref, vm_ref,
                 za_ref, sw_ref, o_ref, kb_ref, vb_ref, kmb_ref, vmb_ref, q2_ref, s_ref, sm_ref,
                 m_ref, l_ref, acc_ref):
    qi = pl.program_id(2)
    tq, tk = AT_TQ, AT_TK
    n_tiles = tk // LANES
    heads = range(AT_HP)
    hl = lambda h: slice(h * LANES, (h + 1) * LANES)

    @pl.when(qi == 0)
    def _():
        lane_m = lax.broadcasted_iota(jnp.int32, (N_META, LANES), 1)
        zeros_kv = jnp.zeros((N_META, LANES), BF16)
        for h in heads:
            kb_ref[h] = k_ref[:, hl(h)].astype(BF16)
            vb_ref[h] = v_ref[:, hl(h)].astype(BF16)
            km = km_ref[:, hl(h)]
            vm = vm_ref[:, hl(h)].astype(BF16)
            kmb_ref[h] = jnp.concatenate(
                [jnp.where(lane_m < A_DH, km, 0.0).astype(BF16),
                 jnp.where(lane_m >= A_DH, km, 0.0).astype(BF16),
                 jnp.zeros((LANES - 2 * N_META, LANES), BF16)], axis=0)
            vmb_ref[h] = jnp.concatenate(
                [jnp.concatenate([vm, zeros_kv], axis=1), jnp.concatenate([zeros_kv, vm], axis=1),
                 jnp.zeros((LANES - 2 * N_META, 2 * LANES), BF16)], axis=0)

    lane = lax.broadcasted_iota(jnp.int32, (tq, LANES), 1)
    qb = []
    for h in heads:
        q = q_ref[:, hl(h)] * (A_DH ** -0.5 * LOG2E)
        qb.append(q.astype(BF16))
        q2_ref[h, 0:tq, :] = jnp.where(lane < A_DH, q, 0.0).astype(BF16)
        q2_ref[h, tq:2 * tq, :] = jnp.where(lane >= A_DH, q, 0.0).astype(BF16)

    def tile_max(m, s):
        for c in range(s.shape[1] // LANES):
            m = jnp.maximum(m, s[:, c * LANES:(c + 1) * LANES])
        return m

    rm = [_dot_nt(qb[h], kmb_ref[h]) for h in heads]
    for h in heads:
        top = jnp.where(lane < N_META, rm[h], NEG)
        bot = jnp.where((lane >= N_META) & (lane < 2 * N_META), rm[h], NEG)
        sm_ref[h, 0:tq, :] = top
        sm_ref[h, tq:2 * tq, :] = bot
        m_ref[h, 0:tq, :] = top
        m_ref[h, tq:2 * tq, :] = bot

    def scores(j):
        rows = pl.ds(pl.multiple_of(j * tk, tk), tk)
        return [_dot_nt(q2_ref[h], kb_ref[h, rows, :]) for h in heads]

    def pass1(j, carry):
        s = scores(j)
        for h in heads:
            s_ref[h, j] = s[h]
            m_ref[h] = tile_max(m_ref[h], s[h])
        return carry
    lax.fori_loop(0, qi, pass1, 0)

    r = lax.broadcasted_iota(jnp.int32, (2 * tq, tk), 0)
    r = jnp.where(r >= tq, r - tq, r)
    c = lax.broadcasted_iota(jnp.int32, (2 * tq, tk), 1)
    causal = c <= r
    s = [jnp.where(causal, t, NEG) for t in scores(qi)]
    for h in heads:
        s_ref[h, qi] = s[h]
    m = [jnp.max(tile_max(m_ref[h], s[h]), axis=-1, keepdims=True) for h in heads]
    for h in heads:
        m_ref[h] = jnp.broadcast_to(m[h], (2 * tq, LANES))

    p = [jnp.exp2(sm_ref[h] - m_ref[h]) for h in heads]
    om = [_dot((p[h][0:tq] + p[h][tq:2 * tq]).astype(BF16), vmb_ref[h]) for h in heads]
    for h in heads:
        l_ref[h] = p[h]
        acc_ref[h, 0:tq, :] = om[h][:, 0:LANES]
        acc_ref[h, tq:2 * tq, :] = om[h][:, LANES:]

    def pass2(j, carry):
        rows = pl.ds(pl.multiple_of(j * tk, tk), tk)
        pb = []
        for h in heads:
            mb = m_ref[h]
            lsum = l_ref[h]
            ps = []
            for t in range(n_tiles):
                p = jnp.exp2(s_ref[h, j, :, t * LANES:(t + 1) * LANES] - mb)
                lsum = lsum + p
                ps.append(p.astype(BF16))
            l_ref[h] = lsum
            pb.append(jnp.concatenate(ps, axis=1))
        pv = [_dot(pb[h], vb_ref[h, rows, :]) for h in heads]
        for h in heads:
            acc_ref[h] += pv[h]
        return carry
    lax.fori_loop(0, qi + 1, pass2, 0)

    lam = _diff_lambda(lq1_ref[...], lk1_ref[...], lq2_ref[...], lk2_ref[...])
    for h in heads:
        l = jnp.sum(l_ref[h], axis=-1, keepdims=True)
        o = acc_ref[h, 0:tq, :] / l[0:tq] - lam * (acc_ref[h, tq:2 * tq, :] / l[tq:2 * tq])
        y = _rmsnorm(o, sw_ref[...]) * (1.0 - LAMBDA_INIT)
        o_ref[:, hl(h)] = (y * _silu(za_ref[:, hl(h)])).astype(BF16)


def _attention(lams, h_main, k_meta, v_meta, subln_w):
    nq = SEQ // AT_TQ
    w = AT_HP * LANES
    lam_spec = pl.BlockSpec((1, A_DH), lambda b, g, i: (0, 0))
    return pl.pallas_call(
        _attn_kernel,
        grid=(BATCH, A_HEADS // AT_HP, nq),
        in_specs=[
            lam_spec, lam_spec, lam_spec, lam_spec,
            pl.BlockSpec((AT_TQ, w), lambda b, g, i: (b * nq + i, O_QA // w + g)),
            pl.BlockSpec((SEQ, w), lambda b, g, i: (b, O_KA // w + g)),
            pl.BlockSpec((SEQ, w), lambda b, g, i: (b, O_VA // w + g)),
            pl.BlockSpec((N_META, w), lambda b, g, i: (0, g)),
            pl.BlockSpec((N_META, w), lambda b, g, i: (0, g)),
            pl.BlockSpec((AT_TQ, w), lambda b, g, i: (b * nq + i, O_ZA // w + g)),
            pl.BlockSpec((1, A_DV), lambda b, g, i: (0, 0)),
        ],
        out_specs=pl.BlockSpec((AT_TQ, w), lambda b, g, i: (b * nq + i, g)),
        out_shape=jax.ShapeDtypeStruct((BATCH * SEQ, A_HEADS * A_DV), BF16),
        scratch_shapes=[
            pltpu.VMEM((AT_HP, SEQ, LANES), BF16), pltpu.VMEM((AT_HP, SEQ, LANES), BF16),
            pltpu.VMEM((AT_HP, LANES, LANES), BF16), pltpu.VMEM((AT_HP, LANES, 2 * LANES), BF16),
            pltpu.VMEM((AT_HP, 2 * AT_TQ, LANES), BF16),
            pltpu.VMEM((AT_HP, SEQ // AT_TK, 2 * AT_TQ, AT_TK), F32),
            pltpu.VMEM((AT_HP, 2 * AT_TQ, LANES), F32),
            pltpu.VMEM((AT_HP, 2 * AT_TQ, LANES), F32), pltpu.VMEM((AT_HP, 2 * AT_TQ, LANES), F32),
            pltpu.VMEM((AT_HP, 2 * AT_TQ, A_DV), F32),
        ],
        compiler_params=pltpu.CompilerParams(
            dimension_semantics=("arbitrary", "arbitrary", "arbitrary"),
            vmem_limit_bytes=VMEM_LIMIT),
        name="diff_attn",
    )(*lams, h_main, h_main, h_main, k_meta, v_meta, h_main, subln_w)


def _unit_lower_inverse_minus_eye(a_list, i_idx, j_idx):
    base = 8
    diag = (i_idx // base) == (j_idx // base)
    b = [jnp.where(diag, a, 0.0) for a in a_list]
    n = [-x for x in b]
    for _ in range(2):
        bb = [x.astype(BF16) for x in b]
        b = [_dot(x, x) for x in bb]
        nb = [_dot(x.astype(BF16), y.astype(BF16)) for x, y in zip(n, b)]
        n = [x + y + z for x, y, z in zip(n, b, nb)]
    s = base
    while s < CHUNK:
        join = ((i_idx // (2 * s)) == (j_idx // (2 * s))) & ((i_idx // s) % 2 == 1) & ((j_idx // s) % 2 == 0)
        a_s = [jnp.where(join, a, 0.0) for a in a_list]
        x = [p + _dot(q.astype(BF16), p.astype(BF16)) for p, q in zip(a_s, n)]
        xn = [_dot(p.astype(BF16), q.astype(BF16)) for p, q in zip(x, n)]
        n = [q - (p + r) for q, p, r in zip(n, x, xn)]
        s *= 2
    return n


def _gdn_kernel(xq_ref, xk_ref, xv_ref, ab_ref, abt_ref, zb_ref, cw_ref, alog_ref, dtb_ref,
                alogc_ref, dtbc_ref, gw_ref, halo_ref, s0_ref, yb_ref, sfin_ref, ext_ref, s_ref,
                *, masked_rows):
    c = pl.program_id(1)
    hist = 8

    @pl.when(c == 0)
    def _():
        ext_ref[0:hist, :] = halo_ref[...]
        s_ref[...] = s0_ref[...]

    x = jnp.concatenate([xq_ref[...], xk_ref[...], xv_ref[...]], axis=1)
    ext_ref[hist:hist + CHUNK, :] = x
    y = cw_ref[CONV_W - 1:CONV_W, :] * x
    for t in range(CONV_W - 1):
        lo = hist - (CONV_W - 1) + t
        y = y + cw_ref[t:t + 1, :] * ext_ref[lo:lo + CHUNK, :]
    y = _silu(y)
    ext_ref[0:hist, :] = x[CHUNK - hist:CHUNK, :]

    i_idx = lax.broadcasted_iota(jnp.int32, (CHUNK, CHUNK), 0)
    j_idx = lax.broadcasted_iota(jnp.int32, (CHUNK, CHUNK), 1)
    tril = i_idx >= j_idx
    strict = i_idx > j_idx
    tril_f = tril.astype(F32)
    triu_f = (i_idx <= j_idx).astype(F32)

    ab = ab_ref[...]
    g_c = -jnp.exp(alog_ref[...]) * _softplus(ab + dtb_ref[...])
    beta_c = _sigmoid(ab)
    abt = abt_ref[...]
    g_r = -jnp.exp(alogc_ref[...]) * _softplus(abt + dtbc_ref[...])
    if masked_rows:
        row_ok = lax.broadcasted_iota(jnp.int32, (CHUNK, LANES), 0) >= masked_rows
        col_ok = lax.broadcasted_iota(jnp.int32, (2 * B_HEADS, CHUNK), 1) >= masked_rows
        g_c = jnp.where(row_ok, g_c, 0.0)
        beta_c = jnp.where(row_ok, beta_c, 0.0)
        g_r = jnp.where(col_ok, g_r, 0.0)
    gc_c = _dot_f32(tril_f, g_c)
    gc_r = _dot_f32(g_r, triu_f)

    nk = B_HEADS * B_DK
    heads = range(B_HEADS)

    def l2n(t):
        return t * lax.rsqrt(jnp.sum(t * t, axis=-1, keepdims=True) + EPS)

    qn = [l2n(y[:, h * B_DK:(h + 1) * B_DK]) * (B_DK ** -0.5) for h in heads]
    kn = [l2n(y[:, nk + h * B_DK:nk + (h + 1) * B_DK]) for h in heads]
    vh = [y[:, 2 * nk + h * B_DV:2 * nk + (h + 1) * B_DV] for h in heads]
    bcol = [beta_c[:, B_HEADS + h:B_HEADS + h + 1] for h in heads]
    gcc = [gc_c[:, h:h + 1] for h in heads]
    decay = [jnp.where(tril, jnp.exp(jnp.where(tril, gcc[h] - gc_r[h:h + 1, :], 0.0)), 0.0)
             for h in heads]
    kbeta = [kn[h] * bcol[h] for h in heads]
    kn_b = [t.astype(BF16) for t in kn]
    kk = [_dot_nt(kbeta[h].astype(BF16), kn_b[h]) for h in heads]
    qk = [_dot_nt(qn[h].astype(BF16), kn_b[h]) for h in heads]
    a = [jnp.where(strict, kk[h] * decay[h], 0.0) for h in heads]
    n = _unit_lower_inverse_minus_eye(a, i_idx, j_idx)
    egc = [jnp.exp(t) for t in gcc]
    rhs = [jnp.concatenate([vh[h] * bcol[h], kbeta[h] * egc[h]], axis=1) for h in heads]
    nr = [_dot(n[h].astype(BF16), rhs[h].astype(BF16)) for h in heads]
    sol = [rhs[h] + nr[h] for h in heads]
    st = [s_ref[h] for h in heads]
    st_b = [t.astype(BF16) for t in st]
    ws = [_dot(sol[h][:, B_DV:B_DV + B_DK].astype(BF16), st_b[h]) for h in heads]
    qs = [_dot((qn[h] * egc[h]).astype(BF16), st_b[h]) for h in heads]
    v_new_b = [(sol[h][:, 0:B_DV] - ws[h]).astype(BF16) for h in heads]
    av = [_dot((qk[h] * decay[h]).astype(BF16), v_new_b[h]) for h in heads]
    g_last = [t[CHUNK - 1:CHUNK, :] for t in gcc]
    ke_t = [(kn[h] * jnp.exp(g_last[h] - gcc[h])).T.astype(BF16) for h in heads]
    kv = [_dot(ke_t[h], v_new_b[h]) for h in heads]
    for h in heads:
        s_ref[h] = st[h] * jnp.exp(g_last[h]) + kv[h]
    for h in heads:
        zb = zb_ref[:, h * B_DV:(h + 1) * B_DV]
        yb_ref[:, h * B_DV:(h + 1) * B_DV] = (_rmsnorm(qs[h] + av[h], gw_ref[...]) * _silu(zb)).astype(BF16)

    @pl.when(c == pl.num_programs(1) - 1)
    def _():
        sfin_ref[...] = s_ref[...]


def _gdn_chunks(x, x_col, ab, ab_col, abt, zsrc, zb_col, cw, gvecs, gw, halo, s0, n_seq, n_chunk,
                masked_rows):
    alog_l, dtb_l, alog_c, dtb_c = gvecs
    const2 = lambda b, c: (0, 0)
    kern = functools.partial(_gdn_kernel, masked_rows=masked_rows)
    part = B_QKV // 3

    def x_spec(k):
        return pl.BlockSpec((CHUNK, part), lambda b, c: (b * n_chunk + c, x_col + k))

    return pl.pallas_call(
        kern,
        grid=(n_seq, n_chunk),
        in_specs=[
            x_spec(0), x_spec(1), x_spec(2),
            pl.BlockSpec((CHUNK, LANES), lambda b, c: (b * n_chunk + c, ab_col)),
            pl.BlockSpec((None, None, 2 * B_HEADS, CHUNK), lambda b, c: (b, c, 0, 0)),
            pl.BlockSpec((CHUNK, B_HEADS * B_DV), lambda b, c: (b * n_chunk + c, zb_col)),
            pl.BlockSpec((CONV_W, B_QKV), const2),
            pl.BlockSpec((1, LANES), const2), pl.BlockSpec((1, LANES), const2),
            pl.BlockSpec((2 * B_HEADS, 1), const2), pl.BlockSpec((2 * B_HEADS, 1), const2),
            pl.BlockSpec((1, B_DV), const2),
            pl.BlockSpec((8, B_QKV), const2),
            pl.BlockSpec((B_HEADS, B_DK, B_DV), lambda b, c: (0, 0, 0)),
        ],
        out_specs=[
            pl.BlockSpec((CHUNK, B_HEADS * B_DV), lambda b, c: (b * n_chunk + c, 0)),
            pl.BlockSpec((None, B_HEADS, B_DK, B_DV), lambda b, c: (b, 0, 0, 0)),
        ],
        out_shape=[jax.ShapeDtypeStruct((n_seq * n_chunk * CHUNK, B_HEADS * B_DV), BF16),
                   jax.ShapeDtypeStruct((n_seq, B_HEADS, B_DK, B_DV), F32)],
        scratch_shapes=[pltpu.VMEM((8 + CHUNK, B_QKV), F32),
                        pltpu.VMEM((B_HEADS, B_DK, B_DV), F32)],
        compiler_params=pltpu.CompilerParams(
            dimension_semantics=("arbitrary", "arbitrary"), vmem_limit_bytes=VMEM_LIMIT),
        name="gdn_chunks",
    )(x, x, x, ab, abt, zsrc, cw, alog_l, dtb_l, alog_c, dtb_c, gw, halo, s0)


def _shift_lanes(x, k):
    nblk = x.shape[1] // LANES
    r = [pltpu.roll(x[:, c * LANES:(c + 1) * LANES], LANES - k, 1) for c in range(nblk)]
    lane = lax.broadcasted_iota(jnp.int32, (x.shape[0], LANES), 1)
    return jnp.concatenate([jnp.where(lane < LANES - k, r[c], r[c + 1]) for c in range(nblk - 1)],
                           axis=1)


def _merge_kernel(x_ref, ya_ref, yb_ref, g4_ref, g5_ref, gt_ref, wpa_ref, wpb_ref, wo_ref, fw_ref,
                  y_ref):
    g5 = g5_ref[...]
    ga = _shift_lanes(jnp.concatenate([g4_ref[...], g5[:, 0:LANES]], axis=1), GATE_SHIFT)
    gb = _shift_lanes(jnp.concatenate([g5, gt_ref[...]], axis=1), GATE_SHIFT)
    pa = _dot(ya_ref[...], wpa_ref[...])
    pb = _dot(yb_ref[...], wpb_ref[...])
    mixed = _sigmoid(ga) * pa + _sigmoid(gb) * pb
    hp = x_ref[...] + _dot(mixed.astype(BF16), wo_ref[...])
    y_ref[...] = _rmsnorm(hp, fw_ref[...])


def _merge(x, ya, yb, hsrc, htail, wpa, wpb, wo, fw, tm):
    m = x.shape[0]
    once = pl.Buffered(1)
    return pl.pallas_call(
        _merge_kernel,
        grid=(m // tm,),
        in_specs=[
            pl.BlockSpec((tm, D_MODEL), lambda i: (i, 0)),
            pl.BlockSpec((tm, A_HEADS * A_DV), lambda i: (i, 0)),
            pl.BlockSpec((tm, B_HEADS * B_DV), lambda i: (i, 0)),
            pl.BlockSpec((tm, D_MODEL), lambda i: (i, O_A // D_MODEL)),
            pl.BlockSpec((tm, D_MODEL), lambda i: (i, O_A // D_MODEL + 1)),
            pl.BlockSpec((tm, LANES), lambda i: (i, 0)),
            pl.BlockSpec((A_HEADS * A_DV, D_MODEL), lambda i: (0, 0), pipeline_mode=once),
            pl.BlockSpec((B_HEADS * B_DV, D_MODEL), lambda i: (0, 0), pipeline_mode=once),
            pl.BlockSpec((D_MODEL, D_MODEL), lambda i: (0, 0), pipeline_mode=once),
            pl.BlockSpec((1, D_MODEL), lambda i: (0, 0)),
        ],
        out_specs=pl.BlockSpec((tm, D_MODEL), lambda i: (i, 0)),
        out_shape=jax.ShapeDtypeStruct((m, D_MODEL), F32),
        compiler_params=pltpu.CompilerParams(
            dimension_semantics=("arbitrary",), vmem_limit_bytes=VMEM_LIMIT),
        name="merge",
    )(x, ya, yb, hsrc, hsrc, htail, wpa, wpb, wo, fw)


DA_GRP = 8
HALF_TOK = PAGE_SIZE // 2
HALF_ROWS = HALF_TOK * A_HEADS
TOK_TILES = HALF_ROWS // LANES


def _lane_group_reduce(x, op):
    s = A_HEADS
    while s < LANES:
        x = op(x, pltpu.roll(x, s, 1))
        s *= 2
    return x


FD_PAGES = N_PAGES // 2
FD_STEPS = 4
FD_NBUF = 2 * FD_PAGES


def _gdn_decode_kernel(pt_ref, xq_ref, xk_ref, xv_ref, ab_ref, abt_ref, zb_ref, cw_ref, alog_ref,
                       dtb_ref, alogc_ref, dtbc_ref, gw_ref, halo_ref, s0_ref,
                       lq1_ref, lk1_ref, lq2_ref, lk2_ref, q_ref, kn_ref, vn_ref, za_ref, sw_ref,
                       ck_ref, cv_ref, yb_ref, sfin_ref, o_ref,
                       ext_ref, s_ref, buf_ref, sem_ref, sc_ref, acc_ref, wn_ref):
    n_chunk = pl.num_programs(1)
    t = pl.program_id(0) * n_chunk + pl.program_id(1)
    n_steps = pl.num_programs(0) * n_chunk
    role = t % FD_STEPS

    def start_step(step):
        seq = step // FD_STEPS
        r = step % FD_STEPS
        half = (step % 2) * FD_PAGES
        first = (r % 2) * FD_PAGES

        def copies(src_ref):
            for s in range(FD_PAGES):
                pltpu.make_async_copy(src_ref.at[0, pt_ref[seq, first + s]], buf_ref.at[half + s],
                                      sem_ref.at[half + s]).start(priority=s % 2)

        @pl.when(r < 2)
        def _():
            copies(ck_ref)

        @pl.when(r >= 2)
        def _():
            copies(cv_ref)

    def wait_slot(slot):
        pltpu.make_async_copy(ck_ref.at[0, 0], buf_ref.at[slot], sem_ref.at[slot]).wait()

    @pl.when(t == 0)
    def _():
        start_step(0)

    @pl.when(t + 1 < n_steps)
    def _():
        start_step(t + 1)

    _gdn_kernel(xq_ref, xk_ref, xv_ref, ab_ref, abt_ref, zb_ref, cw_ref, alog_ref, dtb_ref,
                alogc_ref, dtbc_ref, gw_ref, halo_ref, s0_ref, yb_ref, sfin_ref, ext_ref, s_ref,
                masked_rows=0)

    half = (t % 2) * FD_PAGES
    sub = lax.broadcasted_iota(jnp.int32, (A_HEADS, LANES), 0)
    lane = lax.broadcasted_iota(jnp.int32, (A_HEADS, LANES), 1)
    hmask = (lane % A_HEADS) == sub
    first_half = sub < 2
    second_half = (sub >= 2) & (sub < 4)
    map0 = (sub % 2) == 0
    sub2 = lax.broadcasted_iota(jnp.int32, (A_HEADS, 2 * LANES), 0)
    lane2 = lax.broadcasted_iota(jnp.int32, (A_HEADS, 2 * LANES), 1)
    half_sel = ((sub2 < 4) & (lane2 // A_DH == sub2)).astype(BF16)
    n_grp = FD_PAGES // DA_GRP

    def halves_on_lanes(page):
        return jnp.concatenate([page[0:HALF_TOK].reshape(HALF_ROWS, LANES),
                                page[HALF_TOK:].reshape(HALF_ROWS, LANES)], axis=1)

    def pair_halves(x, op):
        up = pltpu.roll(x, 2, 0)
        dn = pltpu.roll(x, A_HEADS - 2, 0)
        return jnp.where(first_half, op(x, dn), jnp.where(second_half, op(x, up), 0.0))

    @pl.when(role < 2)
    def _():
        q = q_ref[...]

        def k_group(g, carry):
            slots = [half + g * DA_GRP + i for i in range(DA_GRP)]
            for s in slots:
                wait_slot(s)
            prods = [halves_on_lanes(buf_ref[s] * q[None]).astype(BF16) for s in slots]
            scs = [_dot_nt(half_sel, p) for p in prods]
            for i, sc in enumerate(scs):
                base = (role * FD_PAGES + g * DA_GRP + i) * TOK_TILES
                for c in range(TOK_TILES):
                    sc_ref[base + c] = sc[:, c * LANES:(c + 1) * LANES]
            return carry
        lax.fori_loop(0, n_grp, k_group, 0)

    @pl.when(role == 1)
    def _():
        lam = _diff_lambda(lq1_ref[...], lk1_ref[...], lq2_ref[...], lk2_ref[...])
        prod = q_ref[...] * kn_ref[...]
        hs1 = jnp.sum(jnp.where(lane < A_DH, prod, 0.0), axis=1, keepdims=True)
        hs2 = jnp.sum(jnp.where(lane >= A_DH, prod, 0.0), axis=1, keepdims=True)
        row1 = jnp.sum(jnp.where(hmask, hs1, 0.0), axis=0, keepdims=True)
        row2 = jnp.sum(jnp.where(hmask, hs2, 0.0), axis=0, keepdims=True)
        s_new = jnp.where(sub < 4, jnp.where(map0, row1, row2), 0.0)
        sc = sc_ref[...]
        mx = pair_halves(_lane_group_reduce(jnp.max(sc, axis=0), jnp.maximum), jnp.maximum)
        mx = jnp.maximum(mx, s_new)
        p = jnp.exp(sc - mx[None])
        p_new = jnp.exp(s_new - mx)
        den = pair_halves(_lane_group_reduce(jnp.sum(p, axis=0), jnp.add), jnp.add) + p_new
        coef = jnp.where(sub < 4, jnp.where(map0, 1.0 / den, -lam / den), 0.0)
        sc_ref[...] = p * coef[None]
        wn_ref[...] = jnp.sum(jnp.where(first_half, p_new * coef, 0.0), axis=0, keepdims=True)
        acc_ref[...] = jnp.zeros_like(acc_ref)

    @pl.when(role >= 2)
    def _():
        def page_weights(page):
            wa, wb = [], []
            for c in range(TOK_TILES):
                tile = sc_ref[page * TOK_TILES + c]
                ra = jnp.sum(jnp.where(first_half, tile, 0.0), axis=0, keepdims=True)
                rb = jnp.sum(jnp.where(second_half, tile, 0.0), axis=0, keepdims=True)
                wa.append(jnp.where(hmask, ra, 0.0))
                wb.append(jnp.where(hmask, rb, 0.0))
            w = jnp.concatenate([jnp.concatenate(wa, axis=1), jnp.concatenate(wb, axis=1)], axis=0)
            w_hi = w.astype(BF16)
            w_lo = (w - w_hi.astype(F32)).astype(BF16)
            return jnp.concatenate([w_hi, w_lo], axis=0)

        def v_group(g, acc):
            first = (role - 2) * FD_PAGES + g * DA_GRP
            ws = [page_weights(first + i) for i in range(DA_GRP)]
            slots = [half + g * DA_GRP + i for i in range(DA_GRP)]
            for s in slots:
                wait_slot(s)
            vs = [halves_on_lanes(buf_ref[s]).astype(BF16) for s in slots]
            rs = [_dot(w, v) for w, v in zip(ws, vs)]
            for r in rs:
                acc = acc + ((r[0:8, 0:LANES] + r[8:16, LANES:]) + (r[16:24, 0:LANES] + r[24:32, LANES:]))
            return acc
        acc_ref[...] = lax.fori_loop(0, n_grp, v_group, acc_ref[...])

    @pl.when(role == FD_STEPS - 1)
    def _():
        w_new = jnp.sum(jnp.where(hmask, wn_ref[...], 0.0), axis=1, keepdims=True) * (A_HEADS / LANES)
        o = acc_ref[...] + w_new * vn_ref[...]
        y = _rmsnorm(o, sw_ref[...]) * (1.0 - LAMBDA_INIT)
        o_ref[...] = y * _silu(za_ref[...])


def _gdn_decode(page_table, x, ab_src, abt, cw, gvecs, gw, halo, s0,
                lams, q_s, k_new, v_new, za_s, subln_w, cache_k, cache_v):
    alog_l, dtb_l, alog_c, dtb_c = gvecs
    n_chunk = SEQ // CHUNK
    assert BATCH * n_chunk == DEC_BATCH * FD_STEPS
    part = B_QKV // 3
    row = lambda b, c, pt: (b * n_chunk + c, 0)
    const2 = lambda b, c, pt: (0, 0)
    seq3 = lambda b, c, pt: ((b * n_chunk + c) // FD_STEPS, 0, 0)

    def col(width, k):
        return pl.BlockSpec((CHUNK, width), lambda b, c, pt: (b * n_chunk + c, k))

    lam_spec = pl.BlockSpec((1, A_DH), const2)
    seq_spec = pl.BlockSpec((None, A_HEADS, LANES), seq3)
    hbm_spec = pl.BlockSpec(memory_space=pl.ANY)
    grid_spec = pltpu.PrefetchScalarGridSpec(
        num_scalar_prefetch=1,
        grid=(BATCH, n_chunk),
        in_specs=[
            col(part, O_QKVB // part), col(part, O_QKVB // part + 1), col(part, O_QKVB // part + 2),
            col(LANES, O_A // LANES),
            pl.BlockSpec((None, None, 2 * B_HEADS, CHUNK), lambda b, c, pt: (b, c, 0, 0)),
            col(part, O_ZB // part),
            pl.BlockSpec((CONV_W, B_QKV), const2),
            pl.BlockSpec((1, LANES), const2), pl.BlockSpec((1, LANES), const2),
            pl.BlockSpec((2 * B_HEADS, 1), const2), pl.BlockSpec((2 * B_HEADS, 1), const2),
            pl.BlockSpec((1, B_DV), const2),
            pl.BlockSpec((8, B_QKV), const2),
            pl.BlockSpec((B_HEADS, B_DK, B_DV), lambda b, c, pt: (0, 0, 0)),
            lam_spec, lam_spec, lam_spec, lam_spec, seq_spec, seq_spec, seq_spec, seq_spec,
            pl.BlockSpec((1, A_DV), const2), hbm_spec, hbm_spec,
        ],
        out_specs=[
            pl.BlockSpec((CHUNK, B_HEADS * B_DV), row),
            pl.BlockSpec((None, B_HEADS, B_DK, B_DV), lambda b, c, pt: (b, 0, 0, 0)),
            seq_spec,
        ],
        scratch_shapes=[
            pltpu.VMEM((8 + CHUNK, B_QKV), F32),
            pltpu.VMEM((B_HEADS, B_DK, B_DV), F32),
            pltpu.VMEM((FD_NBUF, PAGE_SIZE, A_HEADS, LANES), F32),
            pltpu.SemaphoreType.DMA((FD_NBUF,)),
            pltpu.VMEM((N_PAGES * TOK_TILES, A_HEADS, LANES), F32),
            pltpu.VMEM((A_HEADS, A_DV), F32),
            pltpu.VMEM((1, LANES), F32),
        ],
    )
    return pl.pallas_call(
        _gdn_decode_kernel,
        grid_spec=grid_spec,
        out_shape=[jax.ShapeDtypeStruct((BATCH * SEQ, B_HEADS * B_DV), BF16),
                   jax.ShapeDtypeStruct((BATCH, B_HEADS, B_DK, B_DV), F32),
                   jax.ShapeDtypeStruct((DEC_BATCH, A_HEADS, A_DV), F32)],
        compiler_params=pltpu.CompilerParams(
            dimension_semantics=("arbitrary", "arbitrary"), vmem_limit_bytes=VMEM_LIMIT),
        name="gdn_decode",
    )(page_table, x, x, x, ab_src, abt, x, cw, alog_l, dtb_l, alog_c, dtb_c, gw, halo, s0,
      *lams, q_s, k_new, v_new, za_s, subln_w, cache_k, cache_v)


GS_SEQS = 4


def _gdn_step_kernel(sc_ref, x_ref, cw_ref, ab_ref, alog_ref, dtb_ref, zb_ref, gw_ref, s_ref,
                     yb_ref, sout_ref, cout_ref):
    for i in range(GS_SEQS):
        x = x_ref[i]
        y = cw_ref[CONV_W - 1] * x
        for t in range(CONV_W - 1):
            y = y + cw_ref[t] * sc_ref[i, t]
        y = _silu(y)
        for t in range(CONV_W - 2):
            cout_ref[i, t] = sc_ref[i, t + 1]
        cout_ref[i, CONV_W - 2] = x

        q = y[0:B_HEADS]
        k = y[B_HEADS:2 * B_HEADS]
        v = y[2 * B_HEADS:3 * B_HEADS]
        qn = q * lax.rsqrt(jnp.sum(q * q, axis=-1, keepdims=True) + EPS) * (B_DK ** -0.5)
        kn = k * lax.rsqrt(jnp.sum(k * k, axis=-1, keepdims=True) + EPS)
        qt = qn.T
        kt = kn.T
        ab = ab_ref[i]
        g = -jnp.exp(alog_ref[...]) * _softplus(ab + dtb_ref[...])
        beta = _sigmoid(ab)
        heads = range(B_HEADS)
        kcol = [kt[:, h:h + 1] for h in heads]
        st = [s_ref[i, h] * jnp.exp(g[:, h:h + 1]) for h in heads]
        kv = [jnp.sum(st[h] * kcol[h], axis=0, keepdims=True) for h in heads]
        d = [(v[h:h + 1] - kv[h]) * beta[:, B_HEADS + h:B_HEADS + h + 1] for h in heads]
        st = [st[h] + kcol[h] * d[h] for h in heads]
        for h in heads:
            sout_ref[i, h] = st[h]
        o = [jnp.sum(st[h] * qt[:, h:h + 1], axis=0, keepdims=True) for h in heads]
        for h in heads:
            yb_ref[i, h:h + 1, :] = _rmsnorm(o[h], gw_ref[...]) * _silu(zb_ref[i, h:h + 1, :])


def _gdn_step(state_conv, x, cw, ab, alog_l, dtb_l, zb, gw, state_ssm):
    rows = B_QKV // LANES
    seq3 = lambda b: (b, 0, 0)
    seq4 = lambda b: (b, 0, 0, 0)
    const2 = lambda b: (0, 0)
    return pl.pallas_call(
        _gdn_step_kernel,
        grid=(DEC_BATCH // GS_SEQS,),
        in_specs=[
            pl.BlockSpec((GS_SEQS, CONV_W - 1, rows, LANES), seq4),
            pl.BlockSpec((GS_SEQS, rows, LANES), seq3),
            pl.BlockSpec((CONV_W, rows, LANES), lambda b: (0, 0, 0)),
            pl.BlockSpec((GS_SEQS, 1, LANES), seq3),
            pl.BlockSpec((1, LANES), const2), pl.BlockSpec((1, LANES), const2),
            pl.BlockSpec((GS_SEQS, B_HEADS, B_DV), seq3),
            pl.BlockSpec((1, B_DV), const2),
            pl.BlockSpec((GS_SEQS, B_HEADS, B_DK, B_DV), seq4),
        ],
        out_specs=[
            pl.BlockSpec((GS_SEQS, B_HEADS, B_DV), seq3),
            pl.BlockSpec((GS_SEQS, B_HEADS, B_DK, B_DV), seq4),
            pl.BlockSpec((GS_SEQS, CONV_W - 1, rows, LANES), seq4),
        ],
        out_shape=[jax.ShapeDtypeStruct((DEC_BATCH, B_HEADS, B_DV), F32),
                   jax.ShapeDtypeStruct((DEC_BATCH, B_HEADS, B_DK, B_DV), F32),
                   jax.ShapeDtypeStruct((DEC_BATCH, CONV_W - 1, rows, LANES), F32)],
        compiler_params=pltpu.CompilerParams(
            dimension_semantics=("arbitrary",), vmem_limit_bytes=VMEM_LIMIT),
        name="gdn_step",
    )(state_conv, x, cw, ab, alog_l, dtb_l, zb, gw, state_ssm)


def kernel(x_prompt, x_sample, cache_k, cache_v, state_conv, state_ssm, page_table, meta_tokens,
           norm_w, w_in, lambda_q1, lambda_k1, lambda_q2, lambda_k2, subln_w, conv_w, a_log,
           dt_bias, gdn_norm_w, w_pa, w_pb, w_o, final_norm_w):
    assert x_prompt.shape == (BATCH, SEQ, D_MODEL) and x_sample.shape == (DEC_BATCH, 1, D_MODEL)
    assert w_in.shape == (1, D_MODEL, D_IN) and page_table.shape == (DEC_BATCH, N_PAGES)
    w = w_in[0].T
    nw = norm_w
    lams = (lambda_q1, lambda_k1, lambda_q2, lambda_k2)
    fw = final_norm_w.reshape(1, D_MODEL)
    wpa, wpb, wo = w_pa[0].astype(BF16), w_pb[0].astype(BF16), w_o[0].astype(BF16)
    cw = conv_w[0]

    def lanes8(v, off):
        return jnp.zeros((1, LANES), F32).at[0, off:off + B_HEADS].set(v)

    alog_l, dtb_l = lanes8(a_log[0], 0), lanes8(dt_bias[0], 0)
    alog_c, dtb_c = alog_l[0, 0:2 * B_HEADS].reshape(-1, 1), dtb_l[0, 0:2 * B_HEADS].reshape(-1, 1)
    gvecs = (alog_l, dtb_l, alog_c, dtb_c)

    xa = jnp.concatenate([x_sample[:, 0, :], meta_tokens,
                          jnp.zeros((AUX_ROWS - DEC_BATCH - N_META, D_MODEL), F32)], axis=0)
    pos_a = np.concatenate([np.full((DEC_BATCH,), PAST_LEN), np.arange(N_META),
                            np.zeros((AUX_ROWS - DEC_BATCH - N_META,), np.int64)])
    h_aux, w_bf, w_tail = _aux_inproj(xa, nw, w, _rope_tables(pos_a))
    hs, hm = h_aux[0:DEC_BATCH], h_aux[DEC_BATCH:DEC_BATCH + N_META]

    k_meta, v_meta = hm[:, O_KA:O_VA], hm[:, O_VA:O_ZA]
    tabs_p = _rope_tables(N_META + np.arange(SEQ))
    h_main, h_tail, k_rows_p, v_rows_p = _inproj(
        x_prompt.reshape(BATCH * SEQ, D_MODEL), nw, w_bf, w_tail, tabs_p,
        k_meta.reshape(N_META, A_HEADS, LANES), v_meta.reshape(N_META, A_HEADS, LANES))

    pad_rows = CHUNK - N_META
    x_meta = jnp.pad(hm[:, O_QKVB:O_ZB], ((pad_rows, 0), (0, 0)))
    ab_meta = jnp.pad(hm[:, O_A:O_GA], ((pad_rows, 0), (0, LANES - 2 * B_HEADS)))
    abt_meta = ab_meta[:, 0:2 * B_HEADS].T.reshape(1, 1, 2 * B_HEADS, CHUNK)
    _, s_meta = _gdn_chunks(
        x_meta, 0, ab_meta, 0, abt_meta, jnp.zeros((CHUNK, B_HEADS * B_DV), F32), 0, cw, gvecs,
        gdn_norm_w, jnp.zeros((8, B_QKV), F32), jnp.zeros((B_HEADS, B_DK, B_DV), F32), 1, 1, pad_rows)

    n_chunk = SEQ // CHUNK
    abt = h_main[:, O_A:O_GA].reshape(BATCH, n_chunk, CHUNK, 2 * B_HEADS).transpose(0, 1, 3, 2)
    heads = lambda t: t.reshape(DEC_BATCH, A_HEADS, LANES)
    q_s = heads(hs[:, O_QA:O_KA]) * (A_DH ** -0.5)
    k_s, v_s = heads(hs[:, O_KA:O_VA]), heads(hs[:, O_VA:O_ZA])
    yb, ssm_p, ya_s = _gdn_decode(
        page_table, h_main, h_main, abt, cw, gvecs, gdn_norm_w, x_meta[CHUNK - 8:CHUNK], s_meta[0],
        lams, q_s, k_s, v_s, heads(hs[:, O_ZA:O_QKVB]), subln_w, cache_k, cache_v)
    ya = _attention(lams, h_main, k_meta, v_meta, subln_w)
    y_prompt = _merge(x_prompt.reshape(BATCH * SEQ, D_MODEL), ya, yb, h_main, h_tail,
                      wpa, wpb, wo, fw, 256)
    conv_p = h_main.reshape(BATCH, SEQ, N_MAIN)[:, SEQ - (CONV_W - 1):, O_QKVB:O_ZB][None]

    rows = B_QKV // LANES
    ab_s = jnp.pad(hs[:, O_A:O_GA], ((0, 0), (0, LANES - 2 * B_HEADS))).reshape(DEC_BATCH, 1, LANES)
    yb_s, ssm_s, conv_s = _gdn_step(
        state_conv[0].reshape(DEC_BATCH, CONV_W - 1, rows, LANES),
        hs[:, O_QKVB:O_ZB].reshape(DEC_BATCH, rows, LANES), cw.reshape(CONV_W, rows, LANES),
        ab_s, alog_l, dtb_l, heads(hs[:, O_ZB:O_A]), gdn_norm_w, state_ssm[0])
    hs_tail = jnp.pad(hs[:, N_MAIN:], ((0, 0), (0, LANES - (D_IN - N_MAIN))))
    y_sample = _merge(x_sample[:, 0, :], ya_s.reshape(DEC_BATCH, -1).astype(BF16),
                      yb_s.reshape(DEC_BATCH, -1).astype(BF16), hs, hs_tail,
                      wpa, wpb, wo, fw, DEC_BATCH)

    return (y_prompt.reshape(BATCH, SEQ, D_MODEL), y_sample.reshape(DEC_BATCH, 1, D_MODEL),
            k_rows_p[None], v_rows_p[None], conv_p, ssm_p[None],
            k_s.reshape(1, DEC_BATCH, 1, A_HEADS, LANES), v_s.reshape(1, DEC_BATCH, 1, A_HEADS, LANES),
            conv_s.reshape(1, DEC_BATCH, CONV_W - 1, B_QKV), ssm_s[None])
```
